```python
import jax, jax.numpy as jnp
from jax import lax
import numpy as np

D_MODEL = 2048
BATCH = 16
SEQ = 2048
DEPTH = 2

GRID_W = 64
CTX_LEN = 256
N_MOD = 9
FFN_DIM = 5632
HEAD_DIM = 128
HALF_ROT = HEAD_DIM // 2
ATTN_HEADS = 8
ATTN_KV_HEADS = 2
ATTN_GROUP = ATTN_HEADS // ATTN_KV_HEADS
ATTN_WIDTH = ATTN_HEADS * HEAD_DIM
KV_WIDTH = ATTN_KV_HEADS * HEAD_DIM
RET_HEADS = 4
RET_DK = 128
RET_DV = 128
RET_WIDTH = RET_HEADS * RET_DK
FOURIER_GROUPS = 4
FOURIER_GROUP_DIM = 128
FOURIER_WIDTH = FOURIER_GROUPS * FOURIER_GROUP_DIM
IN_WIDTH = ATTN_WIDTH + 2 * KV_WIDTH + 4 * RET_WIDTH + FOURIER_WIDTH
IN_SPLITS = (
    ATTN_WIDTH,
    ATTN_WIDTH + KV_WIDTH,
    ATTN_WIDTH + 2 * KV_WIDTH,
    ATTN_WIDTH + 2 * KV_WIDTH + RET_WIDTH,
    ATTN_WIDTH + 2 * KV_WIDTH + 2 * RET_WIDTH,
    ATTN_WIDTH + 2 * KV_WIDTH + 3 * RET_WIDTH,
    ATTN_WIDTH + 2 * KV_WIDTH + 4 * RET_WIDTH,
)
Q_BLOCK = 128
RET_CHUNK = 128
ROPE_THETA = 10000.0
RET_DECAY_BASE = 5
EPS = 1e-6

kernel_name = "hybrid_fourier_gqa_retention_macaron_dit"


def rmsnorm(x, gain=None):
    xf = x.astype(jnp.float32)
    y = xf * lax.rsqrt(jnp.mean(xf * xf, axis=-1, keepdims=True) + EPS)
    if gain is not None:
        y = y * gain.astype(jnp.float32)
    return y.astype(x.dtype)


def modulate(x, shift, scale):
    return x * (1 + scale) + shift


def adaln(c_act, w_ada, b_ada):
    m = (c_act @ w_ada + b_ada).reshape(c_act.shape[0], N_MOD, D_MODEL)
    return [m[:, i, None, :] for i in range(N_MOD)]


def swiglu(x, w_gate, w_up, w_down):
    return (jax.nn.silu(x @ w_gate) * (x @ w_up)) @ w_down


def axial_rope_angles(rows):
    n = rows * GRID_W
    t = jnp.arange(n)
    row = jnp.repeat(jnp.arange(rows), GRID_W).astype(jnp.float32)
    col = (t % GRID_W).astype(jnp.float32)
    inv = ROPE_THETA ** (-jnp.arange(0, HALF_ROT, 2, dtype=jnp.float32) / HALF_ROT)
    return row[:, None] * inv[None], col[:, None] * inv[None]


def _rotate(xh, ang):
    half = xh.shape[-1] // 2
    x1, x2 = xh[..., :half], xh[..., half:]
    cos = jnp.cos(ang)[None, :, None, :].astype(xh.dtype)
    sin = jnp.sin(ang)[None, :, None, :].astype(xh.dtype)
    return jnp.concatenate([x1 * cos - x2 * sin, x1 * sin + x2 * cos], axis=-1)


def apply_axial_rope(x, ang_row, ang_col):
    return jnp.concatenate([_rotate(x[..., :HALF_ROT], ang_row),
                            _rotate(x[..., HALF_ROT:], ang_col)], axis=-1)


def gqa_attend(q, k, v):
    B, L, H, hd = q.shape
    nblk = L // Q_BLOCK
    qb = q.reshape(B, nblk, Q_BLOCK, ATTN_KV_HEADS, ATTN_GROUP, hd).transpose(1, 0, 2, 3, 4, 5)
    scale = hd ** -0.5

    def block(qi):
        s = jnp.einsum('bqkgd,bskd->bkgqs', qi, k, preferred_element_type=jnp.float32) * scale
        p = jax.nn.softmax(s, axis=-1).astype(v.dtype)
        return jnp.einsum('bkgqs,bskd->bqkgd', p, v)

    o = lax.map(block, qb)
    return o.transpose(1, 0, 2, 3, 4, 5).reshape(B, L, H * hd)


def retention_chunkwise(q, k, v, log_gamma, state0):
    B, H, L, dk = q.shape
    dv = v.shape[-1]
    C = RET_CHUNK
    n = L // C
    idx = jnp.arange(C, dtype=jnp.float32)
    lg = log_gamma[:, None]
    rel = idx[:, None] - idx[None, :]
    decay_in = jnp.where(rel[None] >= 0, jnp.exp(lg[:, :, None] * jnp.maximum(rel, 0.0)[None]), 0.0)
    decay_q = jnp.exp(lg * (idx + 1.0))[None, :, :, None]
    decay_k = jnp.exp(lg * (C - 1.0 - idx))[None, :, :, None]
    decay_c = jnp.exp(log_gamma * C)[None, :, None, None]
    qs = jnp.moveaxis(q.reshape(B, H, n, C, dk), 2, 0)
    ks = jnp.moveaxis(k.reshape(B, H, n, C, dk), 2, 0)
    vs = jnp.moveaxis(v.reshape(B, H, n, C, dv), 2, 0)

    def step(state, qkv):
        qc, kc, vc = qkv
        inner = jnp.einsum('bhid,bhjd->bhij', qc, kc) * decay_in
        o = jnp.einsum('bhij,bhje->bhie', inner, vc) + jnp.einsum('bhid,bhde->bhie', qc, state) * decay_q
        state = state * decay_c + jnp.einsum('bhjd,bhje->bhde', kc * decay_k, vc)
        return state, o

    state, o = lax.scan(step, state0, (qs, ks, vs))
    return jnp.moveaxis(o, 0, 2).reshape(B, H, L, dv), state


def retention_final_state(k, v, log_gamma):
    L = k.shape[2]
    w = jnp.exp(log_gamma[:, None] * (L - 1.0 - jnp.arange(L, dtype=jnp.float32))[None])
    return jnp.einsum('bhsd,bhse,hs->bhde', k, v, w)


def _ret_heads(t):
    B, L, _ = t.shape
    return t.reshape(B, L, RET_HEADS, -1).transpose(0, 2, 1, 3).astype(jnp.float32)


def _ret_output(o, zg):
    B, H, L, dv = o.shape
    o = o * lax.rsqrt(jnp.mean(o * o, axis=-1, keepdims=True) + EPS)
    o = o.transpose(0, 2, 1, 3).reshape(B, L, H * dv).astype(zg.dtype)
    return jax.nn.silu(zg) * o


def retention_mixer(rq, rk, rv, rg, rqc, rkc, rvc, rgc, ret_decay, need_ctx):
    log_g = -jnp.exp(ret_decay.astype(jnp.float32))
    q, k, v = _ret_heads(rq), _ret_heads(rk) * RET_DK ** -0.5, _ret_heads(rv)
    qc, kc, vc = _ret_heads(rqc), _ret_heads(rkc) * RET_DK ** -0.5, _ret_heads(rvc)
    flip = lambda t: t[:, :, ::-1]
    if need_ctx:
        zeros = jnp.zeros((kc.shape[0], RET_HEADS, RET_DK, RET_DV), jnp.float32)
        oc_f, s_f = retention_chunkwise(qc, kc, vc, log_g[0], zeros)
        oc_b, s_b = retention_chunkwise(flip(qc), flip(kc), flip(vc), log_g[1], zeros)
        y_ctx = _ret_output(oc_f + flip(oc_b), rgc)
    else:
        s_f = retention_final_state(kc, vc, log_g[0])
        s_b = retention_final_state(flip(kc), flip(vc), log_g[1])
        y_ctx = None
    o_f, _ = retention_chunkwise(q, k, v, log_g[0], s_f)
    o_b, _ = retention_chunkwise(flip(q), flip(k), flip(v), log_g[1], s_b)
    return _ret_output(o_f + flip(o_b), rg), y_ctx


def fourier_mix(z):
    B, L, _ = z.shape
    zf = z.reshape(B, L, FOURIER_GROUPS, FOURIER_GROUP_DIM).astype(jnp.float32)
    y = jnp.fft.fftn(zf, axes=(1, 3), norm='ortho').real
    return y.reshape(B, L, FOURIER_WIDTH).astype(z.dtype)


def merge_branches(u, y_f, y_a, y_r, w_bf, w_ba, w_br, w_mg, b_mg, w_out):
    g_f, g_a, g_r = jnp.split(jax.nn.sigmoid(u @ w_mg + b_mg), 3, axis=-1)
    m = g_f * (y_f @ w_bf) + g_a * (y_a @ w_ba) + g_r * (y_r @ w_br)
    return m @ w_out


def token_mixer(u, uc, ang_row, ang_col, w_in, q_norm, k_norm, ret_decay,
                w_bf, w_ba, w_br, w_mg, b_mg, w_out, need_ctx):
    B, L, _ = u.shape
    Lc = uc.shape[1]
    aq, ak, av, rq, rk, rv, rg, fz = jnp.split(u @ w_in, IN_SPLITS, axis=-1)
    aqc, akc, avc, rqc, rkc, rvc, rgc, fzc = jnp.split(uc @ w_in, IN_SPLITS, axis=-1)
    q = apply_axial_rope(rmsnorm(aq.reshape(B, L, ATTN_HEADS, HEAD_DIM), q_norm), ang_row, ang_col)
    k = apply_axial_rope(rmsnorm(ak.reshape(B, L, ATTN_KV_HEADS, HEAD_DIM), k_norm), ang_row, ang_col)
    v = av.reshape(B, L, ATTN_KV_HEADS, HEAD_DIM)
    kc = rmsnorm(akc.reshape(B, Lc, ATTN_KV_HEADS, HEAD_DIM), k_norm)
    vc = avc.reshape(B, Lc, ATTN_KV_HEADS, HEAD_DIM)
    y_attn = gqa_attend(q, jnp.concatenate([kc, k], axis=1), jnp.concatenate([vc, v], axis=1))
    y_ret, y_ret_c = retention_mixer(rq, rk, rv, rg, rqc, rkc, rvc, rgc, ret_decay, need_ctx)
    y_four = fourier_mix(fz)
    out = merge_branches(u, y_four, y_attn, y_ret, w_bf, w_ba, w_br, w_mg, b_mg, w_out)
    if need_ctx:
        qc = rmsnorm(aqc.reshape(B, Lc, ATTN_HEADS, HEAD_DIM), q_norm)
        y_attn_c = gqa_attend(qc, kc, vc)
        out_c = merge_branches(uc, fourier_mix(fzc), y_attn_c, y_ret_c,
                               w_bf, w_ba, w_br, w_mg, b_mg, w_out)
    else:
        out_c = None
    return out, out_c


def setup_inputs(seed: int = 0) -> dict:
    key = jax.random.key(seed)
    ks = jax.random.split(key, 32)
    f32 = jnp.float32
    nrm = lambda k, shape, s: jax.random.normal(k, shape, f32) * s
    gain = lambda k, shape: 1.0 + 0.02 * jax.random.normal(k, shape, f32)
    L = DEPTH
    gam = 1.0 - 2.0 ** (-(RET_DECAY_BASE + jnp.arange(RET_HEADS, dtype=f32)))
    decay_base = jnp.log(-jnp.log(gam))
    return {
        'x': nrm(ks[0], (BATCH, SEQ, D_MODEL), 1.0),
        'c': nrm(ks[1], (BATCH, D_MODEL), 1.0),
        'ctx': nrm(ks[2], (BATCH, CTX_LEN, D_MODEL), 1.0),
        'c_ctx': nrm(ks[3], (D_MODEL,), 1.0),
        'w_ada': nrm(ks[4], (L, D_MODEL, N_MOD * D_MODEL), 0.5 * D_MODEL ** -0.5),
        'b_ada': nrm(ks[5], (L, N_MOD * D_MODEL), 0.01),
        'ffn1_norm': gain(ks[6], (L, D_MODEL)),
        'ffn1_w_gate': nrm(ks[7], (L, D_MODEL, FFN_DIM), D_MODEL ** -0.5),
        'ffn1_w_up': nrm(ks[8], (L, D_MODEL, FFN_DIM), D_MODEL ** -0.5),
        'ffn1_w_down': nrm(ks[9], (L, FFN_DIM, D_MODEL), FFN_DIM ** -0.5),
        'mix_norm': gain(ks[10], (L, D_MODEL)),
        'w_in': nrm(ks[11], (L, D_MODEL, IN_WIDTH), D_MODEL ** -0.5),
        'q_norm': gain(ks[12], (L, HEAD_DIM)),
        'k_norm': gain(ks[13], (L, HEAD_DIM)),
        'ret_decay': decay_base[None, None, :] + nrm(ks[14], (L, 2, RET_HEADS), 0.01),
        'w_branch_fourier': nrm(ks[15], (L, FOURIER_WIDTH, D_MODEL), FOURIER_WIDTH ** -0.5),
        'w_branch_attn': nrm(ks[16], (L, ATTN_WIDTH, D_MODEL), ATTN_WIDTH ** -0.5),
        'w_branch_ret': nrm(ks[17], (L, RET_WIDTH, D_MODEL), RET_WIDTH ** -0.5),
        'w_merge_gate': nrm(ks[18], (L, D_MODEL, 3 * D_MODEL), D_MODEL ** -0.5),
        'b_merge_gate': nrm(ks[19], (L, 3 * D_MODEL), 0.01),
        'w_out': nrm(ks[20], (L, D_MODEL, D_MODEL), D_MODEL ** -0.5),
        'ffn2_norm': gain(ks[21], (L, D_MODEL)),
        'ffn2_w_gate': nrm(ks[22], (L, D_MODEL, FFN_DIM), D_MODEL ** -0.5),
        'ffn2_w_up': nrm(ks[23], (L, D_MODEL, FFN_DIM), D_MODEL ** -0.5),
        'ffn2_w_down': nrm(ks[24], (L, FFN_DIM, D_MODEL), FFN_DIM ** -0.5),
        'final_norm': gain(ks[25], (D_MODEL,)),
    }


def reference(x, c, ctx, c_ctx, w_ada, b_ada, ffn1_norm, ffn1_w_gate, ffn1_w_up, ffn1_w_down,
              mix_norm, w_in, q_norm, k_norm, ret_decay, w_branch_fourier, w_branch_attn,
              w_branch_ret, w_merge_gate, b_merge_gate, w_out, ffn2_norm, ffn2_w_gate,
              ffn2_w_up, ffn2_w_down, final_norm):
    n_lat = x.shape[1]
    rows = n_lat // GRID_W
    ang_row, ang_col = axial_rope_angles(rows)
    c_act = jax.nn.silu(c)
    cc_act = jax.nn.silu(c_ctx)[None]
    h, hc = x, ctx
    for l in range(DEPTH):
        need_ctx = l < DEPTH - 1
        m = adaln(c_act, w_ada[l], b_ada[l])
        mc = adaln(cc_act, w_ada[l], b_ada[l])
        h = h + 0.5 * m[2] * swiglu(modulate(rmsnorm(h, ffn1_norm[l]), m[0], m[1]),
                                    ffn1_w_gate[l], ffn1_w_up[l], ffn1_w_down[l])
        hc = hc + 0.5 * mc[2] * swiglu(modulate(rmsnorm(hc, ffn1_norm[l]), mc[0], mc[1]),
                                       ffn1_w_gate[l], ffn1_w_up[l], ffn1_w_down[l])
        u = modulate(rmsnorm(h, mix_norm[l]), m[3], m[4])
        uc = modulate(rmsnorm(hc, mix_norm[l]), mc[3], mc[4])
        out, out_c = token_mixer(u, uc, ang_row, ang_col, w_in[l], q_norm[l], k_norm[l], ret_decay[l],
                                 w_branch_fourier[l], w_branch_attn[l], w_branch_ret[l],
                                 w_merge_gate[l], b_merge_gate[l], w_out[l], need_ctx)
        h = h + m[5] * out
        h = h + 0.5 * m[8] * swiglu(modulate(rmsnorm(h, ffn2_norm[l]), m[6], m[7]),
                                    ffn2_w_gate[l], ffn2_w_up[l], ffn2_w_down[l])
        if need_ctx:
            hc = hc + mc[5] * out_c
            hc = hc + 0.5 * mc[8] * swiglu(modulate(rmsnorm(hc, ffn2_norm[l]), mc[6], mc[7]),
                                           ffn2_w_gate[l], ffn2_w_up[l], ffn2_w_down[l])
    return rmsnorm(h, final_norm)
```

```python
import functools

import numpy as np
import jax
import jax.numpy as jnp
from jax import lax
from jax.experimental import pallas as pl
from jax.experimental.pallas import tpu as pltpu

F32 = jnp.float32
BF16 = jnp.bfloat16

EPS = 1e-6
N_MOD = 9
GRID_W = 64
HEAD_DIM = 128
HALF_ROT = HEAD_DIM // 2
ATTN_HEADS = 8
ATTN_KV_HEADS = 2
ATTN_GROUP = ATTN_HEADS // ATTN_KV_HEADS
ATTN_WIDTH = ATTN_HEADS * HEAD_DIM
KV_WIDTH = ATTN_KV_HEADS * HEAD_DIM
RET_HEADS = 4
RET_DK = 128
RET_WIDTH = RET_HEADS * RET_DK
FOURIER_GROUPS = 4
FOURIER_GROUP_DIM = 128
FOURIER_WIDTH = FOURIER_GROUPS * FOURIER_GROUP_DIM
IN_WIDTH = ATTN_WIDTH + 2 * KV_WIDTH + 4 * RET_WIDTH + FOURIER_WIDTH
ROPE_THETA = 10000.0

COL_AK = ATTN_WIDTH // 128
COL_AV = COL_AK + KV_WIDTH // 128
COL_RQ = COL_AV + KV_WIDTH // 128
COL_RK = COL_RQ + RET_WIDTH // 128
COL_RV = COL_RK + RET_WIDTH // 128
COL_RG = COL_RV + RET_WIDTH // 128
COL_FZ = COL_RG + RET_WIDTH // 128

V7X_VMEM_LIMIT_BYTES = 56 * 1024 * 1024


def _params(*sem):
    return pltpu.CompilerParams(dimension_semantics=sem, vmem_limit_bytes=V7X_VMEM_LIMIT_BYTES)


def _tile(n, prefs):
    for p in prefs:
        if n % p == 0:
            return p
    return n


def _dot(a, b):
    return jnp.dot(a, b, preferred_element_type=F32)


def _dot_nt(a, b):
    return lax.dot_general(a, b, (((1,), (1,)), ((), ())), preferred_element_type=F32)


def _rms(x):
    return x * lax.rsqrt(jnp.mean(x * x, axis=-1, keepdims=True) + EPS)


def _silu(x):
    return x * jax.nn.sigmoid(x)


def _adaln_kernel(c_ref, w_ref, b_ref, o_ref):
    a = _silu(c_ref[...]).astype(BF16)
    o_ref[...] = _dot(a, w_ref[...].astype(BF16)) + b_ref[...]


def _adaln(c_all, w_ada, b_ada):
    depth, d, nd = w_ada.shape
    r = c_all.shape[0]
    tn = _tile(nd, (1024, 512, 256, 128))
    return pl.pallas_call(
        _adaln_kernel,
        grid=(depth, nd // tn),
        in_specs=[
            pl.BlockSpec((r, d), lambda l, j: (0, 0)),
            pl.BlockSpec((None, d, tn), lambda l, j: (l, 0, j)),
            pl.BlockSpec((None, 1, tn), lambda l, j: (l, 0, j)),
        ],
        out_specs=pl.BlockSpec((None, r, tn), lambda l, j: (l, 0, j)),
        out_shape=jax.ShapeDtypeStruct((depth, r, nd), F32),
        compiler_params=_params("parallel", "parallel"),
    )(c_all, w_ada, b_ada.reshape(depth, 1, nd))


def _mod_spec(d, row_of_tile, k):
    return pl.BlockSpec((None, 1, d), lambda i, j: (row_of_tile(i), 0, k))


def _ffn_kernel(h_ref, sh_ref, sc_ref, gt_ref, gain_ref, wg_ref, wu_ref, wd_ref, *rest, nf, final):
    if final:
        fg_ref, o_ref, xn_ref, acc_ref = rest
    else:
        o_ref, xn_ref, acc_ref = rest
    f = pl.program_id(1)

    @pl.when(f == 0)
    def _():
        y = _rms(h_ref[...]) * gain_ref[...]
        xn_ref[...] = (y * (1.0 + sc_ref[...]) + sh_ref[...]).astype(BF16)
        acc_ref[...] = jnp.zeros_like(acc_ref)

    xn = xn_ref[...]
    g = _dot(xn, wg_ref[...])
    u = _dot(xn, wu_ref[...])
    acc_ref[...] += _dot((_silu(g) * u).astype(BF16), wd_ref[...])

    @pl.when(f == nf - 1)
    def _():
        y = h_ref[...] + (0.5 * gt_ref[...]) * acc_ref[...]
        if final:
            y = _rms(y) * fg_ref[...]
        o_ref[...] = y


def _ffn(h, mod, row_of_tile, tm, k0, gain, wg, wu, wd, layer, final_gain=None):
    n, d = h.shape
    ff = wg.shape[-1]
    tf = _tile(ff, (512, 256, 128))
    nf = ff // tf
    final = final_gain is not None
    in_specs = [
        pl.BlockSpec((tm, d), lambda i, f: (i, 0)),
        _mod_spec(d, row_of_tile, k0),
        _mod_spec(d, row_of_tile, k0 + 1),
        _mod_spec(d, row_of_tile, k0 + 2),
        pl.BlockSpec((None, 1, d), lambda i, f: (layer, 0, 0)),
        pl.BlockSpec((None, d, tf), lambda i, f: (layer, 0, f)),
        pl.BlockSpec((None, d, tf), lambda i, f: (layer, 0, f)),
        pl.BlockSpec((None, tf, d), lambda i, f: (layer, f, 0)),
    ]
    args = [h, mod, mod, mod, gain, wg, wu, wd]
    if final:
        in_specs.append(pl.BlockSpec((1, d), lambda i, f: (0, 0)))
        args.append(final_gain)
    return pl.pallas_call(
        functools.partial(_ffn_kernel, nf=nf, final=final),
        grid=(n // tm, nf),
        in_specs=in_specs,
        out_specs=pl.BlockSpec((tm, d), lambda i, f: (i, 0)),
        out_shape=jax.ShapeDtypeStruct((n, d), F32),
        scratch_shapes=[pltpu.VMEM((tm, d), BF16), pltpu.VMEM((tm, d), F32)],
        compiler_params=_params("parallel", "arbitrary"),
    )(*args)


def _inproj_kernel(h_ref, sh_ref, sc_ref, gain_ref, w_ref, o_ref, xn_ref):
    @pl.when(pl.program_id(1) == 0)
    def _():
        y = _rms(h_ref[...]) * gain_ref[...]
        xn_ref[...] = (y * (1.0 + sc_ref[...]) + sh_ref[...]).astype(BF16)

    o_ref[...] = _dot(xn_ref[...], w_ref[...])


def _inproj(h, mod, row_of_tile, tm, gain, w_in, layer):
    n, d = h.shape
    nw = w_in.shape[-1]
    tn = _tile(nw, (1024, 512))
    return pl.pallas_call(
        _inproj_kernel,
        grid=(n // tm, nw // tn),
        in_specs=[
            pl.BlockSpec((tm, d), lambda i, j: (i, 0)),
            _mod_spec(d, row_of_tile, 3),
            _mod_spec(d, row_of_tile, 4),
            pl.BlockSpec((None, 1, d), lambda i, j: (layer, 0, 0)),
            pl.BlockSpec((None, d, tn), lambda i, j: (layer, 0, j)),
        ],
        out_specs=pl.BlockSpec((tm, tn), lambda i, j: (i, j)),
        out_shape=jax.ShapeDtypeStruct((n, nw), F32),
        scratch_shapes=[pltpu.VMEM((tm, d), BF16)],
        compiler_params=_params("parallel", "arbitrary"),
    )(h, mod, mod, gain, w_in)


def _rope(x, cos, sin_lo, sin_hi):
    return x * cos + pltpu.roll(x, HEAD_DIM - HALF_ROT // 2, 1) * sin_lo + pltpu.roll(x, HALF_ROT // 2, 1) * sin_hi


def _qkprep_kernel(q_ref, kv_ref, qn_ref, kn_ref, *rest, rope):
    if rope:
        cos_ref, slo_ref, shi_ref, qo_ref, ko_ref, vo_ref = rest
        cos, slo, shi = cos_ref[...], slo_ref[...], shi_ref[...]
    else:
        qo_ref, ko_ref, vo_ref = rest
    qn, kn = qn_ref[...], kn_ref[...]
    scale = HEAD_DIM ** -0.5
    for hd in range(ATTN_HEADS):
        sl = slice(hd * HEAD_DIM, (hd + 1) * HEAD_DIM)
        y = _rms(q_ref[:, sl]) * qn
        if rope:
            y = _rope(y, cos, slo, shi)
        qo_ref[:, sl] = (y * scale).astype(BF16)
    for hd in range(ATTN_KV_HEADS):
        sl = slice(hd * HEAD_DIM, (hd + 1) * HEAD_DIM)
        y = _rms(kv_ref[:, sl]) * kn
        if rope:
            y = _rope(y, cos, slo, shi)
        ko_ref[:, sl] = y.astype(BF16)
    vo_ref[...] = kv_ref[:, KV_WIDTH:].astype(BF16)


def _qkprep(proj, seq, q_norm, k_norm, layer, tables=None):
    n = proj.shape[0]
    tr = _tile(seq, (512, 256, 128))
    rope = tables is not None
    in_specs = [
        pl.BlockSpec((tr, ATTN_WIDTH), lambda i: (i, 0)),
        pl.BlockSpec((tr, 2 * KV_WIDTH), lambda i: (i, ATTN_WIDTH // (2 * KV_WIDTH))),
        pl.BlockSpec((None, 1, HEAD_DIM), lambda i: (layer, 0, 0)),
        pl.BlockSpec((None, 1, HEAD_DIM), lambda i: (layer, 0, 0)),
    ]
    args = [proj, proj, q_norm, k_norm]
    if rope:
        nt = seq // tr
        in_specs += [pl.BlockSpec((tr, HEAD_DIM), lambda i: (i % nt, 0))] * 3
        args += list(tables)
    return pl.pallas_call(
        functools.partial(_qkprep_kernel, rope=rope),
        grid=(n // tr,),
        in_specs=in_specs,
        out_specs=[
            pl.BlockSpec((tr, ATTN_WIDTH), lambda i: (i, 0)),
            pl.BlockSpec((tr, KV_WIDTH), lambda i: (i, 0)),
            pl.BlockSpec((tr, KV_WIDTH), lambda i: (i, 0)),
        ],
        out_shape=[
            jax.ShapeDtypeStruct((n, ATTN_WIDTH), BF16),
            jax.ShapeDtypeStruct((n, KV_WIDTH), BF16),
            jax.ShapeDtypeStruct((n, KV_WIDTH), BF16),
        ],
        compiler_params=_params("parallel"),
    )(*args)


def _rope_tables(seq):
    t = jnp.arange(seq)
    row = (t // GRID_W).astype(F32)
    col = (t % GRID_W).astype(F32)
    inv = ROPE_THETA ** (-jnp.arange(0, HALF_ROT, 2, dtype=F32) / HALF_ROT)
    ar, ac = row[:, None] * inv[None], col[:, None] * inv[None]
    zero = jnp.zeros_like(ar)
    cos = jnp.concatenate([jnp.cos(ar), jnp.cos(ar), jnp.cos(ac), jnp.cos(ac)], axis=-1)
    sin_lo = jnp.concatenate([-jnp.sin(ar), zero, -jnp.sin(ac), zero], axis=-1)
    sin_hi = jnp.concatenate([zero, jnp.sin(ar), zero, jnp.sin(ac)], axis=-1)
    return cos, sin_lo, sin_hi


def _attn_kernel(q_ref, kc_ref, vc_ref, *rest, has_lat):
    if has_lat:
        kl_ref, vl_ref, o_ref = rest
        kl, vl = kl_ref[...], vl_ref[...]
    else:
        (o_ref,) = rest
    kc, vc = kc_ref[...], vc_ref[...]
    for g in range(ATTN_GROUP):
        sl = slice(g * HEAD_DIM, (g + 1) * HEAD_DIM)
        q = q_ref[:, sl]
        sc = _dot_nt(q, kc)
        m = jnp.max(sc, axis=-1, keepdims=True)
        if has_lat:
            s_lat = _dot_nt(q, kl)
            m = jnp.maximum(m, jnp.max(s_lat, axis=-1, keepdims=True))
        pc = jnp.exp(sc - m)
        den = jnp.sum(pc, axis=-1, keepdims=True)
        o = _dot(pc.astype(BF16), vc)
        if has_lat:
            p_lat = jnp.exp(s_lat - m)
            den = den + jnp.sum(p_lat, axis=-1, keepdims=True)
            o = o + _dot(p_lat.astype(BF16), vl)
        o_ref[:, sl] = (o / den).astype(BF16)


def _attention(q, kc, vc, batch, seq, ctx_len, k_lat=None, v_lat=None):
    n = q.shape[0]
    tq = _tile(seq, (256, 128))
    nq = seq // tq
    gw = ATTN_GROUP * HEAD_DIM
    has_lat = k_lat is not None
    in_specs = [
        pl.BlockSpec((tq, gw), lambda b, kv, i: (b * nq + i, kv)),
        pl.BlockSpec((ctx_len, HEAD_DIM), lambda b, kv, i: (b, kv)),
        pl.BlockSpec((ctx_len, HEAD_DIM), lambda b, kv, i: (b, kv)),
    ]
    args = [q, kc, vc]
    if has_lat:
        in_specs += [pl.BlockSpec((seq, HEAD_DIM), lambda b, kv, i: (b, kv))] * 2
        args += [k_lat, v_lat]
    return pl.pallas_call(
        functools.partial(_attn_kernel, has_lat=has_lat),
        grid=(batch, ATTN_KV_HEADS, nq),
        in_specs=in_specs,
        out_specs=pl.BlockSpec((tq, gw), lambda b, kv, i: (b * nq + i, kv)),
        out_shape=jax.ShapeDtypeStruct((n, ATTN_WIDTH), BF16),
        compiler_params=_params("parallel", "parallel", "parallel"),
    )(*args)


def _ret_kernel(lg_ref, q_ref, k_ref, v_ref, g_ref, *rest, seq, ctx_len, tq, has_ctx):
    if has_ctx:
        kc_ref, vc_ref, o_ref = rest
    else:
        (o_ref,) = rest
    hd = pl.program_id(1)
    t0 = pl.program_id(2) * tq
    lgf, lgb = lg_ref[0, hd], lg_ref[1, hd]
    scale = RET_DK ** -0.5
    q = q_ref[...].astype(BF16)
    s = _dot_nt(q, k_ref[...].astype(BF16))
    d = (t0 + lax.broadcasted_iota(jnp.int32, (tq, seq), 0) - lax.broadcasted_iota(jnp.int32, (tq, seq), 1)).astype(F32)
    w = jnp.exp(jnp.where(d >= 0.0, lgf * d, -lgb * d)) * jnp.where(d == 0.0, 2.0 * scale, scale)
    o = _dot((s * w).astype(BF16), v_ref[...].astype(BF16))
    if has_ctx:
        sc = _dot_nt(q, kc_ref[...].astype(BF16))
        t = (t0 + lax.broadcasted_iota(jnp.int32, (tq, ctx_len), 0)).astype(F32)
        j = lax.broadcasted_iota(jnp.int32, (tq, ctx_len), 1).astype(F32)
        wc = (jnp.exp(lgf * (ctx_len + t - j)) + jnp.exp(lgb * (seq - t + j))) * scale
        o = o + _dot((sc * wc).astype(BF16), vc_ref[...].astype(BF16))
    o_ref[...] = (_silu(g_ref[...]) * _rms(o)).astype(BF16)


def _retention(proj, log_gamma, batch, seq, proj_ctx=None, ctx_len=0):
    n = proj.shape[0]
    tq = _tile(seq, (512, 256, 128))
    nq = seq // tq
    has_ctx = proj_ctx is not None
    in_specs = [
        pl.BlockSpec(memory_space=pltpu.SMEM),
        pl.BlockSpec((tq, RET_DK), lambda b, h, i: (b * nq + i, COL_RQ + h)),
        pl.BlockSpec((seq, RET_DK), lambda b, h, i: (b, COL_RK + h)),
        pl.BlockSpec((seq, RET_DK), lambda b, h, i: (b, COL_RV + h)),
        pl.BlockSpec((tq, RET_DK), lambda b, h, i: (b * nq + i, COL_RG + h)),
    ]
    args = [log_gamma, proj, proj, proj, proj]
    if has_ctx:
        in_specs += [
            pl.BlockSpec((ctx_len, RET_DK), lambda b, h, i: (b, COL_RK + h)),
            pl.BlockSpec((ctx_len, RET_DK), lambda b, h, i: (b, COL_RV + h)),
        ]
        args += [proj_ctx, proj_ctx]
    return pl.pallas_call(
        functools.partial(_ret_kernel, seq=seq, ctx_len=ctx_len, tq=tq, has_ctx=has_ctx),
        grid=(batch, RET_HEADS, nq),
        in_specs=in_specs,
        out_specs=pl.BlockSpec((tq, RET_DK), lambda b, h, i: (b * nq + i, h)),
        out_shape=jax.ShapeDtypeStruct((n, RET_WIDTH), BF16),
        compiler_params=_params("parallel", "parallel", "parallel"),
    )(*args)


def _dft_cos_sin(n):
    k = np.arange(n, dtype=np.int64)
    ang = 2.0 * np.pi * ((k[:, None] * k[None, :]) % n).astype(np.float64) / n
    return np.cos(ang), np.sin(ang)


def _fourier_consts(seq):
    cg, sg = _dft_cos_sin(FOURIER_GROUP_DIM)
    norm = 1.0 / np.sqrt(float(seq) * FOURIER_GROUP_DIM)
    eye = np.eye(FOURIER_GROUPS)
    chan = np.concatenate([np.kron(eye, cg), np.kron(eye, -sg)], axis=1) * norm
    cs, ss = _dft_cos_sin(seq)
    as_bf16 = lambda a: jnp.asarray(a.astype(np.float32)).astype(BF16)
    return as_bf16(chan), as_bf16(cs), as_bf16(ss)


def _fchan_kernel(z_ref, m_ref, o_ref):
    o_ref[...] = _dot(z_ref[...].astype(BF16), m_ref[...]).astype(BF16)


def _fseq_kernel(c_ref, s_ref, zc_ref, zs_ref, o_ref):
    o_ref[...] = (_dot(c_ref[...], zc_ref[...]) + _dot(s_ref[...], zs_ref[...])).astype(BF16)


def _fourier(proj, batch, seq, consts):
    n = proj.shape[0]
    chan, cs, ss = consts
    fw = FOURIER_WIDTH
    tm = _tile(n, (1024, 512, 256, 128))
    zcs = pl.pallas_call(
        _fchan_kernel,
        grid=(n // tm,),
        in_specs=[
            pl.BlockSpec((tm, fw), lambda i: (i, COL_FZ * 128 // fw)),
            pl.BlockSpec((fw, 2 * fw), lambda i: (0, 0)),
        ],
        out_specs=pl.BlockSpec((tm, 2 * fw), lambda i: (i, 0)),
        out_shape=jax.ShapeDtypeStruct((n, 2 * fw), BF16),
        compiler_params=_params("parallel"),
    )(proj, chan)
    tk = _tile(seq, (512, 256, 128))
    nk = seq // tk
    return pl.pallas_call(
        _fseq_kernel,
        grid=(nk, batch),
        in_specs=[
            pl.BlockSpec((tk, seq), lambda k, b: (k, 0)),
            pl.BlockSpec((tk, seq), lambda k, b: (k, 0)),
            pl.BlockSpec((seq, fw), lambda k, b: (b, 0)),
            pl.BlockSpec((seq, fw), lambda k, b: (b, 1)),
        ],
        out_specs=pl.BlockSpec((tk, fw), lambda k, b: (b * nk + k, 0)),
        out_shape=jax.ShapeDtypeStruct((n, fw), BF16),
        compiler_params=_params("parallel", "parallel"),
    )(cs, ss, zcs, zcs)


def _merge_kernel(h_ref, sh_ref, sc_ref, gt_ref, gain_ref, yf_ref, ya_ref, yr_ref, wf_ref, wa_ref, wr_ref,
                  gf_ref, ga_ref, gr_ref, bf_ref, ba_ref, br_ref, wo_ref, o_ref, u_ref, acc_ref, *, nn):
    j = pl.program_id(1)

    @pl.when(j == 0)
    def _():
        y = _rms(h_ref[...]) * gain_ref[...]
        u_ref[...] = (y * (1.0 + sc_ref[...]) + sh_ref[...]).astype(BF16)
        acc_ref[...] = jnp.zeros_like(acc_ref)

    u = u_ref[...]
    m = jax.nn.sigmoid(_dot(u, gf_ref[...]) + bf_ref[...]) * _dot(yf_ref[...], wf_ref[...])
    m = m + jax.nn.sigmoid(_dot(u, ga_ref[...]) + ba_ref[...]) * _dot(ya_ref[...], wa_ref[...])
    m = m + jax.nn.sigmoid(_dot(u, gr_ref[...]) + br_ref[...]) * _dot(yr_ref[...], wr_ref[...])
    acc_ref[...] += _dot(m.astype(BF16), wo_ref[...])

    @pl.when(j == nn - 1)
    def _():
        o_ref[...] = h_ref[...] + gt_ref[...] * acc_ref[...]


def _merge(h, mod, row_of_tile, tm, gain, y_f, y_a, y_r, w_bf, w_ba, w_br, w_mg, b_mg, w_out, layer):
    n, d = h.shape
    tn = _tile(d, (256, 128))
    nn = d // tn
    wspec = lambda rows: pl.BlockSpec((None, rows, tn), lambda i, j: (layer, 0, j))
    gspec = lambda k: pl.BlockSpec((None, d, tn), lambda i, j: (layer, 0, k * nn + j))
    bspec = lambda k: pl.BlockSpec((None, 1, tn), lambda i, j: (layer, 0, k * nn + j))
    yspec = lambda w: pl.BlockSpec((tm, w), lambda i, j: (i, 0))
    return pl.pallas_call(
        functools.partial(_merge_kernel, nn=nn),
        grid=(n // tm, nn),
        in_specs=[
            pl.BlockSpec((tm, d), lambda i, j: (i, 0)),
            _mod_spec(d, row_of_tile, 3),
            _mod_spec(d, row_of_tile, 4),
            _mod_spec(d, row_of_tile, 5),
            pl.BlockSpec((None, 1, d), lambda i, j: (layer, 0, 0)),
            yspec(FOURIER_WIDTH), yspec(ATTN_WIDTH), yspec(RET_WIDTH),
            wspec(FOURIER_WIDTH), wspec(ATTN_WIDTH), wspec(RET_WIDTH),
            gspec(0), gspec(1), gspec(2),
            bspec(0), bspec(1), bspec(2),
            pl.BlockSpec((None, tn, d), lambda i, j: (layer, j, 0)),
        ],
        out_specs=pl.BlockSpec((tm, d), lambda i, j: (i, 0)),
        out_shape=jax.ShapeDtypeStruct((n, d), F32),
        scratch_shapes=[pltpu.VMEM((tm, d), BF16), pltpu.VMEM((tm, d), F32)],
        compiler_params=_params("parallel", "arbitrary"),
    )(h, mod, mod, mod, gain, y_f, y_a, y_r, w_bf, w_ba, w_br, w_mg, w_mg, w_mg, b_mg, b_mg, b_mg, w_out)


def kernel(x, c, ctx, c_ctx, w_ada, b_ada, ffn1_norm, ffn1_w_gate, ffn1_w_up, ffn1_w_down, mix_norm, w_in, q_norm, k_norm, ret_decay, w_branch_fourier, w_branch_attn, w_branch_ret, w_merge_gate, b_merge_gate, w_out, ffn2_norm, ffn2_w_gate, ffn2_w_up, ffn2_w_down, final_norm):
    batch, seq, d = x.shape
    ctx_len = ctx.shape[1]
    depth = w_ada.shape[0]
    bf = lambda w: w.astype(BF16)
    vec = lambda g: g.reshape(depth, 1, g.shape[-1])

    rows = -(-(batch + 1) // 16) * 16
    c_all = jnp.concatenate([c, c_ctx[None], jnp.zeros((rows - batch - 1, d), F32)], axis=0)
    mod = _adaln(c_all, w_ada, b_ada).reshape(depth * rows, 1, N_MOD * d)

    tm = _tile(seq, (512, 256, 128))
    tiles_per_sample = seq // tm
    tmc = _tile(batch * ctx_len, (512, 256, 128))

    ffn1 = (vec(ffn1_norm), bf(ffn1_w_gate), bf(ffn1_w_up), bf(ffn1_w_down))
    ffn2 = (vec(ffn2_norm), bf(ffn2_w_gate), bf(ffn2_w_up), bf(ffn2_w_down))
    mix_gain, w_in_b = vec(mix_norm), bf(w_in)
    qn, kn = vec(q_norm), vec(k_norm)
    w_bf, w_ba, w_br = bf(w_branch_fourier), bf(w_branch_attn), bf(w_branch_ret)
    w_mg, b_mg, w_o = bf(w_merge_gate), vec(b_merge_gate), bf(w_out)
    log_gamma = -jnp.exp(ret_decay.astype(F32))
    tables = _rope_tables(seq)
    four_lat = _fourier_consts(seq)
    four_ctx = _fourier_consts(ctx_len)

    h = x.reshape(batch * seq, d)
    hc = ctx.reshape(batch * ctx_len, d)
    for l in range(depth):
        need_ctx = l < depth - 1
        lat_row = lambda i, l=l: l * rows + i // tiles_per_sample
        ctx_row = lambda i, l=l: l * rows + batch

        h = _ffn(h, mod, lat_row, tm, 0, *ffn1, l)
        hc = _ffn(hc, mod, ctx_row, tmc, 0, *ffn1, l)

        proj = _inproj(h, mod, lat_row, tm, mix_gain, w_in_b, l)
        proj_c = _inproj(hc, mod, ctx_row, tmc, mix_gain, w_in_b, l)

        q, k, v = _qkprep(proj, seq, qn, kn, l, tables)
        qc, kc, vc = _qkprep(proj_c, ctx_len, qn, kn, l)
        y_a = _attention(q, kc, vc, batch, seq, ctx_len, k, v)
        y_r = _retention(proj, log_gamma[l], batch, seq, proj_c, ctx_len)
        y_f = _fourier(proj, batch, seq, four_lat)
        h = _merge(h, mod, lat_row, tm, mix_gain, y_f, y_a, y_r, w_bf, w_ba, w_br, w_mg, b_mg, w_o, l)
        last = final_norm.reshape(1, d) if l == depth - 1 else None
        h = _ffn(h, mod, lat_row, tm, 6, *ffn2, l, final_gain=last)

        if need_ctx:
            yc_a = _attention(qc, kc, vc, batch, ctx_len, ctx_len)
            yc_r = _retention(proj_c, log_gamma[l], batch, ctx_len)
            yc_f = _fourier(proj_c, batch, ctx_len, four_ctx)
            hc = _merge(hc, mod, ctx_row, tmc, mix_gain, yc_f, yc_a, yc_r, w_bf, w_ba, w_br, w_mg, b_mg, w_o, l)
            hc = _ffn(hc, mod, ctx_row, tmc, 6, *ffn2, l)
    return h.reshape(batch, seq, d)
```

```python
import functools

import numpy as np
import jax
import jax.numpy as jnp
from jax import lax
from jax.experimental import pallas as pl
from jax.experimental.pallas import tpu as pltpu

F32 = jnp.float32
BF16 = jnp.bfloat16

EPS = 1e-6
N_MOD = 9
GRID_W = 64
HEAD_DIM = 128
HALF_ROT = HEAD_DIM // 2
ATTN_HEADS = 8
ATTN_KV_HEADS = 2
ATTN_GROUP = ATTN_HEADS // ATTN_KV_HEADS
ATTN_WIDTH = ATTN_HEADS * HEAD_DIM
KV_WIDTH = ATTN_KV_HEADS * HEAD_DIM
RET_HEADS = 4
RET_DK = 128
RET_WIDTH = RET_HEADS * RET_DK
FOURIER_GROUPS = 4
FOURIER_GROUP_DIM = 128
FOURIER_WIDTH = FOURIER_GROUPS * FOURIER_GROUP_DIM
IN_WIDTH = ATTN_WIDTH + 2 * KV_WIDTH + 4 * RET_WIDTH + FOURIER_WIDTH
ROPE_THETA = 10000.0

COL_AK = ATTN_WIDTH // 128
COL_AV = COL_AK + KV_WIDTH // 128
COL_RQ = COL_AV + KV_WIDTH // 128
COL_RK = COL_RQ + RET_WIDTH // 128
COL_RV = COL_RK + RET_WIDTH // 128
COL_RG = COL_RV + RET_WIDTH // 128
COL_FZ = COL_RG + RET_WIDTH // 128

V7X_VMEM_LIMIT_BYTES = 56 * 1024 * 1024


def _params(*sem):
    return pltpu.CompilerParams(dimension_semantics=sem, vmem_limit_bytes=V7X_VMEM_LIMIT_BYTES)


def _tile(n, prefs):
    for p in prefs:
        if n % p == 0:
            return p
    return n


def _dot(a, b):
    return jnp.dot(a, b, preferred_element_type=F32)


def _dot_nt(a, b):
    return lax.dot_general(a, b, (((1,), (1,)), ((), ())), preferred_element_type=F32)


def _rms(x):
    return x * lax.rsqrt(jnp.mean(x * x, axis=-1, keepdims=True) + EPS)


def _silu(x):
    return x * jax.nn.sigmoid(x)


def _adaln_kernel(c_ref, w_ref, b_ref, o_ref):
    a = _silu(c_ref[...]).astype(BF16)
    o_ref[...] = _dot(a, w_ref[...].astype(BF16)) + b_ref[...]


def _adaln(c_all, w_ada, b_ada):
    depth, d, nd = w_ada.shape
    r = c_all.shape[0]
    tn = _tile(nd, (1024, 512, 256, 128))
    return pl.pallas_call(
        _adaln_kernel,
        grid=(depth, nd // tn),
        in_specs=[
            pl.BlockSpec((r, d), lambda l, j: (0, 0)),
            pl.BlockSpec((None, d, tn), lambda l, j: (l, 0, j)),
            pl.BlockSpec((None, 1, tn), lambda l, j: (l, 0, j)),
        ],
        out_specs=pl.BlockSpec((None, r, tn), lambda l, j: (l, 0, j)),
        out_shape=jax.ShapeDtypeStruct((depth, r, nd), F32),
        name="adaln",
        compiler_params=_params("parallel", "parallel"),
    )(c_all, w_ada, b_ada.reshape(depth, 1, nd))


def _mod_spec(d, row_of_tile, k):
    return pl.BlockSpec((None, 1, d), lambda i, j: (row_of_tile(i), 0, k))


def _ffn_kernel(h_ref, sh_ref, sc_ref, gt_ref, gain_ref, wg_ref, wu_ref, wd_ref, *rest, nf, final):
    if final:
        fg_ref, o_ref, xn_ref, acc_ref = rest
    else:
        o_ref, xn_ref, acc_ref = rest
    f = pl.program_id(1)

    @pl.when(f == 0)
    def _():
        y = _rms(h_ref[...]) * gain_ref[...]
        xn_ref[...] = (y * (1.0 + sc_ref[...]) + sh_ref[...]).astype(BF16)
        acc_ref[...] = jnp.zeros_like(acc_ref)

    xn = xn_ref[...]
    g = _dot(xn, wg_ref[...])
    u = _dot(xn, wu_ref[...])
    acc_ref[...] += _dot((_silu(g) * u).astype(BF16), wd_ref[...])

    @pl.when(f == nf - 1)
    def _():
        y = h_ref[...] + (0.5 * gt_ref[...]) * acc_ref[...]
        if final:
            y = _rms(y) * fg_ref[...]
        o_ref[...] = y


def _ffn(h, mod, row_of_tile, tm, k0, gain, wg, wu, wd, layer, final_gain=None):
    n, d = h.shape
    ff = wg.shape[-1]
    tf = _tile(ff, (512, 256, 128))
    nf = ff // tf
    final = final_gain is not None
    in_specs = [
        pl.BlockSpec((tm, d), lambda i, f: (i, 0)),
        _mod_spec(d, row_of_tile, k0),
        _mod_spec(d, row_of_tile, k0 + 1),
        _mod_spec(d, row_of_tile, k0 + 2),
        pl.BlockSpec((None, 1, d), lambda i, f: (layer, 0, 0)),
        pl.BlockSpec((None, d, tf), lambda i, f: (layer, 0, f)),
        pl.BlockSpec((None, d, tf), lambda i, f: (layer, 0, f)),
        pl.BlockSpec((None, tf, d), lambda i, f: (layer, f, 0)),
    ]
    args = [h, mod, mod, mod, gain, wg, wu, wd]
    if final:
        in_specs.append(pl.BlockSpec((1, d), lambda i, f: (0, 0)))
        args.append(final_gain)
    return pl.pallas_call(
        functools.partial(_ffn_kernel, nf=nf, final=final),
        grid=(n // tm, nf),
        in_specs=in_specs,
        out_specs=pl.BlockSpec((tm, d), lambda i, f: (i, 0)),
        out_shape=jax.ShapeDtypeStruct((n, d), F32),
        scratch_shapes=[pltpu.VMEM((tm, d), BF16), pltpu.VMEM((tm, d), F32)],
        name="ffn",
        compiler_params=_params("parallel", "arbitrary"),
    )(*args)


def _inproj_kernel(h_ref, sh_ref, sc_ref, gain_ref, w_ref, o_ref, xn_ref):
    @pl.when(pl.program_id(1) == 0)
    def _():
        y = _rms(h_ref[...]) * gain_ref[...]
        xn_ref[...] = (y * (1.0 + sc_ref[...]) + sh_ref[...]).astype(BF16)

    o_ref[...] = _dot(xn_ref[...], w_ref[...])


def _inproj(h, mod, row_of_tile, tm, gain, w_in, layer):
    n, d = h.shape
    nw = w_in.shape[-1]
    tn = _tile(nw, (1024, 512))
    return pl.pallas_call(
        _inproj_kernel,
        grid=(n // tm, nw // tn),
        in_specs=[
            pl.BlockSpec((tm, d), lambda i, j: (i, 0)),
            _mod_spec(d, row_of_tile, 3),
            _mod_spec(d, row_of_tile, 4),
            pl.BlockSpec((None, 1, d), lambda i, j: (layer, 0, 0)),
            pl.BlockSpec((None, d, tn), lambda i, j: (layer, 0, j)),
        ],
        out_specs=pl.BlockSpec((tm, tn), lambda i, j: (i, j)),
        out_shape=jax.ShapeDtypeStruct((n, nw), F32),
        scratch_shapes=[pltpu.VMEM((tm, d), BF16)],
        name="inproj",
        compiler_params=_params("parallel", "arbitrary"),
    )(h, mod, mod, gain, w_in)


def _rope(x, cos, sin_lo, sin_hi):
    return x * cos + pltpu.roll(x, HEAD_DIM - HALF_ROT // 2, 1) * sin_lo + pltpu.roll(x, HALF_ROT // 2, 1) * sin_hi


def _qkprep_kernel(q_ref, kv_ref, qn_ref, kn_ref, *rest, rope):
    if rope:
        cos_ref, slo_ref, shi_ref, qo_ref, ko_ref, vo_ref = rest
        cos, slo, shi = cos_ref[...], slo_ref[...], shi_ref[...]
    else:
        qo_ref, ko_ref, vo_ref = rest
    qn, kn = qn_ref[...], kn_ref[...]
    scale = HEAD_DIM ** -0.5
    for hd in range(ATTN_HEADS):
        sl = slice(hd * HEAD_DIM, (hd + 1) * HEAD_DIM)
        y = _rms(q_ref[:, sl]) * qn
        if rope:
            y = _rope(y, cos, slo, shi)
        qo_ref[:, sl] = (y * scale).astype(BF16)
    for hd in range(ATTN_KV_HEADS):
        sl = slice(hd * HEAD_DIM, (hd + 1) * HEAD_DIM)
        y = _rms(kv_ref[:, sl]) * kn
        if rope:
            y = _rope(y, cos, slo, shi)
        ko_ref[:, sl] = y.astype(BF16)
    vo_ref[...] = kv_ref[:, KV_WIDTH:].astype(BF16)


def _qkprep(proj, seq, q_norm, k_norm, layer, tables=None):
    n = proj.shape[0]
    tr = _tile(seq, (512, 256, 128))
    rope = tables is not None
    in_specs = [
        pl.BlockSpec((tr, ATTN_WIDTH), lambda i: (i, 0)),
        pl.BlockSpec((tr, 2 * KV_WIDTH), lambda i: (i, ATTN_WIDTH // (2 * KV_WIDTH))),
        pl.BlockSpec((None, 1, HEAD_DIM), lambda i: (layer, 0, 0)),
        pl.BlockSpec((None, 1, HEAD_DIM), lambda i: (layer, 0, 0)),
    ]
    args = [proj, proj, q_norm, k_norm]
    if rope:
        nt = seq // tr
        in_specs += [pl.BlockSpec((tr, HEAD_DIM), lambda i: (i % nt, 0))] * 3
        args += list(tables)
    return pl.pallas_call(
        functools.partial(_qkprep_kernel, rope=rope),
        grid=(n // tr,),
        in_specs=in_specs,
        out_specs=[
            pl.BlockSpec((tr, ATTN_WIDTH), lambda i: (i, 0)),
            pl.BlockSpec((tr, KV_WIDTH), lambda i: (i, 0)),
            pl.BlockSpec((tr, KV_WIDTH), lambda i: (i, 0)),
        ],
        out_shape=[
            jax.ShapeDtypeStruct((n, ATTN_WIDTH), BF16),
            jax.ShapeDtypeStruct((n, KV_WIDTH), BF16),
            jax.ShapeDtypeStruct((n, KV_WIDTH), BF16),
        ],
        name="qkprep",
        compiler_params=_params("parallel"),
    )(*args)


def _rope_tables(seq):
    t = jnp.arange(seq)
    row = (t // GRID_W).astype(F32)
    col = (t % GRID_W).astype(F32)
    inv = ROPE_THETA ** (-jnp.arange(0, HALF_ROT, 2, dtype=F32) / HALF_ROT)
    ar, ac = row[:, None] * inv[None], col[:, None] * inv[None]
    zero = jnp.zeros_like(ar)
    cos = jnp.concatenate([jnp.cos(ar), jnp.cos(ar), jnp.cos(ac), jnp.cos(ac)], axis=-1)
    sin_lo = jnp.concatenate([-jnp.sin(ar), zero, -jnp.sin(ac), zero], axis=-1)
    sin_hi = jnp.concatenate([zero, jnp.sin(ar), zero, jnp.sin(ac)], axis=-1)
    return cos, sin_lo, sin_hi


def _attn_kernel(q_ref, kc_ref, vc_ref, *rest, has_lat):
    if has_lat:
        kl_ref, vl_ref, o_ref = rest
        kl, vl = kl_ref[...], vl_ref[...]
    else:
        (o_ref,) = rest
    kc, vc = kc_ref[...], vc_ref[...]
    for g in range(ATTN_GROUP):
        sl = slice(g * HEAD_DIM, (g + 1) * HEAD_DIM)
        q = q_ref[:, sl]
        sc = _dot_nt(q, kc)
        m = jnp.max(sc, axis=-1, keepdims=True)
        if has_lat:
            s_lat = _dot_nt(q, kl)
            m = jnp.maximum(m, jnp.max(s_lat, axis=-1, keepdims=True))
        pc = jnp.exp(sc - m)
        den = jnp.sum(pc, axis=-1, keepdims=True)
        o = _dot(pc.astype(BF16), vc)
        if has_lat:
            p_lat = jnp.exp(s_lat - m)
            den = den + jnp.sum(p_lat, axis=-1, keepdims=True)
            o = o + _dot(p_lat.astype(BF16), vl)
        o_ref[:, sl] = (o / den).astype(BF16)


def _attention(q, kc, vc, batch, seq, ctx_len, k_lat=None, v_lat=None):
    n = q.shape[0]
    tq = _tile(seq, (256, 128))
    nq = seq // tq
    gw = ATTN_GROUP * HEAD_DIM
    has_lat = k_lat is not None
    in_specs = [
        pl.BlockSpec((tq, gw), lambda b, kv, i: (b * nq + i, kv)),
        pl.BlockSpec((ctx_len, HEAD_DIM), lambda b, kv, i: (b, kv)),
        pl.BlockSpec((ctx_len, HEAD_DIM), lambda b, kv, i: (b, kv)),
    ]
    args = [q, kc, vc]
    if has_lat:
        in_specs += [pl.BlockSpec((seq, HEAD_DIM), lambda b, kv, i: (b, kv))] * 2
        args += [k_lat, v_lat]
    return pl.pallas_call(
        functools.partial(_attn_kernel, has_lat=has_lat),
        grid=(batch, ATTN_KV_HEADS, nq),
        in_specs=in_specs,
        out_specs=pl.BlockSpec((tq, gw), lambda b, kv, i: (b * nq + i, kv)),
        out_shape=jax.ShapeDtypeStruct((n, ATTN_WIDTH), BF16),
        name="attention",
        compiler_params=_params("parallel", "parallel", "parallel"),
    )(*args)


def _ret_kernel(lg_ref, q_ref, k_ref, v_ref, g_ref, *rest, seq, ctx_len, chunk, has_ctx):
    if has_ctx:
        kc_ref, vc_ref, o_ref, a_ref, s_ref = rest
    else:
        o_ref, a_ref, s_ref = rest
    hd = pl.program_id(1)
    lgf, lgb = lg_ref[0, hd], lg_ref[1, hd]
    c, dk = chunk, RET_DK
    n = seq // c
    scale = dk ** -0.5
    a = lax.broadcasted_iota(jnp.int32, (c, 1), 0).astype(F32)
    wq_f, wq_b = jnp.exp(lgf * (a + 1.0)), jnp.exp(lgb * (c - a))
    wk_f, wk_b = jnp.exp(lgf * (c - 1.0 - a)) * scale, jnp.exp(lgb * a) * scale
    d = (lax.broadcasted_iota(jnp.int32, (c, c), 0) - lax.broadcasted_iota(jnp.int32, (c, c), 1)).astype(F32)
    dmask = jnp.exp(jnp.where(d >= 0.0, lgf * d, -lgb * d)) * jnp.where(d == 0.0, 2.0 * scale, scale)
    zero = jnp.zeros((1, dk), F32)
    gf, gb = jnp.exp(zero + lgf * c), jnp.exp(zero + lgb * c)

    def kv_outer(kr, vr, j):
        k = kr[j * c:(j + 1) * c, :]
        kk = jnp.concatenate([k * wk_f, k * wk_b], axis=1)
        return _dot(kk.T.astype(BF16), vr[j * c:(j + 1) * c, :].astype(BF16))

    sf = jnp.zeros((dk, dk), F32)
    sb = jnp.zeros((dk, dk), F32)
    if has_ctx:
        outer = [kv_outer(kc_ref, vc_ref, j) for j in range(ctx_len // c)]
        for o in outer:
            sf = gf * sf + o[:dk]
        for o in reversed(outer):
            sb = gb * sb + o[dk:]
    for i in range(n):
        a_ref[i] = kv_outer(k_ref, v_ref, i)
    for i in range(n):
        s_ref[i, :dk, :] = sf.astype(BF16)
        sf = gf * sf + a_ref[i, :dk, :]
    for i in reversed(range(n)):
        s_ref[i, dk:, :] = sb.astype(BF16)
        sb = gb * sb + a_ref[i, dk:, :]
    for i in range(n):
        rows = slice(i * c, (i + 1) * c)
        q = q_ref[rows, :]
        s = _dot_nt(q.astype(BF16), k_ref[rows, :].astype(BF16))
        o = _dot((s * dmask).astype(BF16), v_ref[rows, :].astype(BF16))
        qq = jnp.concatenate([q * wq_f, q * wq_b], axis=1).astype(BF16)
        o = o + _dot(qq, s_ref[i])
        o_ref[rows, :] = (_silu(g_ref[rows, :]) * _rms(o)).astype(BF16)


def _retention(proj, log_gamma, batch, seq, proj_ctx=None, ctx_len=0):
    n = proj.shape[0]
    has_ctx = proj_ctx is not None
    chunk = 256 if seq % 256 == 0 and ctx_len % 256 == 0 else 128
    in_specs = [
        pl.BlockSpec(memory_space=pltpu.SMEM),
        pl.BlockSpec((seq, RET_DK), lambda b, h: (b, COL_RQ + h)),
        pl.BlockSpec((seq, RET_DK), lambda b, h: (b, COL_RK + h)),
        pl.BlockSpec((seq, RET_DK), lambda b, h: (b, COL_RV + h)),
        pl.BlockSpec((seq, RET_DK), lambda b, h: (b, COL_RG + h)),
    ]
    args = [log_gamma, proj, proj, proj, proj]
    if has_ctx:
        in_specs += [
            pl.BlockSpec((ctx_len, RET_DK), lambda b, h: (b, COL_RK + h)),
            pl.BlockSpec((ctx_len, RET_DK), lambda b, h: (b, COL_RV + h)),
        ]
        args += [proj_ctx, proj_ctx]
    return pl.pallas_call(
        functools.partial(_ret_kernel, seq=seq, ctx_len=ctx_len, chunk=chunk, has_ctx=has_ctx),
        grid=(batch, RET_HEADS),
        in_specs=in_specs,
        out_specs=pl.BlockSpec((seq, RET_DK), lambda b, h: (b, h)),
        out_shape=jax.ShapeDtypeStruct((n, RET_WIDTH), BF16),
        scratch_shapes=[
            pltpu.VMEM((seq // chunk, 2 * RET_DK, RET_DK), F32),
            pltpu.VMEM((seq // chunk, 2 * RET_DK, RET_DK), BF16),
        ],
        name="retention",
        compiler_params=_params("parallel", "parallel"),
    )(*args)


def _dft_cos_sin(n):
    k = np.arange(n, dtype=np.int64)
    ang = 2.0 * np.pi * ((k[:, None] * k[None, :]) % n).astype(np.float64) / n
    return np.cos(ang), np.sin(ang)


def _fourier_consts(seq):
    cg, sg = _dft_cos_sin(FOURIER_GROUP_DIM)
    norm = 1.0 / np.sqrt(float(seq) * FOURIER_GROUP_DIM)
    eye = np.eye(FOURIER_GROUPS)
    chan = np.concatenate([np.kron(eye, cg), np.kron(eye, -sg)], axis=1) * norm
    cs, ss = _dft_cos_sin(seq)
    as_bf16 = lambda a: jnp.asarray(a.astype(np.float32)).astype(BF16)
    return as_bf16(chan), as_bf16(cs), as_bf16(ss)


def _fchan_kernel(z_ref, m_ref, o_ref):
    o_ref[...] = _dot(z_ref[...].astype(BF16), m_ref[...]).astype(BF16)


def _fseq_kernel(c_ref, s_ref, zc_ref, zs_ref, o_ref):
    o_ref[...] = (_dot(c_ref[...], zc_ref[...]) + _dot(s_ref[...], zs_ref[...])).astype(BF16)


def _fourier(proj, batch, seq, consts):
    n = proj.shape[0]
    chan, cs, ss = consts
    fw = FOURIER_WIDTH
    tm = _tile(n, (1024, 512, 256, 128))
    zcs = pl.pallas_call(
        _fchan_kernel,
        grid=(n // tm,),
        in_specs=[
            pl.BlockSpec((tm, fw), lambda i: (i, COL_FZ * 128 // fw)),
            pl.BlockSpec((fw, 2 * fw), lambda i: (0, 0)),
        ],
        out_specs=pl.BlockSpec((tm, 2 * fw), lambda i: (i, 0)),
        out_shape=jax.ShapeDtypeStruct((n, 2 * fw), BF16),
        name="fourier_chan",
        compiler_params=_params("parallel"),
    )(proj, chan)
    tk = _tile(seq, (512, 256, 128))
    nk = seq // tk
    return pl.pallas_call(
        _fseq_kernel,
        grid=(nk, batch),
        in_specs=[
            pl.BlockSpec((tk, seq), lambda k, b: (k, 0)),
            pl.BlockSpec((tk, seq), lambda k, b: (k, 0)),
            pl.BlockSpec((seq, fw), lambda k, b: (b, 0)),
            pl.BlockSpec((seq, fw), lambda k, b: (b, 1)),
        ],
        out_specs=pl.BlockSpec((tk, fw), lambda k, b: (b * nk + k, 0)),
        out_shape=jax.ShapeDtypeStruct((n, fw), BF16),
        name="fourier_seq",
        compiler_params=_params("parallel", "parallel"),
    )(cs, ss, zcs, zcs)


def _merge_kernel(h_ref, sh_ref, sc_ref, gt_ref, gain_ref, yf_ref, ya_ref, yr_ref, wf_ref, wa_ref, wr_ref,
                  gf_ref, ga_ref, gr_ref, bf_ref, ba_ref, br_ref, wo_ref, o_ref, u_ref, acc_ref, *, nn):
    j = pl.program_id(1)

    @pl.when(j == 0)
    def _():
        y = _rms(h_ref[...]) * gain_ref[...]
        u_ref[...] = (y * (1.0 + sc_ref[...]) + sh_ref[...]).astype(BF16)
        acc_ref[...] = jnp.zeros_like(acc_ref)

    u = u_ref[...]
    m = jax.nn.sigmoid(_dot(u, gf_ref[...]) + bf_ref[...]) * _dot(yf_ref[...], wf_ref[...])
    m = m + jax.nn.sigmoid(_dot(u, ga_ref[...]) + ba_ref[...]) * _dot(ya_ref[...], wa_ref[...])
    m = m + jax.nn.sigmoid(_dot(u, gr_ref[...]) + br_ref[...]) * _dot(yr_ref[...], wr_ref[...])
    acc_ref[...] += _dot(m.astype(BF16), wo_ref[...])

    @pl.when(j == nn - 1)
    def _():
        o_ref[...] = h_ref[...] + gt_ref[...] * acc_ref[...]


def _merge(h, mod, row_of_tile, tm, gain, y_f, y_a, y_r, w_bf, w_ba, w_br, w_mg, b_mg, w_out, layer):
    n, d = h.shape
    tn = _tile(d, (256, 128))
    nn = d // tn
    wspec = lambda rows: pl.BlockSpec((None, rows, tn), lambda i, j: (layer, 0, j))
    gspec = lambda k: pl.BlockSpec((None, d, tn), lambda i, j: (layer, 0, k * nn + j))
    bspec = lambda k: pl.BlockSpec((None, 1, tn), lambda i, j: (layer, 0, k * nn + j))
    yspec = lambda w: pl.BlockSpec((tm, w), lambda i, j: (i, 0))
    return pl.pallas_call(
        functools.partial(_merge_kernel, nn=nn),
        grid=(n // tm, nn),
        in_specs=[
            pl.BlockSpec((tm, d), lambda i, j: (i, 0)),
            _mod_spec(d, row_of_tile, 3),
            _mod_spec(d, row_of_tile, 4),
            _mod_spec(d, row_of_tile, 5),
            pl.BlockSpec((None, 1, d), lambda i, j: (layer, 0, 0)),
            yspec(FOURIER_WIDTH), yspec(ATTN_WIDTH), yspec(RET_WIDTH),
            wspec(FOURIER_WIDTH), wspec(ATTN_WIDTH), wspec(RET_WIDTH),
            gspec(0), gspec(1), gspec(2),
            bspec(0), bspec(1), bspec(2),
            pl.BlockSpec((None, tn, d), lambda i, j: (layer, j, 0)),
        ],
        out_specs=pl.BlockSpec((tm, d), lambda i, j: (i, 0)),
        out_shape=jax.ShapeDtypeStruct((n, d), F32),
        scratch_shapes=[pltpu.VMEM((tm, d), BF16), pltpu.VMEM((tm, d), F32)],
        name="merge",
        compiler_params=_params("parallel", "arbitrary"),
    )(h, mod, mod, mod, gain, y_f, y_a, y_r, w_bf, w_ba, w_br, w_mg, w_mg, w_mg, b_mg, b_mg, b_mg, w_out)


def kernel(x, c, ctx, c_ctx, w_ada, b_ada, ffn1_norm, ffn1_w_gate, ffn1_w_up, ffn1_w_down, mix_norm, w_in, q_norm, k_norm, ret_decay, w_branch_fourier, w_branch_attn, w_branch_ret, w_merge_gate, b_merge_gate, w_out, ffn2_norm, ffn2_w_gate, ffn2_w_up, ffn2_w_down, final_norm):
    batch, seq, d = x.shape
    ctx_len = ctx.shape[1]
    depth = w_ada.shape[0]
    bf = lambda w: w.astype(BF16)
    vec = lambda g: g.reshape(depth, 1, g.shape[-1])

    rows = -(-(batch + 1) // 16) * 16
    c_all = jnp.concatenate([c, c_ctx[None], jnp.zeros((rows - batch - 1, d), F32)], axis=0)
    mod = _adaln(c_all, w_ada, b_ada).reshape(depth * rows, 1, N_MOD * d)

    tm = _tile(seq, (512, 256, 128))
    tiles_per_sample = seq // tm
    tmc = _tile(batch * ctx_len, (512, 256, 128))

    ffn1 = (vec(ffn1_norm), bf(ffn1_w_gate), bf(ffn1_w_up), bf(ffn1_w_down))
    ffn2 = (vec(ffn2_norm), bf(ffn2_w_gate), bf(ffn2_w_up), bf(ffn2_w_down))
    mix_gain, w_in_b = vec(mix_norm), bf(w_in)
    qn, kn = vec(q_norm), vec(k_norm)
    w_bf, w_ba, w_br = bf(w_branch_fourier), bf(w_branch_attn), bf(w_branch_ret)
    w_mg, b_mg, w_o = bf(w_merge_gate), vec(b_merge_gate), bf(w_out)
    log_gamma = -jnp.exp(ret_decay.astype(F32))
    tables = _rope_tables(seq)
    four_lat = _fourier_consts(seq)
    four_ctx = _fourier_consts(ctx_len)

    h = x.reshape(batch * seq, d)
    hc = ctx.reshape(batch * ctx_len, d)
    for l in range(depth):
        need_ctx = l < depth - 1
        lat_row = lambda i, l=l: l * rows + i // tiles_per_sample
        ctx_row = lambda i, l=l: l * rows + batch

        h = _ffn(h, mod, lat_row, tm, 0, *ffn1, l)
        hc = _ffn(hc, mod, ctx_row, tmc, 0, *ffn1, l)

        proj = _inproj(h, mod, lat_row, tm, mix_gain, w_in_b, l)
        proj_c = _inproj(hc, mod, ctx_row, tmc, mix_gain, w_in_b, l)

        q, k, v = _qkprep(proj, seq, qn, kn, l, tables)
        qc, kc, vc = _qkprep(proj_c, ctx_len, qn, kn, l)
        y_a = _attention(q, kc, vc, batch, seq, ctx_len, k, v)
        y_r = _retention(proj, log_gamma[l], batch, seq, proj_c, ctx_len)
        y_f = _fourier(proj, batch, seq, four_lat)
        h = _merge(h, mod, lat_row, tm, mix_gain, y_f, y_a, y_r, w_bf, w_ba, w_br, w_mg, b_mg, w_o, l)
        last = final_norm.reshape(1, d) if l == depth - 1 else None
        h = _ffn(h, mod, lat_row, tm, 6, *ffn2, l, final_gain=last)

        if need_ctx:
            yc_a = _attention(qc, kc, vc, batch, ctx_len, ctx_len)
            yc_r = _retention(proj_c, log_gamma[l], batch, ctx_len)
            yc_f = _fourier(proj_c, batch, ctx_len, four_ctx)
            hc = _merge(hc, mod, ctx_row, tmc, mix_gain, yc_f, yc_a, yc_r, w_bf, w_ba, w_br, w_mg, b_mg, w_o, l)
            hc = _ffn(hc, mod, ctx_row, tmc, 6, *ffn2, l)
    return h.reshape(batch, seq, d)
```

```python
import functools

import numpy as np
import jax
import jax.numpy as jnp
from jax import lax
from jax.experimental import pallas as pl
from jax.experimental.pallas import tpu as pltpu

F32 = jnp.float32
BF16 = jnp.bfloat16

EPS = 1e-6
N_MOD = 9
GRID_W = 64
HEAD_DIM = 128
HALF_ROT = HEAD_DIM // 2
ATTN_HEADS = 8
ATTN_KV_HEADS = 2
ATTN_GROUP = ATTN_HEADS // ATTN_KV_HEADS
ATTN_WIDTH = ATTN_HEADS * HEAD_DIM
KV_WIDTH = ATTN_KV_HEADS * HEAD_DIM
RET_HEADS = 4
RET_DK = 128
RET_WIDTH = RET_HEADS * RET_DK
FOURIER_GROUPS = 4
FOURIER_GROUP_DIM = 128
FOURIER_WIDTH = FOURIER_GROUPS * FOURIER_GROUP_DIM
IN_WIDTH = ATTN_WIDTH + 2 * KV_WIDTH + 4 * RET_WIDTH + FOURIER_WIDTH
ROPE_THETA = 10000.0

COL_AK = ATTN_WIDTH // 128
COL_AV = COL_AK + KV_WIDTH // 128
COL_RQ = COL_AV + KV_WIDTH // 128
COL_RK = COL_RQ + RET_WIDTH // 128
COL_RV = COL_RK + RET_WIDTH // 128
COL_RG = COL_RV + RET_WIDTH // 128
COL_FZ = COL_RG + RET_WIDTH // 128

IN_TILE = ATTN_WIDTH

V7X_VMEM_LIMIT_BYTES = 56 * 1024 * 1024


def _params(*sem):
    return pltpu.CompilerParams(dimension_semantics=sem, vmem_limit_bytes=V7X_VMEM_LIMIT_BYTES)


def _tile(n, prefs):
    for p in prefs:
        if n % p == 0:
            return p
    return n


def _dot(a, b):
    return jnp.dot(a, b, preferred_element_type=F32)


def _dot_nt(a, b):
    return lax.dot_general(a, b, (((1,), (1,)), ((), ())), preferred_element_type=F32)


def _rms(x):
    return x * lax.rsqrt(jnp.mean(x * x, axis=-1, keepdims=True) + EPS)


def _silu(x):
    return x * jax.nn.sigmoid(x)


def _adaln_kernel(c_ref, w_ref, b_ref, o_ref):
    a = _silu(c_ref[...]).astype(BF16)
    o_ref[...] = _dot(a, w_ref[...].astype(BF16)) + b_ref[...]


def _adaln(c_all, w_ada, b_ada):
    depth, d, nd = w_ada.shape
    r = c_all.shape[0]
    tn = _tile(nd, (1024, 512, 256, 128))
    return pl.pallas_call(
        _adaln_kernel,
        grid=(depth, nd // tn),
        in_specs=[
            pl.BlockSpec((r, d), lambda l, j: (0, 0)),
            pl.BlockSpec((None, d, tn), lambda l, j: (l, 0, j)),
            pl.BlockSpec((None, 1, tn), lambda l, j: (l, 0, j)),
        ],
        out_specs=pl.BlockSpec((None, r, tn), lambda l, j: (l, 0, j)),
        out_shape=jax.ShapeDtypeStruct((depth, r, nd), F32),
        name="adaln",
        compiler_params=_params("parallel", "parallel"),
    )(c_all, w_ada, b_ada.reshape(depth, 1, nd))


def _mod_spec(d, row_of_tile, k):
    return pl.BlockSpec((None, 1, d), lambda i, *_: (row_of_tile(i), 0, k))


FFN_TILE = 512


def _ffn_kernel(h_ref, sh_ref, sc_ref, gt_ref, gain_ref, wg_ref, wu_ref, wd_ref, *rest, nf, final, emit):
    if final:
        fg_ref, o_ref, xn_ref, acc_ref = rest
    elif emit:
        sh2_ref, sc2_ref, gain2_ref, o_ref, u_ref, xn_ref, acc_ref = rest
    else:
        o_ref, xn_ref, acc_ref = rest
    f = pl.program_id(1)

    @pl.when(f == 0)
    def _():
        y = _rms(h_ref[...]) * gain_ref[...]
        xn_ref[...] = (y * (1.0 + sc_ref[...]) + sh_ref[...]).astype(BF16)
        acc_ref[...] = jnp.zeros_like(acc_ref)

    xn = xn_ref[...]
    g = _dot(xn, wg_ref[...])
    u = _dot(xn, wu_ref[...])
    acc_ref[...] += _dot((_silu(g) * u).astype(BF16), wd_ref[...])

    @pl.when(f == nf - 1)
    def _():
        y = h_ref[...] + (0.5 * gt_ref[...]) * acc_ref[...]
        if final:
            y = _rms(y) * fg_ref[...]
        o_ref[...] = y
        if emit:
            z = _rms(y) * gain2_ref[...]
            u_ref[...] = (z * (1.0 + sc2_ref[...]) + sh2_ref[...]).astype(BF16)


def _ffn(h, mod, row_of_tile, tm, k0, gain, wg, wu, wd, layer, final_gain=None, emit_gain=None):
    n, d = h.shape
    tf = _tile(wg.shape[-1], (FFN_TILE, 256, 128))
    nf = wg.shape[-1] // tf
    final, emit = final_gain is not None, emit_gain is not None
    vec_spec = pl.BlockSpec((None, 1, d), lambda i, f: (layer, 0, 0))
    in_specs = [
        pl.BlockSpec((tm, d), lambda i, f: (i, 0)),
        _mod_spec(d, row_of_tile, k0),
        _mod_spec(d, row_of_tile, k0 + 1),
        _mod_spec(d, row_of_tile, k0 + 2),
        vec_spec,
        pl.BlockSpec((None, d, tf), lambda i, f: (layer, 0, f)),
        pl.BlockSpec((None, d, tf), lambda i, f: (layer, 0, f)),
        pl.BlockSpec((None, tf, d), lambda i, f: (layer, f, 0)),
    ]
    args = [h, mod, mod, mod, gain, wg, wu, wd]
    row_spec = pl.BlockSpec((tm, d), lambda i, f: (i, 0))
    out_specs, out_shape = row_spec, jax.ShapeDtypeStruct((n, d), F32)
    if final:
        in_specs.append(pl.BlockSpec((1, d), lambda i, f: (0, 0)))
        args.append(final_gain)
    elif emit:
        in_specs += [_mod_spec(d, row_of_tile, 3), _mod_spec(d, row_of_tile, 4), vec_spec]
        args += [mod, mod, emit_gain]
        out_specs = [row_spec, row_spec]
        out_shape = [out_shape, jax.ShapeDtypeStruct((n, d), BF16)]
    return pl.pallas_call(
        functools.partial(_ffn_kernel, nf=nf, final=final, emit=emit),
        grid=(n // tm, nf),
        in_specs=in_specs,
        out_specs=out_specs,
        out_shape=out_shape,
        scratch_shapes=[pltpu.VMEM((tm, d), BF16), pltpu.VMEM((tm, d), F32)],
        name="ffn",
        compiler_params=_params("parallel", "arbitrary"),
    )(*args)


def _rope(x, cos, sin_lo, sin_hi):
    return x * cos + pltpu.roll(x, HEAD_DIM - HALF_ROT // 2, 1) * sin_lo + pltpu.roll(x, HALF_ROT // 2, 1) * sin_hi


def _inproj_kernel(u_ref, w_ref, qn_ref, kn_ref, *rest, rope):
    if rope:
        cos_ref, slo_ref, shi_ref, o_ref = rest
    else:
        (o_ref,) = rest
    j = pl.program_id(1)
    acc = _dot(u_ref[...], w_ref[...])

    def head(hd, gain, scale):
        sl = slice(hd * HEAD_DIM, (hd + 1) * HEAD_DIM)
        y = _rms(acc[:, sl]) * gain
        if rope:
            y = _rope(y, cos_ref[...], slo_ref[...], shi_ref[...])
        o_ref[:, sl] = (y * scale).astype(BF16)

    @pl.when(j == 0)
    def _():
        for hd in range(ATTN_HEADS):
            head(hd, qn_ref[...], HEAD_DIM ** -0.5)

    @pl.when(j == 1)
    def _():
        for hd in range(ATTN_KV_HEADS):
            head(hd, kn_ref[...], 1.0)
        o_ref[:, KV_WIDTH:] = acc[:, KV_WIDTH:].astype(BF16)

    @pl.when(j >= 2)
    def _():
        o_ref[...] = acc.astype(BF16)


def _inproj(u, seq, w_in, q_norm, k_norm, layer, tables=None):
    n, d = u.shape
    tn = IN_TILE
    nj = IN_WIDTH // tn
    assert w_in.shape[-1] == IN_WIDTH and nj * tn == IN_WIDTH
    rope = tables is not None
    tm = _tile(seq, (1024, 512, 256, 128)) if rope else _tile(n, (1024, 512, 256, 128))
    norm_spec = pl.BlockSpec((None, 1, HEAD_DIM), lambda i, j: (layer, 0, 0))
    in_specs = [
        pl.BlockSpec((tm, d), lambda i, j: (i, 0)),
        pl.BlockSpec((None, d, tn), lambda i, j: (layer, 0, j)),
        norm_spec,
        norm_spec,
    ]
    args = [u, w_in, q_norm, k_norm]
    if rope:
        nt = seq // tm
        in_specs += [pl.BlockSpec((tm, HEAD_DIM), lambda i, j: (i % nt, 0))] * 3
        args += list(tables)
    return pl.pallas_call(
        functools.partial(_inproj_kernel, rope=rope),
        grid=(n // tm, nj),
        in_specs=in_specs,
        out_specs=pl.BlockSpec((tm, tn), lambda i, j: (i, j)),
        out_shape=jax.ShapeDtypeStruct((n, IN_WIDTH), BF16),
        name="inproj",
        compiler_params=_params("parallel", "arbitrary"),
    )(*args)


def _rope_tables(seq):
    t = jnp.arange(seq)
    row = (t // GRID_W).astype(F32)
    col = (t % GRID_W).astype(F32)
    inv = ROPE_THETA ** (-jnp.arange(0, HALF_ROT, 2, dtype=F32) / HALF_ROT)
    ar, ac = row[:, None] * inv[None], col[:, None] * inv[None]
    zero = jnp.zeros_like(ar)
    cos = jnp.concatenate([jnp.cos(ar), jnp.cos(ar), jnp.cos(ac), jnp.cos(ac)], axis=-1)
    sin_lo = jnp.concatenate([-jnp.sin(ar), zero, -jnp.sin(ac), zero], axis=-1)
    sin_hi = jnp.concatenate([zero, jnp.sin(ar), zero, jnp.sin(ac)], axis=-1)
    return cos, sin_lo, sin_hi


def _attn_kernel(q_ref, kc_ref, vc_ref, *rest, has_lat):
    if has_lat:
        kl_ref, vl_ref, o_ref = rest
        kl, vl = kl_ref[...], vl_ref[...]
    else:
        (o_ref,) = rest
    kc, vc = kc_ref[...], vc_ref[...]
    for g in range(ATTN_GROUP):
        sl = slice(g * HEAD_DIM, (g + 1) * HEAD_DIM)
        q = q_ref[:, sl]
        sc = _dot_nt(q, kc)
        m = jnp.max(sc, axis=-1, keepdims=True)
        if has_lat:
            s_lat = _dot_nt(q, kl)
            m = jnp.maximum(m, jnp.max(s_lat, axis=-1, keepdims=True))
        pc = jnp.exp(sc - m)
        den = jnp.sum(pc, axis=-1, keepdims=True)
        o = _dot(pc.astype(BF16), vc)
        if has_lat:
            p_lat = jnp.exp(s_lat - m)
            den = den + jnp.sum(p_lat, axis=-1, keepdims=True)
            o = o + _dot(p_lat.astype(BF16), vl)
        o_ref[:, sl] = (o / den).astype(BF16)


def _attention(proj_q, proj_c, batch, seq, ctx_len, proj_kv=None):
    n = proj_q.shape[0]
    tq = _tile(seq, (256, 128))
    nq = seq // tq
    gw = ATTN_GROUP * HEAD_DIM
    has_lat = proj_kv is not None
    in_specs = [
        pl.BlockSpec((tq, gw), lambda b, kv, i: (b * nq + i, kv)),
        pl.BlockSpec((ctx_len, HEAD_DIM), lambda b, kv, i: (b, COL_AK + kv)),
        pl.BlockSpec((ctx_len, HEAD_DIM), lambda b, kv, i: (b, COL_AV + kv)),
    ]
    args = [proj_q, proj_c, proj_c]
    if has_lat:
        in_specs += [
            pl.BlockSpec((seq, HEAD_DIM), lambda b, kv, i: (b, COL_AK + kv)),
            pl.BlockSpec((seq, HEAD_DIM), lambda b, kv, i: (b, COL_AV + kv)),
        ]
        args += [proj_kv, proj_kv]
    return pl.pallas_call(
        functools.partial(_attn_kernel, has_lat=has_lat),
        grid=(batch, ATTN_KV_HEADS, nq),
        in_specs=in_specs,
        out_specs=pl.BlockSpec((tq, gw), lambda b, kv, i: (b * nq + i, kv)),
        out_shape=jax.ShapeDtypeStruct((n, ATTN_WIDTH), BF16),
        name="attention",
        compiler_params=_params("parallel", "parallel", "parallel"),
    )(*args)


def _ret_kernel(lg_ref, q_ref, k_ref, v_ref, g_ref, *rest, seq, ctx_len, chunk, has_ctx):
    if has_ctx:
        kc_ref, vc_ref, o_ref, a_ref, s_ref = rest
    else:
        o_ref, a_ref, s_ref = rest
    hd = pl.program_id(1)
    lgf, lgb = lg_ref[0, hd], lg_ref[1, hd]
    c, dk = chunk, RET_DK
    n = seq // c
    scale = dk ** -0.5
    a = lax.broadcasted_iota(jnp.int32, (c, 1), 0).astype(F32)
    wq_f, wq_b = jnp.exp(lgf * (a + 1.0)), jnp.exp(lgb * (c - a))
    wk_f, wk_b = jnp.exp(lgf * (c - 1.0 - a)) * scale, jnp.exp(lgb * a) * scale
    d = (lax.broadcasted_iota(jnp.int32, (c, c), 0) - lax.broadcasted_iota(jnp.int32, (c, c), 1)).astype(F32)
    dmask = jnp.exp(jnp.where(d >= 0.0, lgf * d, -lgb * d)) * jnp.where(d == 0.0, 2.0 * scale, scale)
    zero = jnp.zeros((1, dk), F32)
    gf, gb = jnp.exp(zero + lgf * c), jnp.exp(zero + lgb * c)

    def kv_outer(kr, vr, j):
        k = kr[j * c:(j + 1) * c, :]
        kk = jnp.concatenate([k * wk_f, k * wk_b], axis=1)
        return _dot(kk.T.astype(BF16), vr[j * c:(j + 1) * c, :])

    sf = jnp.zeros((dk, dk), F32)
    sb = jnp.zeros((dk, dk), F32)
    if has_ctx:
        outer = [kv_outer(kc_ref, vc_ref, j) for j in range(ctx_len // c)]
        for o in outer:
            sf = gf * sf + o[:dk]
        for o in reversed(outer):
            sb = gb * sb + o[dk:]
    for i in range(n):
        a_ref[i] = kv_outer(k_ref, v_ref, i)
    for i in range(n):
        s_ref[i, :dk, :] = sf.astype(BF16)
        sf = gf * sf + a_ref[i, :dk, :]
    for i in reversed(range(n)):
        s_ref[i, dk:, :] = sb.astype(BF16)
        sb = gb * sb + a_ref[i, dk:, :]
    for i in range(n):
        rows = slice(i * c, (i + 1) * c)
        q = q_ref[rows, :]
        s = _dot_nt(q, k_ref[rows, :])
        o = _dot((s * dmask).astype(BF16), v_ref[rows, :])
        qq = jnp.concatenate([q * wq_f, q * wq_b], axis=1).astype(BF16)
        o = o + _dot(qq, s_ref[i])
        o_ref[rows, :] = (_silu(g_ref[rows, :].astype(F32)) * _rms(o)).astype(BF16)


def _retention(proj, log_gamma, batch, seq, proj_ctx=None, ctx_len=0):
    n = proj.shape[0]
    has_ctx = proj_ctx is not None
    chunk = 256 if seq % 256 == 0 and ctx_len % 256 == 0 else 128
    in_specs = [
        pl.BlockSpec(memory_space=pltpu.SMEM),
        pl.BlockSpec((seq, RET_DK), lambda b, h: (b, COL_RQ + h)),
        pl.BlockSpec((seq, RET_DK), lambda b, h: (b, COL_RK + h)),
        pl.BlockSpec((seq, RET_DK), lambda b, h: (b, COL_RV + h)),
        pl.BlockSpec((seq, RET_DK), lambda b, h: (b, COL_RG + h)),
    ]
    args = [log_gamma, proj, proj, proj, proj]
    if has_ctx:
        in_specs += [
            pl.BlockSpec((ctx_len, RET_DK), lambda b, h: (b, COL_RK + h)),
            pl.BlockSpec((ctx_len, RET_DK), lambda b, h: (b, COL_RV + h)),
        ]
        args += [proj_ctx, proj_ctx]
    return pl.pallas_call(
        functools.partial(_ret_kernel, seq=seq, ctx_len=ctx_len, chunk=chunk, has_ctx=has_ctx),
        grid=(batch, RET_HEADS),
        in_specs=in_specs,
        out_specs=pl.BlockSpec((seq, RET_DK), lambda b, h: (b, h)),
        out_shape=jax.ShapeDtypeStruct((n, RET_WIDTH), BF16),
        scratch_shapes=[
            pltpu.VMEM((seq // chunk, 2 * RET_DK, RET_DK), F32),
            pltpu.VMEM((seq // chunk, 2 * RET_DK, RET_DK), BF16),
        ],
        name="retention",
        compiler_params=_params("parallel", "parallel"),
    )(*args)


def _dft_cos_sin(n):
    k = np.arange(n, dtype=np.int64)
    ang = 2.0 * np.pi * ((k[:, None] * k[None, :]) % n).astype(np.float64) / n
    return np.cos(ang), np.sin(ang)


def _fourier_consts(seq):
    cg, sg = _dft_cos_sin(FOURIER_GROUP_DIM)
    norm = 1.0 / np.sqrt(float(seq) * FOURIER_GROUP_DIM)
    eye = np.eye(FOURIER_GROUPS)
    chan = np.concatenate([np.kron(eye, cg), np.kron(eye, -sg)], axis=1) * norm
    cs, ss = _dft_cos_sin(seq)
    as_bf16 = lambda a: jnp.asarray(a.astype(np.float32)).astype(BF16)
    return as_bf16(chan), as_bf16(cs), as_bf16(ss)


def _fchan_kernel(z_ref, m_ref, o_ref):
    o_ref[...] = _dot(z_ref[...], m_ref[...]).astype(BF16)


def _fseq_kernel(c_ref, s_ref, zc_ref, zs_ref, o_ref):
    o_ref[...] = (_dot(c_ref[...], zc_ref[...]) + _dot(s_ref[...], zs_ref[...])).astype(BF16)


def _fourier(proj, batch, seq, consts):
    n = proj.shape[0]
    chan, cs, ss = consts
    fw = FOURIER_WIDTH
    tm = _tile(n, (1024, 512, 256, 128))
    zcs = pl.pallas_call(
        _fchan_kernel,
        grid=(n // tm,),
        in_specs=[
            pl.BlockSpec((tm, fw), lambda i: (i, COL_FZ * 128 // fw)),
            pl.BlockSpec((fw, 2 * fw), lambda i: (0, 0)),
        ],
        out_specs=pl.BlockSpec((tm, 2 * fw), lambda i: (i, 0)),
        out_shape=jax.ShapeDtypeStruct((n, 2 * fw), BF16),
        name="fourier_chan",
        compiler_params=_params("parallel"),
    )(proj, chan)
    tk = _tile(seq, (512, 256, 128))
    nk = seq // tk
    return pl.pallas_call(
        _fseq_kernel,
        grid=(nk, batch),
        in_specs=[
            pl.BlockSpec((tk, seq), lambda k, b: (k, 0)),
            pl.BlockSpec((tk, seq), lambda k, b: (k, 0)),
            pl.BlockSpec((seq, fw), lambda k, b: (b, 0)),
            pl.BlockSpec((seq, fw), lambda k, b: (b, 1)),
        ],
        out_specs=pl.BlockSpec((tk, fw), lambda k, b: (b * nk + k, 0)),
        out_shape=jax.ShapeDtypeStruct((n, fw), BF16),
        name="fourier_seq",
        compiler_params=_params("parallel", "parallel"),
    )(cs, ss, zcs, zcs)


MERGE_TILE = 512


def _gate_kernel(u_ref, yf_ref, ya_ref, yr_ref, wf_ref, wa_ref, wr_ref,
                 gf_ref, ga_ref, gr_ref, bf_ref, ba_ref, br_ref, o_ref):
    u = u_ref[...]
    m = jax.nn.sigmoid(_dot(u, gf_ref[...]) + bf_ref[...]) * _dot(yf_ref[...], wf_ref[...])
    m = m + jax.nn.sigmoid(_dot(u, ga_ref[...]) + ba_ref[...]) * _dot(ya_ref[...], wa_ref[...])
    m = m + jax.nn.sigmoid(_dot(u, gr_ref[...]) + br_ref[...]) * _dot(yr_ref[...], wr_ref[...])
    o_ref[...] = m.astype(BF16)


def _out_kernel(h_ref, gt_ref, m_ref, wo_ref, o_ref, *, tn):
    m = m_ref[...]
    for c in range(0, o_ref.shape[1], tn):
        cols = slice(c, c + tn)
        o_ref[:, cols] = h_ref[:, cols] + gt_ref[:, cols] * _dot(m, wo_ref[:, cols])


def _merge(h, mod, row_of_tile, tm, u, y_f, y_a, y_r, w_bf, w_ba, w_br, w_mg, b_mg, w_out, layer):
    n, d = h.shape
    tn = _tile(d, (MERGE_TILE, 256, 128))
    nn = d // tn
    tg = _tile(n, (1024, 512, 256, 128))
    wspec = lambda rows: pl.BlockSpec((None, rows, tn), lambda i, j: (layer, 0, j))
    gspec = lambda k: pl.BlockSpec((None, d, tn), lambda i, j: (layer, 0, k * nn + j))
    bspec = lambda k: pl.BlockSpec((None, 1, tn), lambda i, j: (layer, 0, k * nn + j))
    yspec = lambda w: pl.BlockSpec((tg, w), lambda i, j: (i, 0))
    m = pl.pallas_call(
        _gate_kernel,
        grid=(n // tg, nn),
        in_specs=[
            yspec(d), yspec(FOURIER_WIDTH), yspec(ATTN_WIDTH), yspec(RET_WIDTH),
            wspec(FOURIER_WIDTH), wspec(ATTN_WIDTH), wspec(RET_WIDTH),
            gspec(0), gspec(1), gspec(2),
            bspec(0), bspec(1), bspec(2),
        ],
        out_specs=pl.BlockSpec((tg, tn), lambda i, j: (i, j)),
        out_shape=jax.ShapeDtypeStruct((n, d), BF16),
        name="merge_gate",
        compiler_params=_params("parallel", "parallel"),
    )(u, y_f, y_a, y_r, w_bf, w_ba, w_br, w_mg, w_mg, w_mg, b_mg, b_mg, b_mg)
    return pl.pallas_call(
        functools.partial(_out_kernel, tn=tn),
        grid=(n // tm,),
        in_specs=[
            pl.BlockSpec((tm, d), lambda i: (i, 0)),
            _mod_spec(d, row_of_tile, 5),
            pl.BlockSpec((tm, d), lambda i: (i, 0)),
            pl.BlockSpec((None, d, d), lambda i: (layer, 0, 0)),
        ],
        out_specs=pl.BlockSpec((tm, d), lambda i: (i, 0)),
        out_shape=jax.ShapeDtypeStruct((n, d), F32),
        name="merge_out",
        compiler_params=_params("parallel"),
    )(h, mod, m, w_out)


def kernel(x, c, ctx, c_ctx, w_ada, b_ada, ffn1_norm, ffn1_w_gate, ffn1_w_up, ffn1_w_down, mix_norm, w_in, q_norm, k_norm, ret_decay, w_branch_fourier, w_branch_attn, w_branch_ret, w_merge_gate, b_merge_gate, w_out, ffn2_norm, ffn2_w_gate, ffn2_w_up, ffn2_w_down, final_norm):
    batch, seq, d = x.shape
    ctx_len = ctx.shape[1]
    depth = w_ada.shape[0]
    vec = lambda g: g.reshape(depth, 1, g.shape[-1])

    rows = -(-(batch + 1) // 16) * 16
    c_all = jnp.concatenate([c, c_ctx[None], jnp.zeros((rows - batch - 1, d), F32)], axis=0)
    mod = _adaln(c_all, w_ada, b_ada).reshape(depth * rows, 1, N_MOD * d)

    tm = _tile(seq, (512, 256, 128))
    tiles_per_sample = seq // tm
    tmc = _tile(batch * ctx_len, (512, 256, 128))

    bf = lambda w: w.astype(BF16)
    ffn1 = (vec(ffn1_norm), bf(ffn1_w_gate), bf(ffn1_w_up), bf(ffn1_w_down))
    ffn2 = (vec(ffn2_norm), bf(ffn2_w_gate), bf(ffn2_w_up), bf(ffn2_w_down))
    mix_gain, w_in_t = vec(mix_norm), bf(w_in)
    qn, kn = vec(q_norm), vec(k_norm)
    w_bf, w_ba, w_br = bf(w_branch_fourier), bf(w_branch_attn), bf(w_branch_ret)
    w_mg, b_mg, w_o = bf(w_merge_gate), vec(b_merge_gate), bf(w_out)
    log_gamma = -jnp.exp(ret_decay.astype(F32))
    tables = _rope_tables(seq)
    four_lat = _fourier_consts(seq)
    four_ctx = _fourier_consts(ctx_len)

    h = x.reshape(batch * seq, d)
    hc = ctx.reshape(batch * ctx_len, d)
    for l in range(depth):
        need_ctx = l < depth - 1
        lat_row = lambda i, l=l: l * rows + i // tiles_per_sample
        ctx_row = lambda i, l=l: l * rows + batch

        h, u = _ffn(h, mod, lat_row, tm, 0, *ffn1, l, emit_gain=mix_gain)
        hc, uc = _ffn(hc, mod, ctx_row, tmc, 0, *ffn1, l, emit_gain=mix_gain)

        proj = _inproj(u, seq, w_in_t, qn, kn, l, tables)
        proj_c = _inproj(uc, ctx_len, w_in_t, qn, kn, l)

        y_a = _attention(proj, proj_c, batch, seq, ctx_len, proj)
        y_r = _retention(proj, log_gamma[l], batch, seq, proj_c, ctx_len)
        y_f = _fourier(proj, batch, seq, four_lat)
        h = _merge(h, mod, lat_row, tm, u, y_f, y_a, y_r, w_bf, w_ba, w_br, w_mg, b_mg, w_o, l)
        last = final_norm.reshape(1, d) if l == depth - 1 else None
        h = _ffn(h, mod, lat_row, tm, 6, *ffn2, l, final_gain=last)

        if need_ctx:
            yc_a = _attention(proj_c, proj_c, batch, ctx_len, ctx_len)
            yc_r = _retention(proj_c, log_gamma[l], batch, ctx_len)
            yc_f = _fourier(proj_c, batch, ctx_len, four_ctx)
            hc = _merge(hc, mod, ctx_row, tmc, uc, yc_f, yc_a, yc_r, w_bf, w_ba, w_br, w_mg, b_mg, w_o, l)
            hc = _ffn(hc, mod, ctx_row, tmc, 6, *ffn2, l)
    return h.reshape(batch, seq, d)
```

```python
import functools

import numpy as np
import jax
import jax.numpy as jnp
from jax import lax
from jax.experimental import pallas as pl
from jax.experimental.pallas import tpu as pltpu

F32 = jnp.float32
BF16 = jnp.bfloat16

EPS = 1e-6
N_MOD = 9
GRID_W = 64
HEAD_DIM = 128
HALF_ROT = HEAD_DIM // 2
ATTN_HEADS = 8
ATTN_KV_HEADS = 2
ATTN_GROUP = ATTN_HEADS // ATTN_KV_HEADS
ATTN_WIDTH = ATTN_HEADS * HEAD_DIM
KV_WIDTH = ATTN_KV_HEADS * HEAD_DIM
RET_HEADS = 4
RET_DK = 128
RET_WIDTH = RET_HEADS * RET_DK
FOURIER_GROUPS = 4
FOURIER_GROUP_DIM = 128
FOURIER_WIDTH = FOURIER_GROUPS * FOURIER_GROUP_DIM
IN_WIDTH = ATTN_WIDTH + 2 * KV_WIDTH + 4 * RET_WIDTH + FOURIER_WIDTH
ROPE_THETA = 10000.0

COL_AK = ATTN_WIDTH // 128
COL_AV = COL_AK + KV_WIDTH // 128
COL_RQ = COL_AV + KV_WIDTH // 128
COL_RK = COL_RQ + RET_WIDTH // 128
COL_RV = COL_RK + RET_WIDTH // 128
COL_RG = COL_RV + RET_WIDTH // 128
COL_FZ = COL_RG + RET_WIDTH // 128

IN_TILE = ATTN_WIDTH

V7X_VMEM_LIMIT_BYTES = 56 * 1024 * 1024


def _params(*sem):
    return pltpu.CompilerParams(dimension_semantics=sem, vmem_limit_bytes=V7X_VMEM_LIMIT_BYTES)


def _tile(n, prefs):
    for p in prefs:
        if n % p == 0:
            return p
    return n


def _dot(a, b):
    return jnp.dot(a, b, preferred_element_type=F32)


def _dot_nt(a, b):
    return lax.dot_general(a, b, (((1,), (1,)), ((), ())), preferred_element_type=F32)


def _rms(x):
    return x * lax.rsqrt(jnp.mean(x * x, axis=-1, keepdims=True) + EPS)


def _silu(x):
    return x * jax.nn.sigmoid(x)


def _adaln_kernel(c_ref, w_ref, b_ref, o_ref):
    a = _silu(c_ref[...]).astype(BF16)
    o_ref[...] = _dot(a, w_ref[...].astype(BF16)) + b_ref[...]


def _adaln(c_all, w_ada, b_ada):
    depth, d, nd = w_ada.shape
    r = c_all.shape[0]
    tn = _tile(nd, (1024, 512, 256, 128))
    return pl.pallas_call(
        _adaln_kernel,
        grid=(depth, nd // tn),
        in_specs=[
            pl.BlockSpec((r, d), lambda l, j: (0, 0)),
            pl.BlockSpec((None, d, tn), lambda l, j: (l, 0, j)),
            pl.BlockSpec((None, 1, tn), lambda l, j: (l, 0, j)),
        ],
        out_specs=pl.BlockSpec((None, r, tn), lambda l, j: (l, 0, j)),
        out_shape=jax.ShapeDtypeStruct((depth, r, nd), F32),
        name="adaln",
        compiler_params=_params("parallel", "parallel"),
    )(c_all, w_ada, b_ada.reshape(depth, 1, nd))


def _mod_spec(d, row_of_tile, k):
    return pl.BlockSpec((None, 1, d), lambda i, *_: (row_of_tile(i), 0, k))


FFN_TILE = 512
FFN_PRO_CHUNKS = 8


def _ffn_kernel(h_ref, hn_ref, sh_ref, sc_ref, shn_ref, scn_ref, gt_ref, gain_ref, wg_ref, wu_ref, wd_ref,
                *rest, nf, final, emit):
    if final:
        fg_ref, o_ref, xn_ref, xnn_ref, acc_ref = rest
    elif emit:
        sh2_ref, sc2_ref, gain2_ref, o_ref, u_ref, xn_ref, xnn_ref, acc_ref = rest
    else:
        o_ref, xn_ref, xnn_ref, acc_ref = rest
    i, f = pl.program_id(0), pl.program_id(1)
    gain = gain_ref[...]

    def normed(x, sh, sc):
        return (_rms(x) * gain * (1.0 + sc) + sh).astype(BF16)

    @pl.when((i == 0) & (f == 0))
    def _():
        xn_ref[...] = normed(h_ref[...], sh_ref[...], sc_ref[...])
        acc_ref[...] = jnp.zeros_like(acc_ref)

    chunk = hn_ref.shape[0]
    rows = pl.ds(pl.multiple_of((f % FFN_PRO_CHUNKS) * chunk, chunk), chunk)
    xnn_ref[rows, :] = normed(hn_ref[...], shn_ref[...], scn_ref[...])

    xn = xn_ref[...]
    g = _dot(xn, wg_ref[...])
    u = _dot(xn, wu_ref[...])
    acc_ref[...] = jnp.where(f == 0, 0.0, acc_ref[...]) + _dot((_silu(g) * u).astype(BF16), wd_ref[...])

    @pl.when(f == nf - 1)
    def _():
        xn_ref[...] = xnn_ref[...]
        y = h_ref[...] + (0.5 * gt_ref[...]) * acc_ref[...]
        if final:
            y = _rms(y) * fg_ref[...]
        o_ref[...] = y
        if emit:
            z = _rms(y) * gain2_ref[...]
            u_ref[...] = (z * (1.0 + sc2_ref[...]) + sh2_ref[...]).astype(BF16)


def _ffn(h, mod, row_of_tile, tm, k0, gain, wg, wu, wd, layer, final_gain=None, emit_gain=None):
    n, d = h.shape
    tf = _tile(wg.shape[-1], (FFN_TILE, 256, 128))
    nf = wg.shape[-1] // tf
    assert nf >= FFN_PRO_CHUNKS and tm % (8 * FFN_PRO_CHUNKS) == 0
    last = n // tm - 1
    nxt = lambda i: jnp.minimum(i + 1, last)
    final, emit = final_gain is not None, emit_gain is not None
    vec_spec = pl.BlockSpec((None, 1, d), lambda i, f: (layer, 0, 0))
    in_specs = [
        pl.BlockSpec((tm, d), lambda i, f: (i, 0)),
        pl.BlockSpec((tm // FFN_PRO_CHUNKS, d), lambda i, f: (nxt(i) * FFN_PRO_CHUNKS + f % FFN_PRO_CHUNKS, 0)),
        _mod_spec(d, row_of_tile, k0),
        _mod_spec(d, row_of_tile, k0 + 1),
        _mod_spec(d, lambda i: row_of_tile(nxt(i)), k0),
        _mod_spec(d, lambda i: row_of_tile(nxt(i)), k0 + 1),
        _mod_spec(d, row_of_tile, k0 + 2),
        vec_spec,
        pl.BlockSpec((None, d, tf), lambda i, f: (layer, 0, f)),
        pl.BlockSpec((None, d, tf), lambda i, f: (layer, 0, f)),
        pl.BlockSpec((None, tf, d), lambda i, f: (layer, f, 0)),
    ]
    args = [h, h, mod, mod, mod, mod, mod, gain, wg, wu, wd]
    row_spec = pl.BlockSpec((tm, d), lambda i, f: (i, 0))
    out_specs, out_shape = row_spec, jax.ShapeDtypeStruct((n, d), F32)
    if final:
        in_specs.append(pl.BlockSpec((1, d), lambda i, f: (0, 0)))
        args.append(final_gain)
    elif emit:
        in_specs += [_mod_spec(d, row_of_tile, 3), _mod_spec(d, row_of_tile, 4), vec_spec]
        args += [mod, mod, emit_gain]
        out_specs = [row_spec, row_spec]
        out_shape = [out_shape, jax.ShapeDtypeStruct((n, d), BF16)]
    return pl.pallas_call(
        functools.partial(_ffn_kernel, nf=nf, final=final, emit=emit),
        grid=(n // tm, nf),
        in_specs=in_specs,
        out_specs=out_specs,
        out_shape=out_shape,
        scratch_shapes=[pltpu.VMEM((tm, d), BF16), pltpu.VMEM((tm, d), BF16), pltpu.VMEM((tm, d), F32)],
        name="ffn",
        compiler_params=_params("arbitrary", "arbitrary"),
    )(*args)


def _rope(x, cos, sin_lo, sin_hi):
    return x * cos + pltpu.roll(x, HEAD_DIM - HALF_ROT // 2, 1) * sin_lo + pltpu.roll(x, HALF_ROT // 2, 1) * sin_hi


def _inproj_kernel(u_ref, w_ref, qn_ref, kn_ref, *rest, rope):
    if rope:
        cos_ref, slo_ref, shi_ref, o_ref = rest
    else:
        (o_ref,) = rest
    j = pl.program_id(1)
    acc = _dot(u_ref[...], w_ref[...])

    def head(hd, gain, scale):
        sl = slice(hd * HEAD_DIM, (hd + 1) * HEAD_DIM)
        y = _rms(acc[:, sl]) * gain
        if rope:
            y = _rope(y, cos_ref[...], slo_ref[...], shi_ref[...])
        o_ref[:, sl] = (y * scale).astype(BF16)

    @pl.when(j == 0)
    def _():
        for hd in range(ATTN_HEADS):
            head(hd, qn_ref[...], HEAD_DIM ** -0.5)

    @pl.when(j == 1)
    def _():
        for hd in range(ATTN_KV_HEADS):
            head(hd, kn_ref[...], 1.0)
        o_ref[:, KV_WIDTH:] = acc[:, KV_WIDTH:].astype(BF16)

    @pl.when(j >= 2)
    def _():
        o_ref[...] = acc.astype(BF16)


def _inproj(u, seq, w_in, q_norm, k_norm, layer, tables=None):
    n, d = u.shape
    tn = IN_TILE
    nj = IN_WIDTH // tn
    assert w_in.shape[-1] == IN_WIDTH and nj * tn == IN_WIDTH
    rope = tables is not None
    tm = _tile(seq, (1024, 512, 256, 128)) if rope else _tile(n, (1024, 512, 256, 128))
    norm_spec = pl.BlockSpec((None, 1, HEAD_DIM), lambda i, j: (layer, 0, 0))
    in_specs = [
        pl.BlockSpec((tm, d), lambda i, j: (i, 0)),
        pl.BlockSpec((None, d, tn), lambda i, j: (layer, 0, j)),
        norm_spec,
        norm_spec,
    ]
    args = [u, w_in, q_norm, k_norm]
    if rope:
        nt = seq // tm
        in_specs += [pl.BlockSpec((tm, HEAD_DIM), lambda i, j: (i % nt, 0))] * 3
        args += list(tables)
    return pl.pallas_call(
        functools.partial(_inproj_kernel, rope=rope),
        grid=(n // tm, nj),
        in_specs=in_specs,
        out_specs=pl.BlockSpec((tm, tn), lambda i, j: (i, j)),
        out_shape=jax.ShapeDtypeStruct((n, IN_WIDTH), BF16),
        name="inproj",
        compiler_params=_params("parallel", "arbitrary"),
    )(*args)


def _rope_tables(seq):
    t = jnp.arange(seq)
    row = (t // GRID_W).astype(F32)
    col = (t % GRID_W).astype(F32)
    inv = ROPE_THETA ** (-jnp.arange(0, HALF_ROT, 2, dtype=F32) / HALF_ROT)
    ar, ac = row[:, None] * inv[None], col[:, None] * inv[None]
    zero = jnp.zeros_like(ar)
    cos = jnp.concatenate([jnp.cos(ar), jnp.cos(ar), jnp.cos(ac), jnp.cos(ac)], axis=-1)
    sin_lo = jnp.concatenate([-jnp.sin(ar), zero, -jnp.sin(ac), zero], axis=-1)
    sin_hi = jnp.concatenate([zero, jnp.sin(ar), zero, jnp.sin(ac)], axis=-1)
    return cos, sin_lo, sin_hi


def _attn_kernel(q_ref, kc_ref, vc_ref, *rest, has_lat):
    if has_lat:
        kl_ref, vl_ref, o_ref = rest
        kl, vl = kl_ref[...], vl_ref[...]
    else:
        (o_ref,) = rest
    kc, vc = kc_ref[...], vc_ref[...]
    for g in range(ATTN_GROUP):
        sl = slice(g * HEAD_DIM, (g + 1) * HEAD_DIM)
        q = q_ref[:, sl]
        sc = _dot_nt(q, kc)
        m = jnp.max(sc, axis=-1, keepdims=True)
        if has_lat:
            s_lat = _dot_nt(q, kl)
            m = jnp.maximum(m, jnp.max(s_lat, axis=-1, keepdims=True))
        pc = jnp.exp(sc - m)
        den = jnp.sum(pc, axis=-1, keepdims=True)
        o = _dot(pc.astype(BF16), vc)
        if has_lat:
            p_lat = jnp.exp(s_lat - m)
            den = den + jnp.sum(p_lat, axis=-1, keepdims=True)
            o = o + _dot(p_lat.astype(BF16), vl)
        o_ref[:, sl] = (o / den).astype(BF16)


def _attention(proj_q, proj_c, batch, seq, ctx_len, proj_kv=None):
    n = proj_q.shape[0]
    tq = _tile(seq, (256, 128))
    nq = seq // tq
    gw = ATTN_GROUP * HEAD_DIM
    has_lat = proj_kv is not None
    in_specs = [
        pl.BlockSpec((tq, gw), lambda b, kv, i: (b * nq + i, kv)),
        pl.BlockSpec((ctx_len, HEAD_DIM), lambda b, kv, i: (b, COL_AK + kv)),
        pl.BlockSpec((ctx_len, HEAD_DIM), lambda b, kv, i: (b, COL_AV + kv)),
    ]
    args = [proj_q, proj_c, proj_c]
    if has_lat:
        in_specs += [
            pl.BlockSpec((seq, HEAD_DIM), lambda b, kv, i: (b, COL_AK + kv)),
            pl.BlockSpec((seq, HEAD_DIM), lambda b, kv, i: (b, COL_AV + kv)),
        ]
        args += [proj_kv, proj_kv]
    return pl.pallas_call(
        functools.partial(_attn_kernel, has_lat=has_lat),
        grid=(batch, ATTN_KV_HEADS, nq),
        in_specs=in_specs,
        out_specs=pl.BlockSpec((tq, gw), lambda b, kv, i: (b * nq + i, kv)),
        out_shape=jax.ShapeDtypeStruct((n, ATTN_WIDTH), BF16),
        name="attention",
        compiler_params=_params("parallel", "parallel", "parallel"),
    )(*args)


def _ret_kernel(lg_ref, q_ref, k_ref, v_ref, g_ref, *rest, seq, ctx_len, chunk, has_ctx):
    if has_ctx:
        kc_ref, vc_ref, o_ref, a_ref, s_ref = rest
    else:
        o_ref, a_ref, s_ref = rest
    hd = pl.program_id(1)
    lgf, lgb = lg_ref[0, hd], lg_ref[1, hd]
    c, dk = chunk, RET_DK
    n = seq // c
    scale = dk ** -0.5
    a = lax.broadcasted_iota(jnp.int32, (c, 1), 0).astype(F32)
    wq_f, wq_b = jnp.exp(lgf * (a + 1.0)), jnp.exp(lgb * (c - a))
    wk_f, wk_b = jnp.exp(lgf * (c - 1.0 - a)) * scale, jnp.exp(lgb * a) * scale
    d = (lax.broadcasted_iota(jnp.int32, (c, c), 0) - lax.broadcasted_iota(jnp.int32, (c, c), 1)).astype(F32)
    dmask = jnp.exp(jnp.where(d >= 0.0, lgf * d, -lgb * d)) * jnp.where(d == 0.0, 2.0 * scale, scale)
    zero = jnp.zeros((1, dk), F32)
    gf, gb = jnp.exp(zero + lgf * c), jnp.exp(zero + lgb * c)

    def kv_outer(kr, vr, j):
        k = kr[j * c:(j + 1) * c, :]
        kk = jnp.concatenate([k * wk_f, k * wk_b], axis=1)
        return _dot(kk.T.astype(BF16), vr[j * c:(j + 1) * c, :])

    sf = jnp.zeros((dk, dk), F32)
    sb = jnp.zeros((dk, dk), F32)
    if has_ctx:
        outer = [kv_outer(kc_ref, vc_ref, j) for j in range(ctx_len // c)]
        for o in outer:
            sf = gf * sf + o[:dk]
        for o in reversed(outer):
            sb = gb * sb + o[dk:]
    for i in range(n):
        a_ref[i] = kv_outer(k_ref, v_ref, i)
    for i in range(n):
        s_ref[i, :dk, :] = sf.astype(BF16)
        sf = gf * sf + a_ref[i, :dk, :]
    for i in reversed(range(n)):
        s_ref[i, dk:, :] = sb.astype(BF16)
        sb = gb * sb + a_ref[i, dk:, :]
    for i in range(n):
        rows = slice(i * c, (i + 1) * c)
        q = q_ref[rows, :]
        s = _dot_nt(q, k_ref[rows, :])
        o = _dot((s * dmask).astype(BF16), v_ref[rows, :])
        qq = jnp.concatenate([q * wq_f, q * wq_b], axis=1).astype(BF16)
        o = o + _dot(qq, s_ref[i])
        o_ref[rows, :] = (_silu(g_ref[rows, :].astype(F32)) * _rms(o)).astype(BF16)


def _retention(proj, log_gamma, batch, seq, proj_ctx=None, ctx_len=0):
    n = proj.shape[0]
    has_ctx = proj_ctx is not None
    chunk = 256 if seq % 256 == 0 and ctx_len % 256 == 0 else 128
    in_specs = [
        pl.BlockSpec(memory_space=pltpu.SMEM),
        pl.BlockSpec((seq, RET_DK), lambda b, h: (b, COL_RQ + h)),
        pl.BlockSpec((seq, RET_DK), lambda b, h: (b, COL_RK + h)),
        pl.BlockSpec((seq, RET_DK), lambda b, h: (b, COL_RV + h)),
        pl.BlockSpec((seq, RET_DK), lambda b, h: (b, COL_RG + h)),
    ]
    args = [log_gamma, proj, proj, proj, proj]
    if has_ctx:
        in_specs += [
            pl.BlockSpec((ctx_len, RET_DK), lambda b, h: (b, COL_RK + h)),
            pl.BlockSpec((ctx_len, RET_DK), lambda b, h: (b, COL_RV + h)),
        ]
        args += [proj_ctx, proj_ctx]
    return pl.pallas_call(
        functools.partial(_ret_kernel, seq=seq, ctx_len=ctx_len, chunk=chunk, has_ctx=has_ctx),
        grid=(batch, RET_HEADS),
        in_specs=in_specs,
        out_specs=pl.BlockSpec((seq, RET_DK), lambda b, h: (b, h)),
        out_shape=jax.ShapeDtypeStruct((n, RET_WIDTH), BF16),
        scratch_shapes=[
            pltpu.VMEM((seq // chunk, 2 * RET_DK, RET_DK), F32),
            pltpu.VMEM((seq // chunk, 2 * RET_DK, RET_DK), BF16),
        ],
        name="retention",
        compiler_params=_params("parallel", "parallel"),
    )(*args)


def _dft_cos_sin(n):
    k = np.arange(n, dtype=np.int64)
    ang = 2.0 * np.pi * ((k[:, None] * k[None, :]) % n).astype(np.float64) / n
    return np.cos(ang), np.sin(ang)


def _fourier_consts(seq):
    cg, sg = _dft_cos_sin(FOURIER_GROUP_DIM)
    norm = 1.0 / np.sqrt(float(seq) * FOURIER_GROUP_DIM)
    eye = np.eye(FOURIER_GROUPS)
    chan = np.concatenate([np.kron(eye, cg), np.kron(eye, -sg)], axis=1) * norm
    cs, ss = _dft_cos_sin(seq)
    as_bf16 = lambda a: jnp.asarray(a.astype(np.float32)).astype(BF16)
    return as_bf16(chan), as_bf16(cs), as_bf16(ss)


def _fchan_kernel(z_ref, m_ref, o_ref):
    o_ref[...] = _dot(z_ref[...], m_ref[...]).astype(BF16)


def _fseq_kernel(c_ref, s_ref, zc_ref, zs_ref, o_ref):
    o_ref[...] = (_dot(c_ref[...], zc_ref[...]) + _dot(s_ref[...], zs_ref[...])).astype(BF16)


def _fourier(proj, batch, seq, consts):
    n = proj.shape[0]
    chan, cs, ss = consts
    fw = FOURIER_WIDTH
    tm = _tile(n, (1024, 512, 256, 128))
    zcs = pl.pallas_call(
        _fchan_kernel,
        grid=(n // tm,),
        in_specs=[
            pl.BlockSpec((tm, fw), lambda i: (i, COL_FZ * 128 // fw)),
            pl.BlockSpec((fw, 2 * fw), lambda i: (0, 0)),
        ],
        out_specs=pl.BlockSpec((tm, 2 * fw), lambda i: (i, 0)),
        out_shape=jax.ShapeDtypeStruct((n, 2 * fw), BF16),
        name="fourier_chan",
        compiler_params=_params("parallel"),
    )(proj, chan)
    tk = _tile(seq, (512, 256, 128))
    nk = seq // tk
    return pl.pallas_call(
        _fseq_kernel,
        grid=(nk, batch),
        in_specs=[
            pl.BlockSpec((tk, seq), lambda k, b: (k, 0)),
            pl.BlockSpec((tk, seq), lambda k, b: (k, 0)),
            pl.BlockSpec((seq, fw), lambda k, b: (b, 0)),
            pl.BlockSpec((seq, fw), lambda k, b: (b, 1)),
        ],
        out_specs=pl.BlockSpec((tk, fw), lambda k, b: (b * nk + k, 0)),
        out_shape=jax.ShapeDtypeStruct((n, fw), BF16),
        name="fourier_seq",
        compiler_params=_params("parallel", "parallel"),
    )(cs, ss, zcs, zcs)


MERGE_TILE = 512


def _gate_kernel(u_ref, yf_ref, ya_ref, yr_ref, wf_ref, wa_ref, wr_ref,
                 gf_ref, ga_ref, gr_ref, bf_ref, ba_ref, br_ref, o_ref):
    u = u_ref[...]
    m = jax.nn.sigmoid(_dot(u, gf_ref[...]) + bf_ref[...]) * _dot(yf_ref[...], wf_ref[...])
    m = m + jax.nn.sigmoid(_dot(u, ga_ref[...]) + ba_ref[...]) * _dot(ya_ref[...], wa_ref[...])
    m = m + jax.nn.sigmoid(_dot(u, gr_ref[...]) + br_ref[...]) * _dot(yr_ref[...], wr_ref[...])
    o_ref[...] = m.astype(BF16)


def _out_kernel(h_ref, gt_ref, m_ref, wo_ref, o_ref, *, tn):
    m = m_ref[...]
    for c in range(0, o_ref.shape[1], tn):
        cols = slice(c, c + tn)
        o_ref[:, cols] = h_ref[:, cols] + gt_ref[:, cols] * _dot(m, wo_ref[:, cols])


def _merge(h, mod, row_of_tile, tm, u, y_f, y_a, y_r, w_bf, w_ba, w_br, w_mg, b_mg, w_out, layer):
    n, d = h.shape
    tn = _tile(d, (MERGE_TILE, 256, 128))
    nn = d // tn
    tg = _tile(n, (1024, 512, 256, 128))
    wspec = lambda rows: pl.BlockSpec((None, rows, tn), lambda i, j: (layer, 0, j))
    gspec = lambda k: pl.BlockSpec((None, d, tn), lambda i, j: (layer, 0, k * nn + j))
    bspec = lambda k: pl.BlockSpec((None, 1, tn), lambda i, j: (layer, 0, k * nn + j))
    yspec = lambda w: pl.BlockSpec((tg, w), lambda i, j: (i, 0))
    m = pl.pallas_call(
        _gate_kernel,
        grid=(n // tg, nn),
        in_specs=[
            yspec(d), yspec(FOURIER_WIDTH), yspec(ATTN_WIDTH), yspec(RET_WIDTH),
            wspec(FOURIER_WIDTH), wspec(ATTN_WIDTH), wspec(RET_WIDTH),
            gspec(0), gspec(1), gspec(2),
            bspec(0), bspec(1), bspec(2),
        ],
        out_specs=pl.BlockSpec((tg, tn), lambda i, j: (i, j)),
        out_shape=jax.ShapeDtypeStruct((n, d), BF16),
        name="merge_gate",
        compiler_params=_params("parallel", "parallel"),
    )(u, y_f, y_a, y_r, w_bf, w_ba, w_br, w_mg, w_mg, w_mg, b_mg, b_mg, b_mg)
    return pl.pallas_call(
        functools.partial(_out_kernel, tn=tn),
        grid=(n // tm,),
        in_specs=[
            pl.BlockSpec((tm, d), lambda i: (i, 0)),
            _mod_spec(d, row_of_tile, 5),
            pl.BlockSpec((tm, d), lambda i: (i, 0)),
            pl.BlockSpec((None, d, d), lambda i: (layer, 0, 0)),
        ],
        out_specs=pl.BlockSpec((tm, d), lambda i: (i, 0)),
        out_shape=jax.ShapeDtypeStruct((n, d), F32),
        name="merge_out",
        compiler_params=_params("parallel"),
    )(h, mod, m, w_out)


def kernel(x, c, ctx, c_ctx, w_ada, b_ada, ffn1_norm, ffn1_w_gate, ffn1_w_up, ffn1_w_down, mix_norm, w_in, q_norm, k_norm, ret_decay, w_branch_fourier, w_branch_attn, w_branch_ret, w_merge_gate, b_merge_gate, w_out, ffn2_norm, ffn2_w_gate, ffn2_w_up, ffn2_w_down, final_norm):
    batch, seq, d = x.shape
    ctx_len = ctx.shape[1]
    depth = w_ada.shape[0]
    vec = lambda g: g.reshape(depth, 1, g.shape[-1])

    rows = -(-(batch + 1) // 16) * 16
    c_all = jnp.concatenate([c, c_ctx[None], jnp.zeros((rows - batch - 1, d), F32)], axis=0)
    mod = _adaln(c_all, w_ada, b_ada).reshape(depth * rows, 1, N_MOD * d)

    tm = _tile(seq, (512, 256, 128))
    tiles_per_sample = seq // tm
    tmc = _tile(batch * ctx_len, (512, 256, 128))

    bf = lambda w: w.astype(BF16)
    ffn1 = (vec(ffn1_norm), bf(ffn1_w_gate), bf(ffn1_w_up), bf(ffn1_w_down))
    ffn2 = (vec(ffn2_norm), bf(ffn2_w_gate), bf(ffn2_w_up), bf(ffn2_w_down))
    mix_gain, w_in_t = vec(mix_norm), bf(w_in)
    qn, kn = vec(q_norm), vec(k_norm)
    w_bf, w_ba, w_br = bf(w_branch_fourier), bf(w_branch_attn), bf(w_branch_ret)
    w_mg, b_mg, w_o = bf(w_merge_gate), vec(b_merge_gate), bf(w_out)
    log_gamma = -jnp.exp(ret_decay.astype(F32))
    tables = _rope_tables(seq)
    four_lat = _fourier_consts(seq)
    four_ctx = _fourier_consts(ctx_len)

    h = x.reshape(batch * seq, d)
    hc = ctx.reshape(batch * ctx_len, d)
    for l in range(depth):
        need_ctx = l < depth - 1
        lat_row = lambda i, l=l: l * rows + i // tiles_per_sample
        ctx_row = lambda i, l=l: l * rows + batch

        h, u = _ffn(h, mod, lat_row, tm, 0, *ffn1, l, emit_gain=mix_gain)
        hc, uc = _ffn(hc, mod, ctx_row, tmc, 0, *ffn1, l, emit_gain=mix_gain)

        proj = _inproj(u, seq, w_in_t, qn, kn, l, tables)
        proj_c = _inproj(uc, ctx_len, w_in_t, qn, kn, l)

        y_a = _attention(proj, proj_c, batch, seq, ctx_len, proj)
        y_r = _retention(proj, log_gamma[l], batch, seq, proj_c, ctx_len)
        y_f = _fourier(proj, batch, seq, four_lat)
        h = _merge(h, mod, lat_row, tm, u, y_f, y_a, y_r, w_bf, w_ba, w_br, w_mg, b_mg, w_o, l)
        last = final_norm.reshape(1, d) if l == depth - 1 else None
        h = _ffn(h, mod, lat_row, tm, 6, *ffn2, l, final_gain=last)

        if need_ctx:
            yc_a = _attention(proj_c, proj_c, batch, ctx_len, ctx_len)
            yc_r = _retention(proj_c, log_gamma[l], batch, ctx_len)
            yc_f = _fourier(proj_c, batch, ctx_len, four_ctx)
            hc = _merge(hc, mod, ctx_row, tmc, uc, yc_f, yc_a, yc_r, w_bf, w_ba, w_br, w_mg, b_mg, w_o, l)
            hc = _ffn(hc, mod, ctx_row, tmc, 6, *ffn2, l)
    return h.reshape(batch, seq, d)
```

```python
import functools

import numpy as np
import jax
import jax.numpy as jnp
from jax import lax
from jax.experimental import pallas as pl
from jax.experimental.pallas import tpu as pltpu

F32 = jnp.float32
BF16 = jnp.bfloat16

EPS = 1e-6
N_MOD = 9
GRID_W = 64
HEAD_DIM = 128
HALF_ROT = HEAD_DIM // 2
ATTN_HEADS = 8
ATTN_KV_HEADS = 2
ATTN_GROUP = ATTN_HEADS // ATTN_KV_HEADS
ATTN_WIDTH = ATTN_HEADS * HEAD_DIM
KV_WIDTH = ATTN_KV_HEADS * HEAD_DIM
RET_HEADS = 4
RET_DK = 128
RET_WIDTH = RET_HEADS * RET_DK
FOURIER_GROUPS = 4
FOURIER_GROUP_DIM = 128
FOURIER_WIDTH = FOURIER_GROUPS * FOURIER_GROUP_DIM
IN_WIDTH = ATTN_WIDTH + 2 * KV_WIDTH + 4 * RET_WIDTH + FOURIER_WIDTH
ROPE_THETA = 10000.0

COL_AK = ATTN_WIDTH // 128
COL_AV = COL_AK + KV_WIDTH // 128
COL_RQ = COL_AV + KV_WIDTH // 128
COL_RK = COL_RQ + RET_WIDTH // 128
COL_RV = COL_RK + RET_WIDTH // 128
COL_RG = COL_RV + RET_WIDTH // 128
COL_FZ = COL_RG + RET_WIDTH // 128

IN_TILE = ATTN_WIDTH

ATTN_Q_SCALE = HEAD_DIM ** -0.5 * float(np.log2(np.e))

V7X_VMEM_LIMIT_BYTES = 56 * 1024 * 1024


def _params(*sem):
    return pltpu.CompilerParams(dimension_semantics=sem, vmem_limit_bytes=V7X_VMEM_LIMIT_BYTES)


def _tile(n, prefs):
    for p in prefs:
        if n % p == 0:
            return p
    return n


def _dot(a, b):
    return jnp.dot(a, b, preferred_element_type=F32)


def _dot_nt(a, b):
    return lax.dot_general(a, b, (((1,), (1,)), ((), ())), preferred_element_type=F32)


def _rms(x):
    return x * lax.rsqrt(jnp.mean(x * x, axis=-1, keepdims=True) + EPS)


def _silu(x):
    return x * jax.nn.sigmoid(x)


def _adaln_kernel(c_ref, w_ref, b_ref, o_ref):
    a = _silu(c_ref[...]).astype(BF16)
    o_ref[...] = _dot(a, w_ref[...].astype(BF16)) + b_ref[...]


def _adaln(c_all, w_ada, b_ada):
    depth, d, nd = w_ada.shape
    r = c_all.shape[0]
    tn = _tile(nd, (1024, 512, 256, 128))
    return pl.pallas_call(
        _adaln_kernel,
        grid=(depth, nd // tn),
        in_specs=[
            pl.BlockSpec((r, d), lambda l, j: (0, 0)),
            pl.BlockSpec((None, d, tn), lambda l, j: (l, 0, j)),
            pl.BlockSpec((None, 1, tn), lambda l, j: (l, 0, j)),
        ],
        out_specs=pl.BlockSpec((None, r, tn), lambda l, j: (l, 0, j)),
        out_shape=jax.ShapeDtypeStruct((depth, r, nd), F32),
        name="adaln",
        compiler_params=_params("parallel", "parallel"),
    )(c_all, w_ada, b_ada.reshape(depth, 1, nd))


def _mod_spec(d, row_of_tile, k):
    return pl.BlockSpec((None, 1, d), lambda i, *_: (row_of_tile(i), 0, k))


FFN_TILE = 512


def _ffn_kernel(h_ref, sh_ref, sc_ref, gt_ref, gain_ref, wg_ref, wu_ref, wd_ref, *rest, nf, final, emit):
    if final:
        fg_ref, o_ref, xn_ref, acc_ref = rest
    elif emit:
        sh2_ref, sc2_ref, gain2_ref, o_ref, u_ref, xn_ref, acc_ref = rest
    else:
        o_ref, xn_ref, acc_ref = rest
    f = pl.program_id(1)

    @pl.when(f == 0)
    def _():
        y = _rms(h_ref[...]) * gain_ref[...]
        xn_ref[...] = (y * (1.0 + sc_ref[...]) + sh_ref[...]).astype(BF16)
        acc_ref[...] = jnp.zeros_like(acc_ref)

    xn = xn_ref[...]
    g = _dot(xn, wg_ref[...])
    u = _dot(xn, wu_ref[...])
    acc_ref[...] += _dot((_silu(g) * u).astype(BF16), wd_ref[...])

    @pl.when(f == nf - 1)
    def _():
        y = h_ref[...] + (0.5 * gt_ref[...]) * acc_ref[...]
        if final:
            y = _rms(y) * fg_ref[...]
        o_ref[...] = y
        if emit:
            z = _rms(y) * gain2_ref[...]
            u_ref[...] = (z * (1.0 + sc2_ref[...]) + sh2_ref[...]).astype(BF16)


def _ffn(h, mod, row_of_tile, tm, k0, gain, wg, wu, wd, layer, final_gain=None, emit_gain=None):
    n, d = h.shape
    tf = _tile(wg.shape[-1], (FFN_TILE, 256, 128))
    nf = wg.shape[-1] // tf
    final, emit = final_gain is not None, emit_gain is not None
    vec_spec = pl.BlockSpec((None, 1, d), lambda i, f: (layer, 0, 0))
    in_specs = [
        pl.BlockSpec((tm, d), lambda i, f: (i, 0)),
        _mod_spec(d, row_of_tile, k0),
        _mod_spec(d, row_of_tile, k0 + 1),
        _mod_spec(d, row_of_tile, k0 + 2),
        vec_spec,
        pl.BlockSpec((None, d, tf), lambda i, f: (layer, 0, f)),
        pl.BlockSpec((None, d, tf), lambda i, f: (layer, 0, f)),
        pl.BlockSpec((None, tf, d), lambda i, f: (layer, f, 0)),
    ]
    args = [h, mod, mod, mod, gain, wg, wu, wd]
    row_spec = pl.BlockSpec((tm, d), lambda i, f: (i, 0))
    out_specs, out_shape = row_spec, jax.ShapeDtypeStruct((n, d), F32)
    if final:
        in_specs.append(pl.BlockSpec((1, d), lambda i, f: (0, 0)))
        args.append(final_gain)
    elif emit:
        in_specs += [_mod_spec(d, row_of_tile, 3), _mod_spec(d, row_of_tile, 4), vec_spec]
        args += [mod, mod, emit_gain]
        out_specs = [row_spec, row_spec]
        out_shape = [out_shape, jax.ShapeDtypeStruct((n, d), BF16)]
    return pl.pallas_call(
        functools.partial(_ffn_kernel, nf=nf, final=final, emit=emit),
        grid=(n // tm, nf),
        in_specs=in_specs,
        out_specs=out_specs,
        out_shape=out_shape,
        scratch_shapes=[pltpu.VMEM((tm, d), BF16), pltpu.VMEM((tm, d), F32)],
        name="ffn",
        compiler_params=_params("parallel", "arbitrary"),
    )(*args)


def _inproj_kernel(u_ref, w_ref, o_ref):
    o_ref[...] = _dot(u_ref[...], w_ref[...]).astype(BF16)


def _inproj(u, w_in, layer):
    n, d = u.shape
    tn = IN_TILE
    assert w_in.shape[-1] == IN_WIDTH and IN_WIDTH % tn == 0
    tm = _tile(n, (1024, 512, 256, 128))
    return pl.pallas_call(
        _inproj_kernel,
        grid=(n // tm, IN_WIDTH // tn),
        in_specs=[
            pl.BlockSpec((tm, d), lambda i, j: (i, 0)),
            pl.BlockSpec((None, d, tn), lambda i, j: (layer, 0, j)),
        ],
        out_specs=pl.BlockSpec((tm, tn), lambda i, j: (i, j)),
        out_shape=jax.ShapeDtypeStruct((n, IN_WIDTH), BF16),
        name="inproj",
        compiler_params=_params("parallel", "parallel"),
    )(u, w_in)


def _rope_tables(seq):
    t = jnp.arange(seq)
    row = (t // GRID_W).astype(F32)
    col = (t % GRID_W).astype(F32)
    inv = ROPE_THETA ** (-jnp.arange(0, HALF_ROT, 2, dtype=F32) / HALF_ROT)
    ar, ac = row[:, None] * inv[None], col[:, None] * inv[None]
    zero = jnp.zeros_like(ar)
    cos = jnp.concatenate([jnp.cos(ar), jnp.cos(ar), jnp.cos(ac), jnp.cos(ac)], axis=-1)
    sin_lo = jnp.concatenate([-jnp.sin(ar), zero, -jnp.sin(ac), zero], axis=-1)
    sin_hi = jnp.concatenate([zero, jnp.sin(ar), zero, jnp.sin(ac)], axis=-1)
    return cos, sin_lo, sin_hi


def _rope(x, cos, sin_lo, sin_hi):
    return x * cos + pltpu.roll(x, HEAD_DIM - HALF_ROT // 2, 1) * sin_lo + pltpu.roll(x, HALF_ROT // 2, 1) * sin_hi


def _attn_kernel(q_ref, kc_ref, vc_ref, qn_ref, kn_ref, *rest, has_lat):
    if has_lat:
        kl_ref, vl_ref, qcos_ref, qlo_ref, qhi_ref, kcos_ref, klo_ref, khi_ref, o_ref, k_ref, v_ref = rest
    else:
        o_ref, k_ref, v_ref = rest
    nc = kc_ref.shape[0]

    @pl.when(pl.program_id(1) == 0)
    def _():
        kn = kn_ref[...]
        for kv in range(ATTN_KV_HEADS):
            sl = slice(kv * HEAD_DIM, (kv + 1) * HEAD_DIM)
            k_ref[kv, :nc, :] = (_rms(kc_ref[:, sl].astype(F32)) * kn).astype(BF16)
            v_ref[kv, :nc, :HEAD_DIM] = vc_ref[:, sl]
            if has_lat:
                kl = _rms(kl_ref[:, sl].astype(F32)) * kn
                k_ref[kv, nc:, :] = _rope(kl, kcos_ref[...], klo_ref[...], khi_ref[...]).astype(BF16)
                v_ref[kv, nc:, :HEAD_DIM] = vl_ref[:, sl]
            v_ref[kv, :, HEAD_DIM:] = jnp.ones((v_ref.shape[1], HEAD_DIM), BF16)

    qn = qn_ref[...]
    for hd in range(ATTN_HEADS):
        sl = slice(hd * HEAD_DIM, (hd + 1) * HEAD_DIM)
        q = _rms(q_ref[:, sl].astype(F32)) * qn
        if has_lat:
            q = _rope(q, qcos_ref[...], qlo_ref[...], qhi_ref[...])
        s = _dot_nt((q * ATTN_Q_SCALE).astype(BF16), k_ref[hd // ATTN_GROUP])
        p = jnp.exp2(s - jnp.max(s, axis=-1, keepdims=True)).astype(BF16)
        ov = _dot(p, v_ref[hd // ATTN_GROUP])
        o_ref[:, sl] = (ov[:, :HEAD_DIM] / ov[:, HEAD_DIM:]).astype(BF16)


def _attention(proj_q, proj_c, q_norm, k_norm, layer, batch, seq, ctx_len, proj_kv=None, tables=None):
    n = proj_q.shape[0]
    tq = _tile(seq, (256, 128))
    nq = seq // tq
    has_lat = proj_kv is not None
    n_keys = ctx_len + (seq if has_lat else 0)
    col_k, col_v = COL_AK * 128 // KV_WIDTH, COL_AV * 128 // KV_WIDTH
    norm_spec = pl.BlockSpec((None, 1, HEAD_DIM), lambda b, i: (layer, 0, 0))
    in_specs = [
        pl.BlockSpec((tq, ATTN_WIDTH), lambda b, i: (b * nq + i, 0)),
        pl.BlockSpec((ctx_len, KV_WIDTH), lambda b, i: (b, col_k)),
        pl.BlockSpec((ctx_len, KV_WIDTH), lambda b, i: (b, col_v)),
        norm_spec,
        norm_spec,
    ]
    args = [proj_q, proj_c, proj_c, q_norm, k_norm]
    if has_lat:
        in_specs += [
            pl.BlockSpec((seq, KV_WIDTH), lambda b, i: (b, col_k)),
            pl.BlockSpec((seq, KV_WIDTH), lambda b, i: (b, col_v)),
        ]
        in_specs += [pl.BlockSpec((tq, HEAD_DIM), lambda b, i: (i, 0))] * 3
        in_specs += [pl.BlockSpec((seq, HEAD_DIM), lambda b, i: (0, 0))] * 3
        args += [proj_kv, proj_kv, *tables, *tables]
    return pl.pallas_call(
        functools.partial(_attn_kernel, has_lat=has_lat),
        grid=(batch, nq),
        in_specs=in_specs,
        out_specs=pl.BlockSpec((tq, ATTN_WIDTH), lambda b, i: (b * nq + i, 0)),
        out_shape=jax.ShapeDtypeStruct((n, ATTN_WIDTH), BF16),
        scratch_shapes=[
            pltpu.VMEM((ATTN_KV_HEADS, n_keys, HEAD_DIM), BF16),
            pltpu.VMEM((ATTN_KV_HEADS, n_keys, 2 * HEAD_DIM), BF16),
        ],
        name="attention",
        compiler_params=_params("parallel", "arbitrary"),
    )(*args)


def _ret_kernel(lg_ref, q_ref, k_ref, v_ref, g_ref, *rest, seq, ctx_len, chunk, has_ctx):
    if has_ctx:
        kc_ref, vc_ref, o_ref, a_ref, s_ref = rest
    else:
        o_ref, a_ref, s_ref = rest
    hd = pl.program_id(1)
    lgf, lgb = lg_ref[0, hd], lg_ref[1, hd]
    c, dk = chunk, RET_DK
    n = seq // c
    scale = dk ** -0.5
    a = lax.broadcasted_iota(jnp.int32, (c, 1), 0).astype(F32)
    wq_f, wq_b = jnp.exp(lgf * (a + 1.0)), jnp.exp(lgb * (c - a))
    wk_f, wk_b = jnp.exp(lgf * (c - 1.0 - a)) * scale, jnp.exp(lgb * a) * scale
    d = (lax.broadcasted_iota(jnp.int32, (c, c), 0) - lax.broadcasted_iota(jnp.int32, (c, c), 1)).astype(F32)
    dmask = jnp.exp(jnp.where(d >= 0.0, lgf * d, -lgb * d)) * jnp.where(d == 0.0, 2.0 * scale, scale)
    zero = jnp.zeros((1, dk), F32)
    gf, gb = jnp.exp(zero + lgf * c), jnp.exp(zero + lgb * c)

    def kv_outer(kr, vr, j):
        k = kr[j * c:(j + 1) * c, :]
        kk = jnp.concatenate([k * wk_f, k * wk_b], axis=1)
        return _dot(kk.T.astype(BF16), vr[j * c:(j + 1) * c, :])

    sf = jnp.zeros((dk, dk), F32)
    sb = jnp.zeros((dk, dk), F32)
    if has_ctx:
        outer = [kv_outer(kc_ref, vc_ref, j) for j in range(ctx_len // c)]
        for o in outer:
            sf = gf * sf + o[:dk]
        for o in reversed(outer):
            sb = gb * sb + o[dk:]
    for i in range(n):
        a_ref[i] = kv_outer(k_ref, v_ref, i)
    for i in range(n):
        s_ref[i, :dk, :] = sf.astype(BF16)
        sf = gf * sf + a_ref[i, :dk, :]
    for i in reversed(range(n)):
        s_ref[i, dk:, :] = sb.astype(BF16)
        sb = gb * sb + a_ref[i, dk:, :]
    for i in range(n):
        rows = slice(i * c, (i + 1) * c)
        q = q_ref[rows, :]
        s = _dot_nt(q, k_ref[rows, :])
        o = _dot((s * dmask).astype(BF16), v_ref[rows, :])
        qq = jnp.concatenate([q * wq_f, q * wq_b], axis=1).astype(BF16)
        o = o + _dot(qq, s_ref[i])
        o_ref[rows, :] = (_silu(g_ref[rows, :].astype(F32)) * _rms(o)).astype(BF16)


def _retention(proj, log_gamma, batch, seq, proj_ctx=None, ctx_len=0):
    n = proj.shape[0]
    has_ctx = proj_ctx is not None
    chunk = 256 if seq % 256 == 0 and ctx_len % 256 == 0 else 128
    in_specs = [
        pl.BlockSpec(memory_space=pltpu.SMEM),
        pl.BlockSpec((seq, RET_DK), lambda b, h: (b, COL_RQ + h)),
        pl.BlockSpec((seq, RET_DK), lambda b, h: (b, COL_RK + h)),
        pl.BlockSpec((seq, RET_DK), lambda b, h: (b, COL_RV + h)),
        pl.BlockSpec((seq, RET_DK), lambda b, h: (b, COL_RG + h)),
    ]
    args = [log_gamma, proj, proj, proj, proj]
    if has_ctx:
        in_specs += [
            pl.BlockSpec((ctx_len, RET_DK), lambda b, h: (b, COL_RK + h)),
            pl.BlockSpec((ctx_len, RET_DK), lambda b, h: (b, COL_RV + h)),
        ]
        args += [proj_ctx, proj_ctx]
    return pl.pallas_call(
        functools.partial(_ret_kernel, seq=seq, ctx_len=ctx_len, chunk=chunk, has_ctx=has_ctx),
        grid=(batch, RET_HEADS),
        in_specs=in_specs,
        out_specs=pl.BlockSpec((seq, RET_DK), lambda b, h: (b, h)),
        out_shape=jax.ShapeDtypeStruct((n, RET_WIDTH), BF16),
        scratch_shapes=[
            pltpu.VMEM((seq // chunk, 2 * RET_DK, RET_DK), F32),
            pltpu.VMEM((seq // chunk, 2 * RET_DK, RET_DK), BF16),
        ],
        name="retention",
        compiler_params=_params("parallel", "parallel"),
    )(*args)


def _dft_cos_sin(n):
    k = np.arange(n, dtype=np.int64)
    ang = 2.0 * np.pi * ((k[:, None] * k[None, :]) % n).astype(np.float64) / n
    return np.cos(ang), np.sin(ang)


def _fourier_consts(seq):
    cg, sg = _dft_cos_sin(FOURIER_GROUP_DIM)
    norm = 1.0 / np.sqrt(float(seq) * FOURIER_GROUP_DIM)
    eye = np.eye(FOURIER_GROUPS)
    chan = np.concatenate([np.kron(eye, cg), np.kron(eye, -sg)], axis=1) * norm
    cs, ss = _dft_cos_sin(seq)
    as_bf16 = lambda a: jnp.asarray(a.astype(np.float32)).astype(BF16)
    return as_bf16(chan), as_bf16(cs), as_bf16(ss)


def _fchan_kernel(z_ref, m_ref, o_ref):
    o_ref[...] = _dot(z_ref[...], m_ref[...]).astype(BF16)


def _fseq_kernel(c_ref, s_ref, zc_ref, zs_ref, o_ref):
    o_ref[...] = (_dot(c_ref[...], zc_ref[...]) + _dot(s_ref[...], zs_ref[...])).astype(BF16)


def _fourier(proj, batch, seq, consts):
    n = proj.shape[0]
    chan, cs, ss = consts
    fw = FOURIER_WIDTH
    tm = _tile(n, (1024, 512, 256, 128))
    zcs = pl.pallas_call(
        _fchan_kernel,
        grid=(n // tm,),
        in_specs=[
            pl.BlockSpec((tm, fw), lambda i: (i, COL_FZ * 128 // fw)),
            pl.BlockSpec((fw, 2 * fw), lambda i: (0, 0)),
        ],
        out_specs=pl.BlockSpec((tm, 2 * fw), lambda i: (i, 0)),
        out_shape=jax.ShapeDtypeStruct((n, 2 * fw), BF16),
        name="fourier_chan",
        compiler_params=_params("parallel"),
    )(proj, chan)
    tk = _tile(seq, (512, 256, 128))
    nk = seq // tk
    return pl.pallas_call(
        _fseq_kernel,
        grid=(nk, batch),
        in_specs=[
            pl.BlockSpec((tk, seq), lambda k, b: (k, 0)),
            pl.BlockSpec((tk, seq), lambda k, b: (k, 0)),
            pl.BlockSpec((seq, fw), lambda k, b: (b, 0)),
            pl.BlockSpec((seq, fw), lambda k, b: (b, 1)),
        ],
        out_specs=pl.BlockSpec((tk, fw), lambda k, b: (b * nk + k, 0)),
        out_shape=jax.ShapeDtypeStruct((n, fw), BF16),
        name="fourier_seq",
        compiler_params=_params("parallel", "parallel"),
    )(cs, ss, zcs, zcs)


MERGE_TILE = 512


def _gate_kernel(u_ref, yf_ref, ya_ref, yr_ref, wf_ref, wa_ref, wr_ref,
                 gf_ref, ga_ref, gr_ref, bf_ref, ba_ref, br_ref, o_ref):
    u = u_ref[...]
    m = jax.nn.sigmoid(_dot(u, gf_ref[...]) + bf_ref[...]) * _dot(yf_ref[...], wf_ref[...])
    m = m + jax.nn.sigmoid(_dot(u, ga_ref[...]) + ba_ref[...]) * _dot(ya_ref[...], wa_ref[...])
    m = m + jax.nn.sigmoid(_dot(u, gr_ref[...]) + br_ref[...]) * _dot(yr_ref[...], wr_ref[...])
    o_ref[...] = m.astype(BF16)


def _out_kernel(h_ref, gt_ref, m_ref, wo_ref, o_ref, *, tn):
    m = m_ref[...]
    for c in range(0, o_ref.shape[1], tn):
        cols = slice(c, c + tn)
        o_ref[:, cols] = h_ref[:, cols] + gt_ref[:, cols] * _dot(m, wo_ref[:, cols])


def _merge(h, mod, row_of_tile, tm, u, y_f, y_a, y_r, w_bf, w_ba, w_br, w_mg, b_mg, w_out, layer):
    n, d = h.shape
    tn = _tile(d, (MERGE_TILE, 256, 128))
    nn = d // tn
    tg = _tile(n, (1024, 512, 256, 128))
    wspec = lambda rows: pl.BlockSpec((None, rows, tn), lambda i, j: (layer, 0, j))
    gspec = lambda k: pl.BlockSpec((None, d, tn), lambda i, j: (layer, 0, k * nn + j))
    bspec = lambda k: pl.BlockSpec((None, 1, tn), lambda i, j: (layer, 0, k * nn + j))
    yspec = lambda w: pl.BlockSpec((tg, w), lambda i, j: (i, 0))
    m = pl.pallas_call(
        _gate_kernel,
        grid=(n // tg, nn),
        in_specs=[
            yspec(d), yspec(FOURIER_WIDTH), yspec(ATTN_WIDTH), yspec(RET_WIDTH),
            wspec(FOURIER_WIDTH), wspec(ATTN_WIDTH), wspec(RET_WIDTH),
            gspec(0), gspec(1), gspec(2),
            bspec(0), bspec(1), bspec(2),
        ],
        out_specs=pl.BlockSpec((tg, tn), lambda i, j: (i, j)),
        out_shape=jax.ShapeDtypeStruct((n, d), BF16),
        name="merge_gate",
        compiler_params=_params("parallel", "parallel"),
    )(u, y_f, y_a, y_r, w_bf, w_ba, w_br, w_mg, w_mg, w_mg, b_mg, b_mg, b_mg)
    return pl.pallas_call(
        functools.partial(_out_kernel, tn=tn),
        grid=(n // tm,),
        in_specs=[
            pl.BlockSpec((tm, d), lambda i: (i, 0)),
            _mod_spec(d, row_of_tile, 5),
            pl.BlockSpec((tm, d), lambda i: (i, 0)),
            pl.BlockSpec((None, d, d), lambda i: (layer, 0, 0)),
        ],
        out_specs=pl.BlockSpec((tm, d), lambda i: (i, 0)),
        out_shape=jax.ShapeDtypeStruct((n, d), F32),
        name="merge_out",
        compiler_params=_params("parallel"),
    )(h, mod, m, w_out)


def kernel(x, c, ctx, c_ctx, w_ada, b_ada, ffn1_norm, ffn1_w_gate, ffn1_w_up, ffn1_w_down, mix_norm, w_in, q_norm, k_norm, ret_decay, w_branch_fourier, w_branch_attn, w_branch_ret, w_merge_gate, b_merge_gate, w_out, ffn2_norm, ffn2_w_gate, ffn2_w_up, ffn2_w_down, final_norm):
    batch, seq, d = x.shape
    ctx_len = ctx.shape[1]
    depth = w_ada.shape[0]
    vec = lambda g: g.reshape(depth, 1, g.shape[-1])

    rows = -(-(batch + 1) // 16) * 16
    c_all = jnp.concatenate([c, c_ctx[None], jnp.zeros((rows - batch - 1, d), F32)], axis=0)
    mod = _adaln(c_all, w_ada, b_ada).reshape(depth * rows, 1, N_MOD * d)

    tm = _tile(seq, (512, 256, 128))
    tiles_per_sample = seq // tm
    tmc = _tile(batch * ctx_len, (512, 256, 128))

    bf = lambda w: w.astype(BF16)
    ffn1 = (vec(ffn1_norm), bf(ffn1_w_gate), bf(ffn1_w_up), bf(ffn1_w_down))
    ffn2 = (vec(ffn2_norm), bf(ffn2_w_gate), bf(ffn2_w_up), bf(ffn2_w_down))
    mix_gain, w_in_t = vec(mix_norm), bf(w_in)
    qn, kn = vec(q_norm), vec(k_norm)
    w_bf, w_ba, w_br = bf(w_branch_fourier), bf(w_branch_attn), bf(w_branch_ret)
    w_mg, b_mg, w_o = bf(w_merge_gate), vec(b_merge_gate), bf(w_out)
    log_gamma = -jnp.exp(ret_decay.astype(F32))
    tables = _rope_tables(seq)
    four_lat = _fourier_consts(seq)
    four_ctx = _fourier_consts(ctx_len)

    h = x.reshape(batch * seq, d)
    hc = ctx.reshape(batch * ctx_len, d)
    for l in range(depth):
        need_ctx = l < depth - 1
        lat_row = lambda i, l=l: l * rows + i // tiles_per_sample
        ctx_row = lambda i, l=l: l * rows + batch

        h, u = _ffn(h, mod, lat_row, tm, 0, *ffn1, l, emit_gain=mix_gain)
        hc, uc = _ffn(hc, mod, ctx_row, tmc, 0, *ffn1, l, emit_gain=mix_gain)

        proj = _inproj(u, w_in_t, l)
        proj_c = _inproj(uc, w_in_t, l)

        y_a = _attention(proj, proj_c, qn, kn, l, batch, seq, ctx_len, proj, tables)
        y_r = _retention(proj, log_gamma[l], batch, seq, proj_c, ctx_len)
        y_f = _fourier(proj, batch, seq, four_lat)
        h = _merge(h, mod, lat_row, tm, u, y_f, y_a, y_r, w_bf, w_ba, w_br, w_mg, b_mg, w_o, l)
        last = final_norm.reshape(1, d) if l == depth - 1 else None
        h = _ffn(h, mod, lat_row, tm, 6, *ffn2, l, final_gain=last)

        if need_ctx:
            yc_a = _attention(proj_c, proj_c, qn, kn, l, batch, ctx_len, ctx_len)
            yc_r = _retention(proj_c, log_gamma[l], batch, ctx_len)
            yc_f = _fourier(proj_c, batch, ctx_len, four_ctx)
            hc = _merge(hc, mod, ctx_row, tmc, uc, yc_f, yc_a, yc_r, w_bf, w_ba, w_br, w_mg, b_mg, w_o, l)
            hc = _ffn(hc, mod, ctx_row, tmc, 6, *ffn2, l)
    return h.reshape(batch, seq, d)
```

```python
import functools

import numpy as np
import jax
import jax.numpy as jnp
from jax import lax
from jax.experimental import pallas as pl
from jax.experimental.pallas import tpu as pltpu

F32 = jnp.float32
BF16 = jnp.bfloat16

EPS = 1e-6
N_MOD = 9
GRID_W = 64
HEAD_DIM = 128
HALF_ROT = HEAD_DIM // 2
ATTN_HEADS = 8
ATTN_KV_HEADS = 2
ATTN_GROUP = ATTN_HEADS // ATTN_KV_HEADS
ATTN_WIDTH = ATTN_HEADS * HEAD_DIM
KV_WIDTH = ATTN_KV_HEADS * HEAD_DIM
RET_HEADS = 4
RET_DK = 128
RET_WIDTH = RET_HEADS * RET_DK
FOURIER_GROUPS = 4
FOURIER_GROUP_DIM = 128
FOURIER_WIDTH = FOURIER_GROUPS * FOURIER_GROUP_DIM
IN_WIDTH = ATTN_WIDTH + 2 * KV_WIDTH + 4 * RET_WIDTH + FOURIER_WIDTH
ROPE_THETA = 10000.0

COL_AK = ATTN_WIDTH // 128
COL_AV = COL_AK + KV_WIDTH // 128
COL_RQ = COL_AV + KV_WIDTH // 128
COL_RK = COL_RQ + RET_WIDTH // 128
COL_RV = COL_RK + RET_WIDTH // 128
COL_RG = COL_RV + RET_WIDTH // 128
COL_FZ = COL_RG + RET_WIDTH // 128

IN_TILE = ATTN_WIDTH

ATTN_Q_SCALE = HEAD_DIM ** -0.5 * float(np.log2(np.e))

V7X_VMEM_LIMIT_BYTES = 56 * 1024 * 1024


def _params(*sem):
    return pltpu.CompilerParams(dimension_semantics=sem, vmem_limit_bytes=V7X_VMEM_LIMIT_BYTES)


def _tile(n, prefs):
    for p in prefs:
        if n % p == 0:
            return p
    return n


def _dot(a, b):
    return jnp.dot(a, b, preferred_element_type=F32)


def _dot_nt(a, b):
    return lax.dot_general(a, b, (((1,), (1,)), ((), ())), preferred_element_type=F32)


def _rms(x):
    return x * lax.rsqrt(jnp.mean(x * x, axis=-1, keepdims=True) + EPS)


def _silu(x):
    return x * jax.nn.sigmoid(x)


def _adaln_kernel(c_ref, w_ref, b_ref, o_ref):
    a = _silu(c_ref[...]).astype(BF16)
    o_ref[...] = _dot(a, w_ref[...].astype(BF16)) + b_ref[...]


def _adaln(c_all, w_ada, b_ada):
    depth, d, nd = w_ada.shape
    r = c_all.shape[0]
    tn = _tile(nd, (1024, 512, 256, 128))
    return pl.pallas_call(
        _adaln_kernel,
        grid=(depth, nd // tn),
        in_specs=[
            pl.BlockSpec((r, d), lambda l, j: (0, 0)),
            pl.BlockSpec((None, d, tn), lambda l, j: (l, 0, j)),
            pl.BlockSpec((None, 1, tn), lambda l, j: (l, 0, j)),
        ],
        out_specs=pl.BlockSpec((None, r, tn), lambda l, j: (l, 0, j)),
        out_shape=jax.ShapeDtypeStruct((depth, r, nd), F32),
        name="adaln",
        compiler_params=_params("parallel", "parallel"),
    )(c_all, w_ada, b_ada.reshape(depth, 1, nd))


def _mod_spec(d, row_of_tile, k):
    return pl.BlockSpec((None, 1, d), lambda i, *_: (row_of_tile(i), 0, k))


FFN_TILE = 512
ROW_CHUNK = 16


def _ffn_kernel(h_ref, sh_ref, sc_ref, gt_ref, gain_ref, wg_ref, wu_ref, wd_ref, *rest, nf, final, emit):
    if final:
        fg_ref, o_ref, xn_ref, acc_ref = rest
    elif emit:
        sh2_ref, sc2_ref, gain2_ref, o_ref, u_ref, xn_ref, acc_ref = rest
    else:
        o_ref, xn_ref, acc_ref = rest
    f = pl.program_id(1)
    tm = h_ref.shape[0]

    @pl.when(f == 0)
    def _():
        gs, sh = gain_ref[...] * (1.0 + sc_ref[...]), sh_ref[...]
        for r in range(0, tm, ROW_CHUNK):
            rows = slice(r, r + ROW_CHUNK)
            xn_ref[rows, :] = (_rms(h_ref[rows, :]) * gs + sh).astype(BF16)

    @pl.when((pl.program_id(0) == 0) & (f == 0))
    def _():
        acc_ref[...] = jnp.zeros_like(acc_ref)

    xn = xn_ref[...]
    g = _dot(xn, wg_ref[...])
    u = _dot(xn, wu_ref[...])
    acc_ref[...] = jnp.where(f == 0, 0.0, acc_ref[...]) + _dot((_silu(g) * u).astype(BF16), wd_ref[...])

    @pl.when(f == nf - 1)
    def _():
        half_gate = 0.5 * gt_ref[...]
        if final:
            fg = fg_ref[...]
        if emit:
            gs2, sh2 = gain2_ref[...] * (1.0 + sc2_ref[...]), sh2_ref[...]
        for r in range(0, tm, ROW_CHUNK):
            rows = slice(r, r + ROW_CHUNK)
            y = h_ref[rows, :] + half_gate * acc_ref[rows, :]
            if final:
                y = _rms(y) * fg
            o_ref[rows, :] = y
            if emit:
                u_ref[rows, :] = (_rms(y) * gs2 + sh2).astype(BF16)


def _ffn(h, mod, row_of_tile, tm, k0, gain, wg, wu, wd, layer, final_gain=None, emit_gain=None):
    n, d = h.shape
    tf = _tile(wg.shape[-1], (FFN_TILE, 256, 128))
    nf = wg.shape[-1] // tf
    final, emit = final_gain is not None, emit_gain is not None
    vec_spec = pl.BlockSpec((None, 1, d), lambda i, f: (layer, 0, 0))
    in_specs = [
        pl.BlockSpec((tm, d), lambda i, f: (i, 0)),
        _mod_spec(d, row_of_tile, k0),
        _mod_spec(d, row_of_tile, k0 + 1),
        _mod_spec(d, row_of_tile, k0 + 2),
        vec_spec,
        pl.BlockSpec((None, d, tf), lambda i, f: (layer, 0, f)),
        pl.BlockSpec((None, d, tf), lambda i, f: (layer, 0, f)),
        pl.BlockSpec((None, tf, d), lambda i, f: (layer, f, 0)),
    ]
    args = [h, mod, mod, mod, gain, wg, wu, wd]
    row_spec = pl.BlockSpec((tm, d), lambda i, f: (i, 0))
    out_specs, out_shape = row_spec, jax.ShapeDtypeStruct((n, d), F32)
    if final:
        in_specs.append(pl.BlockSpec((1, d), lambda i, f: (0, 0)))
        args.append(final_gain)
    elif emit:
        in_specs += [_mod_spec(d, row_of_tile, 3), _mod_spec(d, row_of_tile, 4), vec_spec]
        args += [mod, mod, emit_gain]
        out_specs = [row_spec, row_spec]
        out_shape = [out_shape, jax.ShapeDtypeStruct((n, d), BF16)]
    return pl.pallas_call(
        functools.partial(_ffn_kernel, nf=nf, final=final, emit=emit),
        grid=(n // tm, nf),
        in_specs=in_specs,
        out_specs=out_specs,
        out_shape=out_shape,
        scratch_shapes=[pltpu.VMEM((tm, d), BF16), pltpu.VMEM((tm, d), F32)],
        name="ffn",
        compiler_params=_params("parallel", "arbitrary"),
    )(*args)


def _inproj_kernel(u_ref, w_ref, o_ref):
    o_ref[...] = _dot(u_ref[...], w_ref[...]).astype(BF16)


def _inproj(u, w_in, layer):
    n, d = u.shape
    tn = IN_TILE
    assert w_in.shape[-1] == IN_WIDTH and IN_WIDTH % tn == 0
    tm = _tile(n, (1024, 512, 256, 128))
    return pl.pallas_call(
        _inproj_kernel,
        grid=(n // tm, IN_WIDTH // tn),
        in_specs=[
            pl.BlockSpec((tm, d), lambda i, j: (i, 0)),
            pl.BlockSpec((None, d, tn), lambda i, j: (layer, 0, j)),
        ],
        out_specs=pl.BlockSpec((tm, tn), lambda i, j: (i, j)),
        out_shape=jax.ShapeDtypeStruct((n, IN_WIDTH), BF16),
        name="inproj",
        compiler_params=_params("parallel", "parallel"),
    )(u, w_in)


def _rope_tables(seq):
    t = jnp.arange(seq)
    row = (t // GRID_W).astype(F32)
    col = (t % GRID_W).astype(F32)
    inv = ROPE_THETA ** (-jnp.arange(0, HALF_ROT, 2, dtype=F32) / HALF_ROT)
    ar, ac = row[:, None] * inv[None], col[:, None] * inv[None]
    zero = jnp.zeros_like(ar)
    cos = jnp.concatenate([jnp.cos(ar), jnp.cos(ar), jnp.cos(ac), jnp.cos(ac)], axis=-1)
    sin_lo = jnp.concatenate([-jnp.sin(ar), zero, -jnp.sin(ac), zero], axis=-1)
    sin_hi = jnp.concatenate([zero, jnp.sin(ar), zero, jnp.sin(ac)], axis=-1)
    return cos, sin_lo, sin_hi


def _rope(x, cos, sin_lo, sin_hi):
    return x * cos + pltpu.roll(x, HEAD_DIM - HALF_ROT // 2, 1) * sin_lo + pltpu.roll(x, HALF_ROT // 2, 1) * sin_hi


def _attn_kernel(q_ref, kc_ref, vc_ref, qn_ref, kn_ref, *rest, has_lat):
    if has_lat:
        kl_ref, vl_ref, qcos_ref, qlo_ref, qhi_ref, kcos_ref, klo_ref, khi_ref, o_ref, k_ref, v_ref = rest
    else:
        o_ref, k_ref, v_ref = rest
    nc = kc_ref.shape[0]

    @pl.when(pl.program_id(1) == 0)
    def _():
        kn = kn_ref[...]
        for kv in range(ATTN_KV_HEADS):
            sl = slice(kv * HEAD_DIM, (kv + 1) * HEAD_DIM)
            k_ref[kv, :nc, :] = (_rms(kc_ref[:, sl].astype(F32)) * kn).astype(BF16)
            v_ref[kv, :nc, :HEAD_DIM] = vc_ref[:, sl]
            if has_lat:
                kl = _rms(kl_ref[:, sl].astype(F32)) * kn
                k_ref[kv, nc:, :] = _rope(kl, kcos_ref[...], klo_ref[...], khi_ref[...]).astype(BF16)
                v_ref[kv, nc:, :HEAD_DIM] = vl_ref[:, sl]
            v_ref[kv, :, HEAD_DIM:] = jnp.ones((v_ref.shape[1], HEAD_DIM), BF16)

    qn = qn_ref[...]
    for hd in range(ATTN_HEADS):
        sl = slice(hd * HEAD_DIM, (hd + 1) * HEAD_DIM)
        q = _rms(q_ref[:, sl].astype(F32)) * qn
        if has_lat:
            q = _rope(q, qcos_ref[...], qlo_ref[...], qhi_ref[...])
        s = _dot_nt((q * ATTN_Q_SCALE).astype(BF16), k_ref[hd // ATTN_GROUP])
        p = jnp.exp2(s - jnp.max(s, axis=-1, keepdims=True)).astype(BF16)
        ov = _dot(p, v_ref[hd // ATTN_GROUP])
        o_ref[:, sl] = (ov[:, :HEAD_DIM] / ov[:, HEAD_DIM:]).astype(BF16)


def _attention(proj_q, proj_c, q_norm, k_norm, layer, batch, seq, ctx_len, proj_kv=None, tables=None):
    n = proj_q.shape[0]
    tq = _tile(seq, (256, 128))
    nq = seq // tq
    has_lat = proj_kv is not None
    n_keys = ctx_len + (seq if has_lat else 0)
    col_k, col_v = COL_AK * 128 // KV_WIDTH, COL_AV * 128 // KV_WIDTH
    norm_spec = pl.BlockSpec((None, 1, HEAD_DIM), lambda b, i: (layer, 0, 0))
    in_specs = [
        pl.BlockSpec((tq, ATTN_WIDTH), lambda b, i: (b * nq + i, 0)),
        pl.BlockSpec((ctx_len, KV_WIDTH), lambda b, i: (b, col_k)),
        pl.BlockSpec((ctx_len, KV_WIDTH), lambda b, i: (b, col_v)),
        norm_spec,
        norm_spec,
    ]
    args = [proj_q, proj_c, proj_c, q_norm, k_norm]
    if has_lat:
        in_specs += [
            pl.BlockSpec((seq, KV_WIDTH), lambda b, i: (b, col_k)),
            pl.BlockSpec((seq, KV_WIDTH), lambda b, i: (b, col_v)),
        ]
        in_specs += [pl.BlockSpec((tq, HEAD_DIM), lambda b, i: (i, 0))] * 3
        in_specs += [pl.BlockSpec((seq, HEAD_DIM), lambda b, i: (0, 0))] * 3
        args += [proj_kv, proj_kv, *tables, *tables]
    return pl.pallas_call(
        functools.partial(_attn_kernel, has_lat=has_lat),
        grid=(batch, nq),
        in_specs=in_specs,
        out_specs=pl.BlockSpec((tq, ATTN_WIDTH), lambda b, i: (b * nq + i, 0)),
        out_shape=jax.ShapeDtypeStruct((n, ATTN_WIDTH), BF16),
        scratch_shapes=[
            pltpu.VMEM((ATTN_KV_HEADS, n_keys, HEAD_DIM), BF16),
            pltpu.VMEM((ATTN_KV_HEADS, n_keys, 2 * HEAD_DIM), BF16),
        ],
        name="attention",
        compiler_params=_params("parallel", "arbitrary"),
    )(*args)


def _ret_kernel(lg_ref, q_ref, k_ref, v_ref, g_ref, *rest, seq, ctx_len, chunk, has_ctx):
    if has_ctx:
        kc_ref, vc_ref, o_ref, a_ref, s_ref = rest
    else:
        o_ref, a_ref, s_ref = rest
    hd = pl.program_id(1)
    lgf, lgb = lg_ref[0, hd], lg_ref[1, hd]
    c, dk = chunk, RET_DK
    n = seq // c
    scale = dk ** -0.5
    a = lax.broadcasted_iota(jnp.int32, (c, 1), 0).astype(F32)
    wq_f, wq_b = jnp.exp(lgf * (a + 1.0)), jnp.exp(lgb * (c - a))
    wk_f, wk_b = jnp.exp(lgf * (c - 1.0 - a)) * scale, jnp.exp(lgb * a) * scale
    d = (lax.broadcasted_iota(jnp.int32, (c, c), 0) - lax.broadcasted_iota(jnp.int32, (c, c), 1)).astype(F32)
    dmask = jnp.exp(jnp.where(d >= 0.0, lgf * d, -lgb * d)) * jnp.where(d == 0.0, 2.0 * scale, scale)
    zero = jnp.zeros((1, dk), F32)
    gf, gb = jnp.exp(zero + lgf * c), jnp.exp(zero + lgb * c)

    def kv_outer(kr, vr, j):
        k = kr[j * c:(j + 1) * c, :]
        kk = jnp.concatenate([k * wk_f, k * wk_b], axis=1)
        return _dot(kk.T.astype(BF16), vr[j * c:(j + 1) * c, :])

    sf = jnp.zeros((dk, dk), F32)
    sb = jnp.zeros((dk, dk), F32)
    if has_ctx:
        outer = [kv_outer(kc_ref, vc_ref, j) for j in range(ctx_len // c)]
        for o in outer:
            sf = gf * sf + o[:dk]
        for o in reversed(outer):
            sb = gb * sb + o[dk:]
    for i in range(n):
        a_ref[i] = kv_outer(k_ref, v_ref, i)
    for i in range(n):
        s_ref[i, :dk, :] = sf.astype(BF16)
        sf = gf * sf + a_ref[i, :dk, :]
    for i in reversed(range(n)):
        s_ref[i, dk:, :] = sb.astype(BF16)
        sb = gb * sb + a_ref[i, dk:, :]
    for i in range(n):
        rows = slice(i * c, (i + 1) * c)
        q = q_ref[rows, :]
        s = _dot_nt(q, k_ref[rows, :])
        o = _dot((s * dmask).astype(BF16), v_ref[rows, :])
        qq = jnp.concatenate([q * wq_f, q * wq_b], axis=1).astype(BF16)
        o = o + _dot(qq, s_ref[i])
        o_ref[rows, :] = (_silu(g_ref[rows, :].astype(F32)) * _rms(o)).astype(BF16)


def _retention(proj, log_gamma, batch, seq, proj_ctx=None, ctx_len=0):
    n = proj.shape[0]
    has_ctx = proj_ctx is not None
    chunk = 256 if seq % 256 == 0 and ctx_len % 256 == 0 else 128
    in_specs = [
        pl.BlockSpec(memory_space=pltpu.SMEM),
        pl.BlockSpec((seq, RET_DK), lambda b, h: (b, COL_RQ + h)),
        pl.BlockSpec((seq, RET_DK), lambda b, h: (b, COL_RK + h)),
        pl.BlockSpec((seq, RET_DK), lambda b, h: (b, COL_RV + h)),
        pl.BlockSpec((seq, RET_DK), lambda b, h: (b, COL_RG + h)),
    ]
    args = [log_gamma, proj, proj, proj, proj]
    if has_ctx:
        in_specs += [
            pl.BlockSpec((ctx_len, RET_DK), lambda b, h: (b, COL_RK + h)),
            pl.BlockSpec((ctx_len, RET_DK), lambda b, h: (b, COL_RV + h)),
        ]
        args += [proj_ctx, proj_ctx]
    return pl.pallas_call(
        functools.partial(_ret_kernel, seq=seq, ctx_len=ctx_len, chunk=chunk, has_ctx=has_ctx),
        grid=(batch, RET_HEADS),
        in_specs=in_specs,
        out_specs=pl.BlockSpec((seq, RET_DK), lambda b, h: (b, h)),
        out_shape=jax.ShapeDtypeStruct((n, RET_WIDTH), BF16),
        scratch_shapes=[
            pltpu.VMEM((seq // chunk, 2 * RET_DK, RET_DK), F32),
            pltpu.VMEM((seq // chunk, 2 * RET_DK, RET_DK), BF16),
        ],
        name="retention",
        compiler_params=_params("parallel", "parallel"),
    )(*args)


def _dft_cos_sin(n):
    k = np.arange(n, dtype=np.int64)
    ang = 2.0 * np.pi * ((k[:, None] * k[None, :]) % n).astype(np.float64) / n
    return np.cos(ang), np.sin(ang)


def _fourier_consts(seq):
    cg, sg = _dft_cos_sin(FOURIER_GROUP_DIM)
    norm = 1.0 / np.sqrt(float(seq) * FOURIER_GROUP_DIM)
    eye = np.eye(FOURIER_GROUPS)
    chan = np.concatenate([np.kron(eye, cg), np.kron(eye, -sg)], axis=1) * norm
    cs, ss = _dft_cos_sin(seq)
    as_bf16 = lambda a: jnp.asarray(a.astype(np.float32)).astype(BF16)
    return as_bf16(chan), as_bf16(cs), as_bf16(ss)


def _fchan_kernel(z_ref, m_ref, o_ref):
    o_ref[...] = _dot(z_ref[...], m_ref[...]).astype(BF16)


def _fseq_kernel(c_ref, s_ref, zc_ref, zs_ref, o_ref):
    o_ref[...] = (_dot(c_ref[...], zc_ref[...]) + _dot(s_ref[...], zs_ref[...])).astype(BF16)


def _fourier(proj, batch, seq, consts):
    n = proj.shape[0]
    chan, cs, ss = consts
    fw = FOURIER_WIDTH
    tm = _tile(n, (1024, 512, 256, 128))
    zcs = pl.pallas_call(
        _fchan_kernel,
        grid=(n // tm,),
        in_specs=[
            pl.BlockSpec((tm, fw), lambda i: (i, COL_FZ * 128 // fw)),
            pl.BlockSpec((fw, 2 * fw), lambda i: (0, 0)),
        ],
        out_specs=pl.BlockSpec((tm, 2 * fw), lambda i: (i, 0)),
        out_shape=jax.ShapeDtypeStruct((n, 2 * fw), BF16),
        name="fourier_chan",
        compiler_params=_params("parallel"),
    )(proj, chan)
    tk = _tile(seq, (512, 256, 128))
    nk = seq // tk
    return pl.pallas_call(
        _fseq_kernel,
        grid=(nk, batch),
        in_specs=[
            pl.BlockSpec((tk, seq), lambda k, b: (k, 0)),
            pl.BlockSpec((tk, seq), lambda k, b: (k, 0)),
            pl.BlockSpec((seq, fw), lambda k, b: (b, 0)),
            pl.BlockSpec((seq, fw), lambda k, b: (b, 1)),
        ],
        out_specs=pl.BlockSpec((tk, fw), lambda k, b: (b * nk + k, 0)),
        out_shape=jax.ShapeDtypeStruct((n, fw), BF16),
        name="fourier_seq",
        compiler_params=_params("parallel", "parallel"),
    )(cs, ss, zcs, zcs)


MERGE_TILE = 512


def _gate_kernel(u_ref, yf_ref, ya_ref, yr_ref, wf_ref, wa_ref, wr_ref,
                 gf_ref, ga_ref, gr_ref, bf_ref, ba_ref, br_ref, o_ref):
    u = u_ref[...]
    m = jax.nn.sigmoid(_dot(u, gf_ref[...]) + bf_ref[...]) * _dot(yf_ref[...], wf_ref[...])
    m = m + jax.nn.sigmoid(_dot(u, ga_ref[...]) + ba_ref[...]) * _dot(ya_ref[...], wa_ref[...])
    m = m + jax.nn.sigmoid(_dot(u, gr_ref[...]) + br_ref[...]) * _dot(yr_ref[...], wr_ref[...])
    o_ref[...] = m.astype(BF16)


def _out_kernel(h_ref, gt_ref, m_ref, wo_ref, o_ref, *, tn):
    m = m_ref[...]
    for c in range(0, o_ref.shape[1], tn):
        cols = slice(c, c + tn)
        o_ref[:, cols] = h_ref[:, cols] + gt_ref[:, cols] * _dot(m, wo_ref[:, cols])


def _merge(h, mod, row_of_tile, tm, u, y_f, y_a, y_r, w_bf, w_ba, w_br, w_mg, b_mg, w_out, layer):
    n, d = h.shape
    tn = _tile(d, (MERGE_TILE, 256, 128))
    nn = d // tn
    tg = _tile(n, (1024, 512, 256, 128))
    wspec = lambda rows: pl.BlockSpec((None, rows, tn), lambda i, j: (layer, 0, j))
    gspec = lambda k: pl.BlockSpec((None, d, tn), lambda i, j: (layer, 0, k * nn + j))
    bspec = lambda k: pl.BlockSpec((None, 1, tn), lambda i, j: (layer, 0, k * nn + j))
    yspec = lambda w: pl.BlockSpec((tg, w), lambda i, j: (i, 0))
    m = pl.pallas_call(
        _gate_kernel,
        grid=(n // tg, nn),
        in_specs=[
            yspec(d), yspec(FOURIER_WIDTH), yspec(ATTN_WIDTH), yspec(RET_WIDTH),
            wspec(FOURIER_WIDTH), wspec(ATTN_WIDTH), wspec(RET_WIDTH),
            gspec(0), gspec(1), gspec(2),
            bspec(0), bspec(1), bspec(2),
        ],
        out_specs=pl.BlockSpec((tg, tn), lambda i, j: (i, j)),
        out_shape=jax.ShapeDtypeStruct((n, d), BF16),
        name="merge_gate",
        compiler_params=_params("parallel", "parallel"),
    )(u, y_f, y_a, y_r, w_bf, w_ba, w_br, w_mg, w_mg, w_mg, b_mg, b_mg, b_mg)
    return pl.pallas_call(
        functools.partial(_out_kernel, tn=tn),
        grid=(n // tm,),
        in_specs=[
            pl.BlockSpec((tm, d), lambda i: (i, 0)),
            _mod_spec(d, row_of_tile, 5),
            pl.BlockSpec((tm, d), lambda i: (i, 0)),
            pl.BlockSpec((None, d, d), lambda i: (layer, 0, 0)),
        ],
        out_specs=pl.BlockSpec((tm, d), lambda i: (i, 0)),
        out_shape=jax.ShapeDtypeStruct((n, d), F32),
        name="merge_out",
        compiler_params=_params("parallel"),
    )(h, mod, m, w_out)


def kernel(x, c, ctx, c_ctx, w_ada, b_ada, ffn1_norm, ffn1_w_gate, ffn1_w_up, ffn1_w_down, mix_norm, w_in, q_norm, k_norm, ret_decay, w_branch_fourier, w_branch_attn, w_branch_ret, w_merge_gate, b_merge_gate, w_out, ffn2_norm, ffn2_w_gate, ffn2_w_up, ffn2_w_down, final_norm):
    batch, seq, d = x.shape
    ctx_len = ctx.shape[1]
    depth = w_ada.shape[0]
    vec = lambda g: g.reshape(depth, 1, g.shape[-1])

    rows = -(-(batch + 1) // 16) * 16
    c_all = jnp.concatenate([c, c_ctx[None], jnp.zeros((rows - batch - 1, d), F32)], axis=0)
    mod = _adaln(c_all, w_ada, b_ada).reshape(depth * rows, 1, N_MOD * d)

    tm = _tile(seq, (512, 256, 128))
    tiles_per_sample = seq // tm
    tmc = _tile(batch * ctx_len, (512, 256, 128))

    bf = lambda w: w.astype(BF16)
    ffn1 = (vec(ffn1_norm), bf(ffn1_w_gate), bf(ffn1_w_up), bf(ffn1_w_down))
    ffn2 = (vec(ffn2_norm), bf(ffn2_w_gate), bf(ffn2_w_up), bf(ffn2_w_down))
    mix_gain, w_in_t = vec(mix_norm), bf(w_in)
    qn, kn = vec(q_norm), vec(k_norm)
    w_bf, w_ba, w_br = bf(w_branch_fourier), bf(w_branch_attn), bf(w_branch_ret)
    w_mg, b_mg, w_o = bf(w_merge_gate), vec(b_merge_gate), bf(w_out)
    log_gamma = -jnp.exp(ret_decay.astype(F32))
    tables = _rope_tables(seq)
    four_lat = _fourier_consts(seq)
    four_ctx = _fourier_consts(ctx_len)

    h = x.reshape(batch * seq, d)
    hc = ctx.reshape(batch * ctx_len, d)
    for l in range(depth):
        need_ctx = l < depth - 1
        lat_row = lambda i, l=l: l * rows + i // tiles_per_sample
        ctx_row = lambda i, l=l: l * rows + batch

        h, u = _ffn(h, mod, lat_row, tm, 0, *ffn1, l, emit_gain=mix_gain)
        hc, uc = _ffn(hc, mod, ctx_row, tmc, 0, *ffn1, l, emit_gain=mix_gain)

        proj = _inproj(u, w_in_t, l)
        proj_c = _inproj(uc, w_in_t, l)

        y_a = _attention(proj, proj_c, qn, kn, l, batch, seq, ctx_len, proj, tables)
        y_r = _retention(proj, log_gamma[l], batch, seq, proj_c, ctx_len)
        y_f = _fourier(proj, batch, seq, four_lat)
        h = _merge(h, mod, lat_row, tm, u, y_f, y_a, y_r, w_bf, w_ba, w_br, w_mg, b_mg, w_o, l)
        last = final_norm.reshape(1, d) if l == depth - 1 else None
        h = _ffn(h, mod, lat_row, tm, 6, *ffn2, l, final_gain=last)

        if need_ctx:
            yc_a = _attention(proj_c, proj_c, qn, kn, l, batch, ctx_len, ctx_len)
            yc_r = _retention(proj_c, log_gamma[l], batch, ctx_len)
            yc_f = _fourier(proj_c, batch, ctx_len, four_ctx)
            hc = _merge(hc, mod, ctx_row, tmc, uc, yc_f, yc_a, yc_r, w_bf, w_ba, w_br, w_mg, b_mg, w_o, l)
            hc = _ffn(hc, mod, ctx_row, tmc, 6, *ffn2, l)
    return h.reshape(batch, seq, d)
```

```python
import functools

import numpy as np
import jax
import jax.numpy as jnp
from jax import lax
from jax.experimental import pallas as pl
from jax.experimental.pallas import tpu as pltpu

F32 = jnp.float32
BF16 = jnp.bfloat16

EPS = 1e-6
N_MOD = 9
GRID_W = 64
HEAD_DIM = 128
HALF_ROT = HEAD_DIM // 2
ATTN_HEADS = 8
ATTN_KV_HEADS = 2
ATTN_GROUP = ATTN_HEADS // ATTN_KV_HEADS
ATTN_WIDTH = ATTN_HEADS * HEAD_DIM
KV_WIDTH = ATTN_KV_HEADS * HEAD_DIM
RET_HEADS = 4
RET_DK = 128
RET_WIDTH = RET_HEADS * RET_DK
FOURIER_GROUPS = 4
FOURIER_GROUP_DIM = 128
FOURIER_WIDTH = FOURIER_GROUPS * FOURIER_GROUP_DIM
IN_WIDTH = ATTN_WIDTH + 2 * KV_WIDTH + 4 * RET_WIDTH + FOURIER_WIDTH
ROPE_THETA = 10000.0

COL_AK = ATTN_WIDTH // 128
COL_AV = COL_AK + KV_WIDTH // 128
COL_RQ = COL_AV + KV_WIDTH // 128
COL_RK = COL_RQ + RET_WIDTH // 128
COL_RV = COL_RK + RET_WIDTH // 128
COL_RG = COL_RV + RET_WIDTH // 128
COL_FZ = COL_RG + RET_WIDTH // 128

IN_TILE = ATTN_WIDTH

ATTN_Q_SCALE = HEAD_DIM ** -0.5 * float(np.log2(np.e))

V7X_VMEM_LIMIT_BYTES = 56 * 1024 * 1024


def _params(*sem):
    return pltpu.CompilerParams(dimension_semantics=sem, vmem_limit_bytes=V7X_VMEM_LIMIT_BYTES)


def _tile(n, prefs):
    for p in prefs:
        if n % p == 0:
            return p
    return n


def _dot(a, b):
    return jnp.dot(a, b, preferred_element_type=F32)


def _dot_nt(a, b):
    return lax.dot_general(a, b, (((1,), (1,)), ((), ())), preferred_element_type=F32)


def _rms(x):
    return x * lax.rsqrt(jnp.mean(x * x, axis=-1, keepdims=True) + EPS)


def _silu(x):
    return x * jax.nn.sigmoid(x)


def _adaln_kernel(c_ref, w_ref, b_ref, o_ref):
    a = _silu(c_ref[...]).astype(BF16)
    o_ref[...] = _dot(a, w_ref[...].astype(BF16)) + b_ref[...]


def _adaln(c_all, w_ada, b_ada):
    depth, d, nd = w_ada.shape
    r = c_all.shape[0]
    tn = _tile(nd, (1024, 512, 256, 128))
    return pl.pallas_call(
        _adaln_kernel,
        grid=(depth, nd // tn),
        in_specs=[
            pl.BlockSpec((r, d), lambda l, j: (0, 0)),
            pl.BlockSpec((None, d, tn), lambda l, j: (l, 0, j)),
            pl.BlockSpec((None, 1, tn), lambda l, j: (l, 0, j)),
        ],
        out_specs=pl.BlockSpec((None, r, tn), lambda l, j: (l, 0, j)),
        out_shape=jax.ShapeDtypeStruct((depth, r, nd), F32),
        name="adaln",
        compiler_params=_params("parallel", "parallel"),
    )(c_all, w_ada, b_ada.reshape(depth, 1, nd))


def _mod_spec(d, row_of_tile, k):
    return pl.BlockSpec((None, 1, d), lambda i, *_: (row_of_tile(i), 0, k))


FFN_TILE = 512
ACT_CHUNK = 32
ROW_CHUNK = 16


def _ffn_kernel(h_ref, sh_ref, sc_ref, gt_ref, gain_ref, wg_ref, wu_ref, wd_ref, *rest, nf, final, emit):
    if final:
        fg_ref, o_ref, xn_ref, acc_ref, a_ref = rest
    elif emit:
        sh2_ref, sc2_ref, gain2_ref, o_ref, u_ref, xn_ref, acc_ref, a_ref = rest
    else:
        o_ref, xn_ref, acc_ref, a_ref = rest
    f = pl.program_id(1)
    tm = h_ref.shape[0]

    @pl.when(f == 0)
    def _():
        gs, sh = gain_ref[...] * (1.0 + sc_ref[...]), sh_ref[...]
        for r in range(0, tm, ROW_CHUNK):
            rows = slice(r, r + ROW_CHUNK)
            xn_ref[rows, :] = (_rms(h_ref[rows, :]) * gs + sh).astype(BF16)

    @pl.when((pl.program_id(0) == 0) & (f == 0))
    def _():
        acc_ref[...] = jnp.zeros_like(acc_ref)

    xn = xn_ref[...]
    g = _dot(xn, wg_ref[...])
    u = _dot(xn, wu_ref[...])
    for r in range(0, tm, ACT_CHUNK):
        rows = slice(r, r + ACT_CHUNK)
        a_ref[rows, :] = (_silu(g[rows, :]) * u[rows, :]).astype(BF16)
    acc_ref[...] = jnp.where(f == 0, 0.0, acc_ref[...]) + _dot(a_ref[...], wd_ref[...])

    @pl.when(f == nf - 1)
    def _():
        half_gate = 0.5 * gt_ref[...]
        if final:
            fg = fg_ref[...]
        if emit:
            gs2, sh2 = gain2_ref[...] * (1.0 + sc2_ref[...]), sh2_ref[...]
        for r in range(0, tm, ROW_CHUNK):
            rows = slice(r, r + ROW_CHUNK)
            y = h_ref[rows, :] + half_gate * acc_ref[rows, :]
            if final:
                y = _rms(y) * fg
            o_ref[rows, :] = y
            if emit:
                u_ref[rows, :] = (_rms(y) * gs2 + sh2).astype(BF16)


def _ffn(h, mod, row_of_tile, tm, k0, gain, wg, wu, wd, layer, final_gain=None, emit_gain=None):
    n, d = h.shape
    tf = _tile(wg.shape[-1], (FFN_TILE, 256, 128))
    nf = wg.shape[-1] // tf
    final, emit = final_gain is not None, emit_gain is not None
    vec_spec = pl.BlockSpec((None, 1, d), lambda i, f: (layer, 0, 0))
    in_specs = [
        pl.BlockSpec((tm, d), lambda i, f: (i, 0)),
        _mod_spec(d, row_of_tile, k0),
        _mod_spec(d, row_of_tile, k0 + 1),
        _mod_spec(d, row_of_tile, k0 + 2),
        vec_spec,
        pl.BlockSpec((None, d, tf), lambda i, f: (layer, 0, f)),
        pl.BlockSpec((None, d, tf), lambda i, f: (layer, 0, f)),
        pl.BlockSpec((None, tf, d), lambda i, f: (layer, f, 0)),
    ]
    args = [h, mod, mod, mod, gain, wg, wu, wd]
    row_spec = pl.BlockSpec((tm, d), lambda i, f: (i, 0))
    out_specs, out_shape = row_spec, jax.ShapeDtypeStruct((n, d), F32)
    if final:
        in_specs.append(pl.BlockSpec((1, d), lambda i, f: (0, 0)))
        args.append(final_gain)
    elif emit:
        in_specs += [_mod_spec(d, row_of_tile, 3), _mod_spec(d, row_of_tile, 4), vec_spec]
        args += [mod, mod, emit_gain]
        out_specs = [row_spec, row_spec]
        out_shape = [out_shape, jax.ShapeDtypeStruct((n, d), BF16)]
    return pl.pallas_call(
        functools.partial(_ffn_kernel, nf=nf, final=final, emit=emit),
        grid=(n // tm, nf),
        in_specs=in_specs,
        out_specs=out_specs,
        out_shape=out_shape,
        scratch_shapes=[pltpu.VMEM((tm, d), BF16), pltpu.VMEM((tm, d), F32), pltpu.VMEM((tm, tf), BF16)],
        name="ffn",
        compiler_params=_params("parallel", "arbitrary"),
    )(*args)


def _inproj_kernel(u_ref, w_ref, o_ref):
    o_ref[...] = _dot(u_ref[...], w_ref[...]).astype(BF16)


def _inproj(u, w_in, layer):
    n, d = u.shape
    tn = IN_TILE
    assert w_in.shape[-1] == IN_WIDTH and IN_WIDTH % tn == 0
    tm = _tile(n, (1024, 512, 256, 128))
    return pl.pallas_call(
        _inproj_kernel,
        grid=(n // tm, IN_WIDTH // tn),
        in_specs=[
            pl.BlockSpec((tm, d), lambda i, j: (i, 0)),
            pl.BlockSpec((None, d, tn), lambda i, j: (layer, 0, j)),
        ],
        out_specs=pl.BlockSpec((tm, tn), lambda i, j: (i, j)),
        out_shape=jax.ShapeDtypeStruct((n, IN_WIDTH), BF16),
        name="inproj",
        compiler_params=_params("parallel", "parallel"),
    )(u, w_in)


def _rope_tables(seq):
    t = jnp.arange(seq)
    row = (t // GRID_W).astype(F32)
    col = (t % GRID_W).astype(F32)
    inv = ROPE_THETA ** (-jnp.arange(0, HALF_ROT, 2, dtype=F32) / HALF_ROT)
    ar, ac = row[:, None] * inv[None], col[:, None] * inv[None]
    zero = jnp.zeros_like(ar)
    cos = jnp.concatenate([jnp.cos(ar), jnp.cos(ar), jnp.cos(ac), jnp.cos(ac)], axis=-1)
    sin_lo = jnp.concatenate([-jnp.sin(ar), zero, -jnp.sin(ac), zero], axis=-1)
    sin_hi = jnp.concatenate([zero, jnp.sin(ar), zero, jnp.sin(ac)], axis=-1)
    return cos, sin_lo, sin_hi


def _rope(x, cos, sin_lo, sin_hi):
    return x * cos + pltpu.roll(x, HEAD_DIM - HALF_ROT // 2, 1) * sin_lo + pltpu.roll(x, HALF_ROT // 2, 1) * sin_hi


def _attn_kernel(q_ref, kc_ref, vc_ref, qn_ref, kn_ref, *rest, has_lat):
    if has_lat:
        kl_ref, vl_ref, qcos_ref, qlo_ref, qhi_ref, kcos_ref, klo_ref, khi_ref, o_ref, k_ref, v_ref = rest
    else:
        o_ref, k_ref, v_ref = rest
    nc = kc_ref.shape[0]

    @pl.when(pl.program_id(1) == 0)
    def _():
        kn = kn_ref[...]
        for kv in range(ATTN_KV_HEADS):
            sl = slice(kv * HEAD_DIM, (kv + 1) * HEAD_DIM)
            k_ref[kv, :nc, :] = (_rms(kc_ref[:, sl].astype(F32)) * kn).astype(BF16)
            v_ref[kv, :nc, :HEAD_DIM] = vc_ref[:, sl]
            if has_lat:
                kl = _rms(kl_ref[:, sl].astype(F32)) * kn
                k_ref[kv, nc:, :] = _rope(kl, kcos_ref[...], klo_ref[...], khi_ref[...]).astype(BF16)
                v_ref[kv, nc:, :HEAD_DIM] = vl_ref[:, sl]
            v_ref[kv, :, HEAD_DIM:] = jnp.ones((v_ref.shape[1], HEAD_DIM), BF16)

    qn = qn_ref[...]
    for hd in range(ATTN_HEADS):
        sl = slice(hd * HEAD_DIM, (hd + 1) * HEAD_DIM)
        q = _rms(q_ref[:, sl].astype(F32)) * qn
        if has_lat:
            q = _rope(q, qcos_ref[...], qlo_ref[...], qhi_ref[...])
        s = _dot_nt((q * ATTN_Q_SCALE).astype(BF16), k_ref[hd // ATTN_GROUP])
        p = jnp.exp2(s - jnp.max(s, axis=-1, keepdims=True)).astype(BF16)
        ov = _dot(p, v_ref[hd // ATTN_GROUP])
        o_ref[:, sl] = (ov[:, :HEAD_DIM] / ov[:, HEAD_DIM:]).astype(BF16)


def _attention(proj_q, proj_c, q_norm, k_norm, layer, batch, seq, ctx_len, proj_kv=None, tables=None):
    n = proj_q.shape[0]
    tq = _tile(seq, (256, 128))
    nq = seq // tq
    has_lat = proj_kv is not None
    n_keys = ctx_len + (seq if has_lat else 0)
    col_k, col_v = COL_AK * 128 // KV_WIDTH, COL_AV * 128 // KV_WIDTH
    norm_spec = pl.BlockSpec((None, 1, HEAD_DIM), lambda b, i: (layer, 0, 0))
    in_specs = [
        pl.BlockSpec((tq, ATTN_WIDTH), lambda b, i: (b * nq + i, 0)),
        pl.BlockSpec((ctx_len, KV_WIDTH), lambda b, i: (b, col_k)),
        pl.BlockSpec((ctx_len, KV_WIDTH), lambda b, i: (b, col_v)),
        norm_spec,
        norm_spec,
    ]
    args = [proj_q, proj_c, proj_c, q_norm, k_norm]
    if has_lat:
        in_specs += [
            pl.BlockSpec((seq, KV_WIDTH), lambda b, i: (b, col_k)),
            pl.BlockSpec((seq, KV_WIDTH), lambda b, i: (b, col_v)),
        ]
        in_specs += [pl.BlockSpec((tq, HEAD_DIM), lambda b, i: (i, 0))] * 3
        in_specs += [pl.BlockSpec((seq, HEAD_DIM), lambda b, i: (0, 0))] * 3
        args += [proj_kv, proj_kv, *tables, *tables]
    return pl.pallas_call(
        functools.partial(_attn_kernel, has_lat=has_lat),
        grid=(batch, nq),
        in_specs=in_specs,
        out_specs=pl.BlockSpec((tq, ATTN_WIDTH), lambda b, i: (b * nq + i, 0)),
        out_shape=jax.ShapeDtypeStruct((n, ATTN_WIDTH), BF16),
        scratch_shapes=[
            pltpu.VMEM((ATTN_KV_HEADS, n_keys, HEAD_DIM), BF16),
            pltpu.VMEM((ATTN_KV_HEADS, n_keys, 2 * HEAD_DIM), BF16),
        ],
        name="attention",
        compiler_params=_params("parallel", "arbitrary"),
    )(*args)


def _ret_kernel(lg_ref, q_ref, k_ref, v_ref, g_ref, *rest, seq, ctx_len, chunk, has_ctx):
    if has_ctx:
        kc_ref, vc_ref, o_ref, a_ref, s_ref = rest
    else:
        o_ref, a_ref, s_ref = rest
    hd = pl.program_id(1)
    lgf, lgb = lg_ref[0, hd], lg_ref[1, hd]
    c, dk = chunk, RET_DK
    n = seq // c
    scale = dk ** -0.5
    a = lax.broadcasted_iota(jnp.int32, (c, 1), 0).astype(F32)
    wq_f, wq_b = jnp.exp(lgf * (a + 1.0)), jnp.exp(lgb * (c - a))
    wk_f, wk_b = jnp.exp(lgf * (c - 1.0 - a)) * scale, jnp.exp(lgb * a) * scale
    d = (lax.broadcasted_iota(jnp.int32, (c, c), 0) - lax.broadcasted_iota(jnp.int32, (c, c), 1)).astype(F32)
    dmask = jnp.exp(jnp.where(d >= 0.0, lgf * d, -lgb * d)) * jnp.where(d == 0.0, 2.0 * scale, scale)
    zero = jnp.zeros((1, dk), F32)
    gf, gb = jnp.exp(zero + lgf * c), jnp.exp(zero + lgb * c)

    def kv_outer(kr, vr, j):
        k = kr[j * c:(j + 1) * c, :]
        kk = jnp.concatenate([k * wk_f, k * wk_b], axis=1)
        return _dot(kk.T.astype(BF16), vr[j * c:(j + 1) * c, :])

    sf = jnp.zeros((dk, dk), F32)
    sb = jnp.zeros((dk, dk), F32)
    if has_ctx:
        outer = [kv_outer(kc_ref, vc_ref, j) for j in range(ctx_len // c)]
        for o in outer:
            sf = gf * sf + o[:dk]
        for o in reversed(outer):
            sb = gb * sb + o[dk:]
    for i in range(n):
        a_ref[i] = kv_outer(k_ref, v_ref, i)
    for i in range(n):
        s_ref[i, :dk, :] = sf.astype(BF16)
        sf = gf * sf + a_ref[i, :dk, :]
    for i in reversed(range(n)):
        s_ref[i, dk:, :] = sb.astype(BF16)
        sb = gb * sb + a_ref[i, dk:, :]
    for i in range(n):
        rows = slice(i * c, (i + 1) * c)
        q = q_ref[rows, :]
        s = _dot_nt(q, k_ref[rows, :])
        o = _dot((s * dmask).astype(BF16), v_ref[rows, :])
        qq = jnp.concatenate([q * wq_f, q * wq_b], axis=1).astype(BF16)
        o = o + _dot(qq, s_ref[i])
        o_ref[rows, :] = (_silu(g_ref[rows, :].astype(F32)) * _rms(o)).astype(BF16)


def _retention(proj, log_gamma, batch, seq, proj_ctx=None, ctx_len=0):
    n = proj.shape[0]
    has_ctx = proj_ctx is not None
    chunk = 256 if seq % 256 == 0 and ctx_len % 256 == 0 else 128
    in_specs = [
        pl.BlockSpec(memory_space=pltpu.SMEM),
        pl.BlockSpec((seq, RET_DK), lambda b, h: (b, COL_RQ + h)),
        pl.BlockSpec((seq, RET_DK), lambda b, h: (b, COL_RK + h)),
        pl.BlockSpec((seq, RET_DK), lambda b, h: (b, COL_RV + h)),
        pl.BlockSpec((seq, RET_DK), lambda b, h: (b, COL_RG + h)),
    ]
    args = [log_gamma, proj, proj, proj, proj]
    if has_ctx:
        in_specs += [
            pl.BlockSpec((ctx_len, RET_DK), lambda b, h: (b, COL_RK + h)),
            pl.BlockSpec((ctx_len, RET_DK), lambda b, h: (b, COL_RV + h)),
        ]
        args += [proj_ctx, proj_ctx]
    return pl.pallas_call(
        functools.partial(_ret_kernel, seq=seq, ctx_len=ctx_len, chunk=chunk, has_ctx=has_ctx),
        grid=(batch, RET_HEADS),
        in_specs=in_specs,
        out_specs=pl.BlockSpec((seq, RET_DK), lambda b, h: (b, h)),
        out_shape=jax.ShapeDtypeStruct((n, RET_WIDTH), BF16),
        scratch_shapes=[
            pltpu.VMEM((seq // chunk, 2 * RET_DK, RET_DK), F32),
            pltpu.VMEM((seq // chunk, 2 * RET_DK, RET_DK), BF16),
        ],
        name="retention",
        compiler_params=_params("parallel", "parallel"),
    )(*args)


def _dft_cos_sin(n):
    k = np.arange(n, dtype=np.int64)
    ang = 2.0 * np.pi * ((k[:, None] * k[None, :]) % n).astype(np.float64) / n
    return np.cos(ang), np.sin(ang)


def _fourier_consts(seq):
    cg, sg = _dft_cos_sin(FOURIER_GROUP_DIM)
    norm = 1.0 / np.sqrt(float(seq) * FOURIER_GROUP_DIM)
    eye = np.eye(FOURIER_GROUPS)
    chan = np.concatenate([np.kron(eye, cg), np.kron(eye, -sg)], axis=1) * norm
    cs, ss = _dft_cos_sin(seq)
    as_bf16 = lambda a: jnp.asarray(a.astype(np.float32)).astype(BF16)
    return as_bf16(chan), as_bf16(cs), as_bf16(ss)


def _fchan_kernel(z_ref, m_ref, o_ref):
    o_ref[...] = _dot(z_ref[...], m_ref[...]).astype(BF16)


def _fseq_kernel(c_ref, s_ref, zc_ref, zs_ref, o_ref):
    o_ref[...] = (_dot(c_ref[...], zc_ref[...]) + _dot(s_ref[...], zs_ref[...])).astype(BF16)


def _fourier(proj, batch, seq, consts):
    n = proj.shape[0]
    chan, cs, ss = consts
    fw = FOURIER_WIDTH
    tm = _tile(n, (1024, 512, 256, 128))
    zcs = pl.pallas_call(
        _fchan_kernel,
        grid=(n // tm,),
        in_specs=[
            pl.BlockSpec((tm, fw), lambda i: (i, COL_FZ * 128 // fw)),
            pl.BlockSpec((fw, 2 * fw), lambda i: (0, 0)),
        ],
        out_specs=pl.BlockSpec((tm, 2 * fw), lambda i: (i, 0)),
        out_shape=jax.ShapeDtypeStruct((n, 2 * fw), BF16),
        name="fourier_chan",
        compiler_params=_params("parallel"),
    )(proj, chan)
    tk = _tile(seq, (512, 256, 128))
    nk = seq // tk
    return pl.pallas_call(
        _fseq_kernel,
        grid=(nk, batch),
        in_specs=[
            pl.BlockSpec((tk, seq), lambda k, b: (k, 0)),
            pl.BlockSpec((tk, seq), lambda k, b: (k, 0)),
            pl.BlockSpec((seq, fw), lambda k, b: (b, 0)),
            pl.BlockSpec((seq, fw), lambda k, b: (b, 1)),
        ],
        out_specs=pl.BlockSpec((tk, fw), lambda k, b: (b * nk + k, 0)),
        out_shape=jax.ShapeDtypeStruct((n, fw), BF16),
        name="fourier_seq",
        compiler_params=_params("parallel", "parallel"),
    )(cs, ss, zcs, zcs)


MERGE_TILE = 512


def _gate_kernel(u_ref, yf_ref, ya_ref, yr_ref, wf_ref, wa_ref, wr_ref,
                 gf_ref, ga_ref, gr_ref, bf_ref, ba_ref, br_ref, o_ref):
    u = u_ref[...]
    gates = [_dot(u, g[...]) for g in (gf_ref, ga_ref, gr_ref)]
    branches = [_dot(y[...], w[...]) for y, w in ((yf_ref, wf_ref), (ya_ref, wa_ref), (yr_ref, wr_ref))]
    biases = [b[...] for b in (bf_ref, ba_ref, br_ref)]
    for r in range(0, o_ref.shape[0], ACT_CHUNK):
        rows = slice(r, r + ACT_CHUNK)
        terms = [jax.nn.sigmoid(g[rows, :] + b) * y[rows, :] for g, b, y in zip(gates, biases, branches)]
        o_ref[rows, :] = (terms[0] + terms[1] + terms[2]).astype(BF16)


def _out_kernel(h_ref, gt_ref, m_ref, wo_ref, o_ref, *, tn):
    m = m_ref[...]
    for c in range(0, o_ref.shape[1], tn):
        cols = slice(c, c + tn)
        o_ref[:, cols] = h_ref[:, cols] + gt_ref[:, cols] * _dot(m, wo_ref[:, cols])


def _merge(h, mod, row_of_tile, tm, u, y_f, y_a, y_r, w_bf, w_ba, w_br, w_mg, b_mg, w_out, layer):
    n, d = h.shape
    tn = _tile(d, (MERGE_TILE, 256, 128))
    nn = d // tn
    tg = _tile(n, (1024, 512, 256, 128))
    wspec = lambda rows: pl.BlockSpec((None, rows, tn), lambda i, j: (layer, 0, j))
    gspec = lambda k: pl.BlockSpec((None, d, tn), lambda i, j: (layer, 0, k * nn + j))
    bspec = lambda k: pl.BlockSpec((None, 1, tn), lambda i, j: (layer, 0, k * nn + j))
    yspec = lambda w: pl.BlockSpec((tg, w), lambda i, j: (i, 0))
    m = pl.pallas_call(
        _gate_kernel,
        grid=(n // tg, nn),
        in_specs=[
            yspec(d), yspec(FOURIER_WIDTH), yspec(ATTN_WIDTH), yspec(RET_WIDTH),
            wspec(FOURIER_WIDTH), wspec(ATTN_WIDTH), wspec(RET_WIDTH),
            gspec(0), gspec(1), gspec(2),
            bspec(0), bspec(1), bspec(2),
        ],
        out_specs=pl.BlockSpec((tg, tn), lambda i, j: (i, j)),
        out_shape=jax.ShapeDtypeStruct((n, d), BF16),
        name="merge_gate",
        compiler_params=_params("parallel", "parallel"),
    )(u, y_f, y_a, y_r, w_bf, w_ba, w_br, w_mg, w_mg, w_mg, b_mg, b_mg, b_mg)
    return pl.pallas_call(
        functools.partial(_out_kernel, tn=tn),
        grid=(n // tm,),
        in_specs=[
            pl.BlockSpec((tm, d), lambda i: (i, 0)),
            _mod_spec(d, row_of_tile, 5),
            pl.BlockSpec((tm, d), lambda i: (i, 0)),
            pl.BlockSpec((None, d, d), lambda i: (layer, 0, 0)),
        ],
        out_specs=pl.BlockSpec((tm, d), lambda i: (i, 0)),
        out_shape=jax.ShapeDtypeStruct((n, d), F32),
        name="merge_out",
        compiler_params=_params("parallel"),
    )(h, mod, m, w_out)


def kernel(x, c, ctx, c_ctx, w_ada, b_ada, ffn1_norm, ffn1_w_gate, ffn1_w_up, ffn1_w_down, mix_norm, w_in, q_norm, k_norm, ret_decay, w_branch_fourier, w_branch_attn, w_branch_ret, w_merge_gate, b_merge_gate, w_out, ffn2_norm, ffn2_w_gate, ffn2_w_up, ffn2_w_down, final_norm):
    batch, seq, d = x.shape
    ctx_len = ctx.shape[1]
    depth = w_ada.shape[0]
    vec = lambda g: g.reshape(depth, 1, g.shape[-1])

    rows = -(-(batch + 1) // 16) * 16
    c_all = jnp.concatenate([c, c_ctx[None], jnp.zeros((rows - batch - 1, d), F32)], axis=0)
    mod = _adaln(c_all, w_ada, b_ada).reshape(depth * rows, 1, N_MOD * d)

    tm = _tile(seq, (512, 256, 128))
    tiles_per_sample = seq // tm
    tmc = _tile(batch * ctx_len, (512, 256, 128))

    bf = lambda w: w.astype(BF16)
    ffn1 = (vec(ffn1_norm), bf(ffn1_w_gate), bf(ffn1_w_up), bf(ffn1_w_down))
    ffn2 = (vec(ffn2_norm), bf(ffn2_w_gate), bf(ffn2_w_up), bf(ffn2_w_down))
    mix_gain, w_in_t = vec(mix_norm), bf(w_in)
    qn, kn = vec(q_norm), vec(k_norm)
    w_bf, w_ba, w_br = bf(w_branch_fourier), bf(w_branch_attn), bf(w_branch_ret)
    w_mg, b_mg, w_o = bf(w_merge_gate), vec(b_merge_gate), bf(w_out)
    log_gamma = -jnp.exp(ret_decay.astype(F32))
    tables = _rope_tables(seq)
    four_lat = _fourier_consts(seq)
    four_ctx = _fourier_consts(ctx_len)

    h = x.reshape(batch * seq, d)
    hc = ctx.reshape(batch * ctx_len, d)
    for l in range(depth):
        need_ctx = l < depth - 1
        lat_row = lambda i, l=l: l * rows + i // tiles_per_sample
        ctx_row = lambda i, l=l: l * rows + batch

        h, u = _ffn(h, mod, lat_row, tm, 0, *ffn1, l, emit_gain=mix_gain)
        hc, uc = _ffn(hc, mod, ctx_row, tmc, 0, *ffn1, l, emit_gain=mix_gain)

        proj = _inproj(u, w_in_t, l)
        proj_c = _inproj(uc, w_in_t, l)

        y_a = _attention(proj, proj_c, qn, kn, l, batch, seq, ctx_len, proj, tables)
        y_r = _retention(proj, log_gamma[l], batch, seq, proj_c, ctx_len)
        y_f = _fourier(proj, batch, seq, four_lat)
        h = _merge(h, mod, lat_row, tm, u, y_f, y_a, y_r, w_bf, w_ba, w_br, w_mg, b_mg, w_o, l)
        last = final_norm.reshape(1, d) if l == depth - 1 else None
        h = _ffn(h, mod, lat_row, tm, 6, *ffn2, l, final_gain=last)

        if need_ctx:
            yc_a = _attention(proj_c, proj_c, qn, kn, l, batch, ctx_len, ctx_len)
            yc_r = _retention(proj_c, log_gamma[l], batch, ctx_len)
            yc_f = _fourier(proj_c, batch, ctx_len, four_ctx)
            hc = _merge(hc, mod, ctx_row, tmc, uc, yc_f, yc_a, yc_r, w_bf, w_ba, w_br, w_mg, b_mg, w_o, l)
            hc = _ffn(hc, mod, ctx_row, tmc, 6, *ffn2, l)
    return h.reshape(batch, seq, d)
```

```python
import functools

import numpy as np
import jax
import jax.numpy as jnp
from jax import lax
from jax.experimental import pallas as pl
from jax.experimental.pallas import tpu as pltpu

F32 = jnp.float32
BF16 = jnp.bfloat16

EPS = 1e-6
N_MOD = 9
GRID_W = 64
HEAD_DIM = 128
HALF_ROT = HEAD_DIM // 2
ATTN_HEADS = 8
ATTN_KV_HEADS = 2
ATTN_GROUP = ATTN_HEADS // ATTN_KV_HEADS
ATTN_WIDTH = ATTN_HEADS * HEAD_DIM
KV_WIDTH = ATTN_KV_HEADS * HEAD_DIM
RET_HEADS = 4
RET_DK = 128
RET_WIDTH = RET_HEADS * RET_DK
FOURIER_GROUPS = 4
FOURIER_GROUP_DIM = 128
FOURIER_WIDTH = FOURIER_GROUPS * FOURIER_GROUP_DIM
IN_WIDTH = ATTN_WIDTH + 2 * KV_WIDTH + 4 * RET_WIDTH + FOURIER_WIDTH
ROPE_THETA = 10000.0

COL_AK = ATTN_WIDTH // 128
COL_AV = COL_AK + KV_WIDTH // 128
COL_RQ = COL_AV + KV_WIDTH // 128
COL_RK = COL_RQ + RET_WIDTH // 128
COL_RV = COL_RK + RET_WIDTH // 128
COL_RG = COL_RV + RET_WIDTH // 128
COL_FZ = COL_RG + RET_WIDTH // 128

IN_TILE = ATTN_WIDTH

ATTN_Q_SCALE = HEAD_DIM ** -0.5 * float(np.log2(np.e))

V7X_VMEM_LIMIT_BYTES = 56 * 1024 * 1024
V7X_FFN_VMEM_LIMIT_BYTES = 60 * 1024 * 1024


def _params(*sem, vmem_limit_bytes=V7X_VMEM_LIMIT_BYTES):
    return pltpu.CompilerParams(dimension_semantics=sem, vmem_limit_bytes=vmem_limit_bytes)


def _tile(n, prefs):
    for p in prefs:
        if n % p == 0:
            return p
    return n


def _dot(a, b):
    return jnp.dot(a, b, preferred_element_type=F32)


def _dot_nt(a, b):
    return lax.dot_general(a, b, (((1,), (1,)), ((), ())), preferred_element_type=F32)


def _rms(x):
    return x * lax.rsqrt(jnp.mean(x * x, axis=-1, keepdims=True) + EPS)


def _silu(x):
    return x * jax.nn.sigmoid(x)


def _adaln_kernel(c_ref, w_ref, b_ref, o_ref):
    a = _silu(c_ref[...]).astype(BF16)
    o_ref[...] = _dot(a, w_ref[...].astype(BF16)) + b_ref[...]


def _adaln(c_all, w_ada, b_ada):
    depth, d, nd = w_ada.shape
    r = c_all.shape[0]
    tn = _tile(nd, (1024, 512, 256, 128))
    return pl.pallas_call(
        _adaln_kernel,
        grid=(depth, nd // tn),
        in_specs=[
            pl.BlockSpec((r, d), lambda l, j: (0, 0)),
            pl.BlockSpec((None, d, tn), lambda l, j: (l, 0, j)),
            pl.BlockSpec((None, 1, tn), lambda l, j: (l, 0, j)),
        ],
        out_specs=pl.BlockSpec((None, r, tn), lambda l, j: (l, 0, j)),
        out_shape=jax.ShapeDtypeStruct((depth, r, nd), F32),
        name="adaln",
        compiler_params=_params("parallel", "parallel"),
    )(c_all, w_ada, b_ada.reshape(depth, 1, nd))


def _mod_spec(d, row_of_tile, k):
    return pl.BlockSpec((None, 1, d), lambda i, *_: (row_of_tile(i), 0, k))


FFN_TILE = 512
FFN_ROWS = 1024
FFN_H_DOUBLE_BUFFER_BYTES = 4 * 1024 * 1024
ACT_CHUNK = 32
ROW_CHUNK = 16


def _ffn_kernel(h_ref, sh_ref, sc_ref, gt_ref, gain_ref, wg_ref, wu_ref, wd_ref, *rest, nf, final, emit):
    if final:
        fg_ref, o_ref, xn_ref, a_ref = rest
    elif emit:
        sh2_ref, sc2_ref, gain2_ref, o_ref, u_ref, xn_ref, a_ref = rest
    else:
        o_ref, xn_ref, a_ref = rest
    acc_ref = o_ref
    f = pl.program_id(1)
    tm = h_ref.shape[0]

    @pl.when(f == 0)
    def _():
        gs, sh = gain_ref[...] * (1.0 + sc_ref[...]), sh_ref[...]
        for r in range(0, tm, ROW_CHUNK):
            rows = slice(r, r + ROW_CHUNK)
            xn_ref[rows, :] = (_rms(h_ref[rows, :]) * gs + sh).astype(BF16)

    @pl.when((pl.program_id(0) < 2) & (f == 0))
    def _():
        acc_ref[...] = jnp.zeros_like(acc_ref)

    xn = xn_ref[...]
    g = _dot(xn, wg_ref[...])
    u = _dot(xn, wu_ref[...])
    for r in range(0, tm, ACT_CHUNK):
        rows = slice(r, r + ACT_CHUNK)
        a_ref[rows, :] = (_silu(g[rows, :]) * u[rows, :]).astype(BF16)
    acc_ref[...] = jnp.where(f == 0, 0.0, acc_ref[...]) + _dot(a_ref[...], wd_ref[...])

    @pl.when(f == nf - 1)
    def _():
        half_gate = 0.5 * gt_ref[...]
        if final:
            fg = fg_ref[...]
        if emit:
            gs2, sh2 = gain2_ref[...] * (1.0 + sc2_ref[...]), sh2_ref[...]
        for r in range(0, tm, ROW_CHUNK):
            rows = slice(r, r + ROW_CHUNK)
            y = h_ref[rows, :] + half_gate * acc_ref[rows, :]
            if final:
                y = _rms(y) * fg
            o_ref[rows, :] = y
            if emit:
                u_ref[rows, :] = (_rms(y) * gs2 + sh2).astype(BF16)


def _ffn(h, mod, row_of_tile, tm, k0, gain, wg, wu, wd, layer, final_gain=None, emit_gain=None):
    n, d = h.shape
    tf = _tile(wg.shape[-1], (FFN_TILE, 256, 128))
    nf = wg.shape[-1] // tf
    final, emit = final_gain is not None, emit_gain is not None
    vec_spec = pl.BlockSpec((None, 1, d), lambda i, f: (layer, 0, 0))
    big = tm * d * 4 > FFN_H_DOUBLE_BUFFER_BYTES
    h_mode = dict(pipeline_mode=pl.Buffered(1)) if big and (emit or final) else {}
    in_specs = [
        pl.BlockSpec((tm, d), lambda i, f: (i, 0), **h_mode),
        _mod_spec(d, row_of_tile, k0),
        _mod_spec(d, row_of_tile, k0 + 1),
        _mod_spec(d, row_of_tile, k0 + 2),
        vec_spec,
        pl.BlockSpec((None, d, tf), lambda i, f: (layer, 0, f)),
        pl.BlockSpec((None, d, tf), lambda i, f: (layer, 0, f)),
        pl.BlockSpec((None, tf, d), lambda i, f: (layer, f, 0)),
    ]
    args = [h, mod, mod, mod, gain, wg, wu, wd]
    row_spec = pl.BlockSpec((tm, d), lambda i, f: (i, 0))
    out_specs, out_shape = row_spec, jax.ShapeDtypeStruct((n, d), F32)
    if final:
        in_specs.append(pl.BlockSpec((1, d), lambda i, f: (0, 0)))
        args.append(final_gain)
    elif emit:
        in_specs += [_mod_spec(d, row_of_tile, 3), _mod_spec(d, row_of_tile, 4), vec_spec]
        args += [mod, mod, emit_gain]
        out_specs = [row_spec, row_spec]
        out_shape = [out_shape, jax.ShapeDtypeStruct((n, d), BF16)]
    return pl.pallas_call(
        functools.partial(_ffn_kernel, nf=nf, final=final, emit=emit),
        grid=(n // tm, nf),
        in_specs=in_specs,
        out_specs=out_specs,
        out_shape=out_shape,
        scratch_shapes=[pltpu.VMEM((tm, d), BF16), pltpu.VMEM((tm, tf), BF16)],
        name="ffn",
        compiler_params=_params("arbitrary", "arbitrary", vmem_limit_bytes=V7X_FFN_VMEM_LIMIT_BYTES),
    )(*args)


def _inproj_kernel(u_ref, w_ref, o_ref):
    o_ref[...] = _dot(u_ref[...], w_ref[...]).astype(BF16)


def _inproj(u, w_in, layer):
    n, d = u.shape
    tn = IN_TILE
    assert w_in.shape[-1] == IN_WIDTH and IN_WIDTH % tn == 0
    tm = _tile(n, (1024, 512, 256, 128))
    return pl.pallas_call(
        _inproj_kernel,
        grid=(n // tm, IN_WIDTH // tn),
        in_specs=[
            pl.BlockSpec((tm, d), lambda i, j: (i, 0)),
            pl.BlockSpec((None, d, tn), lambda i, j: (layer, 0, j)),
        ],
        out_specs=pl.BlockSpec((tm, tn), lambda i, j: (i, j)),
        out_shape=jax.ShapeDtypeStruct((n, IN_WIDTH), BF16),
        name="inproj",
        compiler_params=_params("parallel", "parallel"),
    )(u, w_in)


def _rope_tables(seq):
    t = jnp.arange(seq)
    row = (t // GRID_W).astype(F32)
    col = (t % GRID_W).astype(F32)
    inv = ROPE_THETA ** (-jnp.arange(0, HALF_ROT, 2, dtype=F32) / HALF_ROT)
    ar, ac = row[:, None] * inv[None], col[:, None] * inv[None]
    zero = jnp.zeros_like(ar)
    cos = jnp.concatenate([jnp.cos(ar), jnp.cos(ar), jnp.cos(ac), jnp.cos(ac)], axis=-1)
    sin_lo = jnp.concatenate([-jnp.sin(ar), zero, -jnp.sin(ac), zero], axis=-1)
    sin_hi = jnp.concatenate([zero, jnp.sin(ar), zero, jnp.sin(ac)], axis=-1)
    return cos, sin_lo, sin_hi


def _rope(x, cos, sin_lo, sin_hi):
    return x * cos + pltpu.roll(x, HEAD_DIM - HALF_ROT // 2, 1) * sin_lo + pltpu.roll(x, HALF_ROT // 2, 1) * sin_hi


def _attn_kernel(q_ref, kc_ref, vc_ref, qn_ref, kn_ref, *rest, has_lat):
    if has_lat:
        kl_ref, vl_ref, qcos_ref, qlo_ref, qhi_ref, kcos_ref, klo_ref, khi_ref, o_ref, k_ref, v_ref = rest
    else:
        o_ref, k_ref, v_ref = rest
    nc = kc_ref.shape[0]

    @pl.when(pl.program_id(1) == 0)
    def _():
        kn = kn_ref[...]
        for kv in range(ATTN_KV_HEADS):
            sl = slice(kv * HEAD_DIM, (kv + 1) * HEAD_DIM)
            k_ref[kv, :nc, :] = (_rms(kc_ref[:, sl].astype(F32)) * kn).astype(BF16)
            v_ref[kv, :nc, :HEAD_DIM] = vc_ref[:, sl]
            if has_lat:
                kl = _rms(kl_ref[:, sl].astype(F32)) * kn
                k_ref[kv, nc:, :] = _rope(kl, kcos_ref[...], klo_ref[...], khi_ref[...]).astype(BF16)
                v_ref[kv, nc:, :HEAD_DIM] = vl_ref[:, sl]
            v_ref[kv, :, HEAD_DIM:] = jnp.ones((v_ref.shape[1], HEAD_DIM), BF16)

    qn = qn_ref[...]
    for hd in range(ATTN_HEADS):
        sl = slice(hd * HEAD_DIM, (hd + 1) * HEAD_DIM)
        q = _rms(q_ref[:, sl].astype(F32)) * qn
        if has_lat:
            q = _rope(q, qcos_ref[...], qlo_ref[...], qhi_ref[...])
        s = _dot_nt((q * ATTN_Q_SCALE).astype(BF16), k_ref[hd // ATTN_GROUP])
        p = jnp.exp2(s - jnp.max(s, axis=-1, keepdims=True)).astype(BF16)
        ov = _dot(p, v_ref[hd // ATTN_GROUP])
        o_ref[:, sl] = (ov[:, :HEAD_DIM] / ov[:, HEAD_DIM:]).astype(BF16)


def _attention(proj_q, proj_c, q_norm, k_norm, layer, batch, seq, ctx_len, proj_kv=None, tables=None):
    n = proj_q.shape[0]
    tq = _tile(seq, (256, 128))
    nq = seq // tq
    has_lat = proj_kv is not None
    n_keys = ctx_len + (seq if has_lat else 0)
    col_k, col_v = COL_AK * 128 // KV_WIDTH, COL_AV * 128 // KV_WIDTH
    norm_spec = pl.BlockSpec((None, 1, HEAD_DIM), lambda b, i: (layer, 0, 0))
    in_specs = [
        pl.BlockSpec((tq, ATTN_WIDTH), lambda b, i: (b * nq + i, 0)),
        pl.BlockSpec((ctx_len, KV_WIDTH), lambda b, i: (b, col_k)),
        pl.BlockSpec((ctx_len, KV_WIDTH), lambda b, i: (b, col_v)),
        norm_spec,
        norm_spec,
    ]
    args = [proj_q, proj_c, proj_c, q_norm, k_norm]
    if has_lat:
        in_specs += [
            pl.BlockSpec((seq, KV_WIDTH), lambda b, i: (b, col_k)),
            pl.BlockSpec((seq, KV_WIDTH), lambda b, i: (b, col_v)),
        ]
        in_specs += [pl.BlockSpec((tq, HEAD_DIM), lambda b, i: (i, 0))] * 3
        in_specs += [pl.BlockSpec((seq, HEAD_DIM), lambda b, i: (0, 0))] * 3
        args += [proj_kv, proj_kv, *tables, *tables]
    return pl.pallas_call(
        functools.partial(_attn_kernel, has_lat=has_lat),
        grid=(batch, nq),
        in_specs=in_specs,
        out_specs=pl.BlockSpec((tq, ATTN_WIDTH), lambda b, i: (b * nq + i, 0)),
        out_shape=jax.ShapeDtypeStruct((n, ATTN_WIDTH), BF16),
        scratch_shapes=[
            pltpu.VMEM((ATTN_KV_HEADS, n_keys, HEAD_DIM), BF16),
            pltpu.VMEM((ATTN_KV_HEADS, n_keys, 2 * HEAD_DIM), BF16),
        ],
        name="attention",
        compiler_params=_params("parallel", "arbitrary"),
    )(*args)


def _ret_kernel(lg_ref, q_ref, k_ref, v_ref, g_ref, *rest, seq, ctx_len, chunk, has_ctx):
    if has_ctx:
        kc_ref, vc_ref, o_ref, a_ref, s_ref = rest
    else:
        o_ref, a_ref, s_ref = rest
    hd = pl.program_id(1)
    lgf, lgb = lg_ref[0, hd], lg_ref[1, hd]
    c, dk = chunk, RET_DK
    n = seq // c
    scale = dk ** -0.5
    a = lax.broadcasted_iota(jnp.int32, (c, 1), 0).astype(F32)
    wq_f, wq_b = jnp.exp(lgf * (a + 1.0)), jnp.exp(lgb * (c - a))
    wk_f, wk_b = jnp.exp(lgf * (c - 1.0 - a)) * scale, jnp.exp(lgb * a) * scale
    d = (lax.broadcasted_iota(jnp.int32, (c, c), 0) - lax.broadcasted_iota(jnp.int32, (c, c), 1)).astype(F32)
    dmask = jnp.exp(jnp.where(d >= 0.0, lgf * d, -lgb * d)) * jnp.where(d == 0.0, 2.0 * scale, scale)
    zero = jnp.zeros((1, dk), F32)
    gf, gb = jnp.exp(zero + lgf * c), jnp.exp(zero + lgb * c)

    def kv_outer(kr, vr, j):
        k = kr[j * c:(j + 1) * c, :]
        kk = jnp.concatenate([k * wk_f, k * wk_b], axis=1)
        return _dot(kk.T.astype(BF16), vr[j * c:(j + 1) * c, :])

    sf = jnp.zeros((dk, dk), F32)
    sb = jnp.zeros((dk, dk), F32)
    if has_ctx:
        outer = [kv_outer(kc_ref, vc_ref, j) for j in range(ctx_len // c)]
        for o in outer:
            sf = gf * sf + o[:dk]
        for o in reversed(outer):
            sb = gb * sb + o[dk:]
    for i in range(n):
        a_ref[i] = kv_outer(k_ref, v_ref, i)
    for i in range(n):
        s_ref[i, :dk, :] = sf.astype(BF16)
        sf = gf * sf + a_ref[i, :dk, :]
    for i in reversed(range(n)):
        s_ref[i, dk:, :] = sb.astype(BF16)
        sb = gb * sb + a_ref[i, dk:, :]
    for i in range(n):
        rows = slice(i * c, (i + 1) * c)
        q = q_ref[rows, :]
        s = _dot_nt(q, k_ref[rows, :])
        o = _dot((s * dmask).astype(BF16), v_ref[rows, :])
        qq = jnp.concatenate([q * wq_f, q * wq_b], axis=1).astype(BF16)
        o = o + _dot(qq, s_ref[i])
        o_ref[rows, :] = (_silu(g_ref[rows, :].astype(F32)) * _rms(o)).astype(BF16)


def _retention(proj, log_gamma, batch, seq, proj_ctx=None, ctx_len=0):
    n = proj.shape[0]
    has_ctx = proj_ctx is not None
    chunk = 256 if seq % 256 == 0 and ctx_len % 256 == 0 else 128
    in_specs = [
        pl.BlockSpec(memory_space=pltpu.SMEM),
        pl.BlockSpec((seq, RET_DK), lambda b, h: (b, COL_RQ + h)),
        pl.BlockSpec((seq, RET_DK), lambda b, h: (b, COL_RK + h)),
        pl.BlockSpec((seq, RET_DK), lambda b, h: (b, COL_RV + h)),
        pl.BlockSpec((seq, RET_DK), lambda b, h: (b, COL_RG + h)),
    ]
    args = [log_gamma, proj, proj, proj, proj]
    if has_ctx:
        in_specs += [
            pl.BlockSpec((ctx_len, RET_DK), lambda b, h: (b, COL_RK + h)),
            pl.BlockSpec((ctx_len, RET_DK), lambda b, h: (b, COL_RV + h)),
        ]
        args += [proj_ctx, proj_ctx]
    return pl.pallas_call(
        functools.partial(_ret_kernel, seq=seq, ctx_len=ctx_len, chunk=chunk, has_ctx=has_ctx),
        grid=(batch, RET_HEADS),
        in_specs=in_specs,
        out_specs=pl.BlockSpec((seq, RET_DK), lambda b, h: (b, h)),
        out_shape=jax.ShapeDtypeStruct((n, RET_WIDTH), BF16),
        scratch_shapes=[
            pltpu.VMEM((seq // chunk, 2 * RET_DK, RET_DK), F32),
            pltpu.VMEM((seq // chunk, 2 * RET_DK, RET_DK), BF16),
        ],
        name="retention",
        compiler_params=_params("parallel", "parallel"),
    )(*args)


def _dft_cos_sin(n):
    k = np.arange(n, dtype=np.int64)
    ang = 2.0 * np.pi * ((k[:, None] * k[None, :]) % n).astype(np.float64) / n
    return np.cos(ang), np.sin(ang)


def _fourier_consts(seq):
    cg, sg = _dft_cos_sin(FOURIER_GROUP_DIM)
    norm = 1.0 / np.sqrt(float(seq) * FOURIER_GROUP_DIM)
    eye = np.eye(FOURIER_GROUPS)
    chan = np.concatenate([np.kron(eye, cg), np.kron(eye, -sg)], axis=1) * norm
    cs, ss = _dft_cos_sin(seq)
    as_bf16 = lambda a: jnp.asarray(a.astype(np.float32)).astype(BF16)
    return as_bf16(chan), as_bf16(cs), as_bf16(ss)


def _fchan_kernel(z_ref, m_ref, o_ref):
    o_ref[...] = _dot(z_ref[...], m_ref[...]).astype(BF16)


def _fseq_kernel(c_ref, s_ref, zc_ref, zs_ref, o_ref):
    o_ref[...] = (_dot(c_ref[...], zc_ref[...]) + _dot(s_ref[...], zs_ref[...])).astype(BF16)


def _fourier(proj, batch, seq, consts):
    n = proj.shape[0]
    chan, cs, ss = consts
    fw = FOURIER_WIDTH
    tm = _tile(n, (1024, 512, 256, 128))
    zcs = pl.pallas_call(
        _fchan_kernel,
        grid=(n // tm,),
        in_specs=[
            pl.BlockSpec((tm, fw), lambda i: (i, COL_FZ * 128 // fw)),
            pl.BlockSpec((fw, 2 * fw), lambda i: (0, 0)),
        ],
        out_specs=pl.BlockSpec((tm, 2 * fw), lambda i: (i, 0)),
        out_shape=jax.ShapeDtypeStruct((n, 2 * fw), BF16),
        name="fourier_chan",
        compiler_params=_params("parallel"),
    )(proj, chan)
    tk = _tile(seq, (512, 256, 128))
    nk = seq // tk
    return pl.pallas_call(
        _fseq_kernel,
        grid=(nk, batch),
        in_specs=[
            pl.BlockSpec((tk, seq), lambda k, b: (k, 0)),
            pl.BlockSpec((tk, seq), lambda k, b: (k, 0)),
            pl.BlockSpec((seq, fw), lambda k, b: (b, 0)),
            pl.BlockSpec((seq, fw), lambda k, b: (b, 1)),
        ],
        out_specs=pl.BlockSpec((tk, fw), lambda k, b: (b * nk + k, 0)),
        out_shape=jax.ShapeDtypeStruct((n, fw), BF16),
        name="fourier_seq",
        compiler_params=_params("parallel", "parallel"),
    )(cs, ss, zcs, zcs)


MERGE_TILE = 512


def _gate_kernel(u_ref, yf_ref, ya_ref, yr_ref, wf_ref, wa_ref, wr_ref,
                 gf_ref, ga_ref, gr_ref, bf_ref, ba_ref, br_ref, o_ref):
    u = u_ref[...]
    gates = [_dot(u, g[...]) for g in (gf_ref, ga_ref, gr_ref)]
    branches = [_dot(y[...], w[...]) for y, w in ((yf_ref, wf_ref), (ya_ref, wa_ref), (yr_ref, wr_ref))]
    biases = [b[...] for b in (bf_ref, ba_ref, br_ref)]
    for r in range(0, o_ref.shape[0], ACT_CHUNK):
        rows = slice(r, r + ACT_CHUNK)
        terms = [jax.nn.sigmoid(g[rows, :] + b) * y[rows, :] for g, b, y in zip(gates, biases, branches)]
        o_ref[rows, :] = (terms[0] + terms[1] + terms[2]).astype(BF16)


def _out_kernel(h_ref, gt_ref, m_ref, wo_ref, o_ref, *, tn):
    m = m_ref[...]
    for c in range(0, o_ref.shape[1], tn):
        cols = slice(c, c + tn)
        o_ref[:, cols] = h_ref[:, cols] + gt_ref[:, cols] * _dot(m, wo_ref[:, cols])


def _merge(h, mod, row_of_tile, tm, u, y_f, y_a, y_r, w_bf, w_ba, w_br, w_mg, b_mg, w_out, layer):
    n, d = h.shape
    tn = _tile(d, (MERGE_TILE, 256, 128))
    nn = d // tn
    tg = _tile(n, (1024, 512, 256, 128))
    wspec = lambda rows: pl.BlockSpec((None, rows, tn), lambda i, j: (layer, 0, j))
    gspec = lambda k: pl.BlockSpec((None, d, tn), lambda i, j: (layer, 0, k * nn + j))
    bspec = lambda k: pl.BlockSpec((None, 1, tn), lambda i, j: (layer, 0, k * nn + j))
    yspec = lambda w: pl.BlockSpec((tg, w), lambda i, j: (i, 0))
    m = pl.pallas_call(
        _gate_kernel,
        grid=(n // tg, nn),
        in_specs=[
            yspec(d), yspec(FOURIER_WIDTH), yspec(ATTN_WIDTH), yspec(RET_WIDTH),
            wspec(FOURIER_WIDTH), wspec(ATTN_WIDTH), wspec(RET_WIDTH),
            gspec(0), gspec(1), gspec(2),
            bspec(0), bspec(1), bspec(2),
        ],
        out_specs=pl.BlockSpec((tg, tn), lambda i, j: (i, j)),
        out_shape=jax.ShapeDtypeStruct((n, d), BF16),
        name="merge_gate",
        compiler_params=_params("parallel", "parallel"),
    )(u, y_f, y_a, y_r, w_bf, w_ba, w_br, w_mg, w_mg, w_mg, b_mg, b_mg, b_mg)
    return pl.pallas_call(
        functools.partial(_out_kernel, tn=tn),
        grid=(n // tm,),
        in_specs=[
            pl.BlockSpec((tm, d), lambda i: (i, 0)),
            _mod_spec(d, row_of_tile, 5),
            pl.BlockSpec((tm, d), lambda i: (i, 0)),
            pl.BlockSpec((None, d, d), lambda i: (layer, 0, 0)),
        ],
        out_specs=pl.BlockSpec((tm, d), lambda i: (i, 0)),
        out_shape=jax.ShapeDtypeStruct((n, d), F32),
        name="merge_out",
        compiler_params=_params("parallel"),
    )(h, mod, m, w_out)


def kernel(x, c, ctx, c_ctx, w_ada, b_ada, ffn1_norm, ffn1_w_gate, ffn1_w_up, ffn1_w_down, mix_norm, w_in, q_norm, k_norm, ret_decay, w_branch_fourier, w_branch_attn, w_branch_ret, w_merge_gate, b_merge_gate, w_out, ffn2_norm, ffn2_w_gate, ffn2_w_up, ffn2_w_down, final_norm):
    batch, seq, d = x.shape
    ctx_len = ctx.shape[1]
    depth = w_ada.shape[0]
    vec = lambda g: g.reshape(depth, 1, g.shape[-1])

    rows = -(-(batch + 1) // 16) * 16
    c_all = jnp.concatenate([c, c_ctx[None], jnp.zeros((rows - batch - 1, d), F32)], axis=0)
    mod = _adaln(c_all, w_ada, b_ada).reshape(depth * rows, 1, N_MOD * d)

    tm = _tile(seq, (512, 256, 128))
    tmc = _tile(batch * ctx_len, (512, 256, 128))
    tm_ffn = _tile(seq, (FFN_ROWS, 512, 256, 128))
    tmc_ffn = _tile(batch * ctx_len, (FFN_ROWS, 512, 256, 128))

    bf = lambda w: w.astype(BF16)
    ffn1 = (vec(ffn1_norm), bf(ffn1_w_gate), bf(ffn1_w_up), bf(ffn1_w_down))
    ffn2 = (vec(ffn2_norm), bf(ffn2_w_gate), bf(ffn2_w_up), bf(ffn2_w_down))
    mix_gain, w_in_t = vec(mix_norm), bf(w_in)
    qn, kn = vec(q_norm), vec(k_norm)
    w_bf, w_ba, w_br = bf(w_branch_fourier), bf(w_branch_attn), bf(w_branch_ret)
    w_mg, b_mg, w_o = bf(w_merge_gate), vec(b_merge_gate), bf(w_out)
    log_gamma = -jnp.exp(ret_decay.astype(F32))
    tables = _rope_tables(seq)
    four_lat = _fourier_consts(seq)
    four_ctx = _fourier_consts(ctx_len)

    h = x.reshape(batch * seq, d)
    hc = ctx.reshape(batch * ctx_len, d)
    for l in range(depth):
        need_ctx = l < depth - 1
        lat_rows = lambda t, l=l: (lambda i: l * rows + i // (seq // t))
        lat_row = lat_rows(tm)
        ctx_row = lambda i, l=l: l * rows + batch

        h, u = _ffn(h, mod, lat_rows(tm_ffn), tm_ffn, 0, *ffn1, l, emit_gain=mix_gain)
        hc, uc = _ffn(hc, mod, ctx_row, tmc_ffn, 0, *ffn1, l, emit_gain=mix_gain)

        proj = _inproj(u, w_in_t, l)
        proj_c = _inproj(uc, w_in_t, l)

        y_a = _attention(proj, proj_c, qn, kn, l, batch, seq, ctx_len, proj, tables)
        y_r = _retention(proj, log_gamma[l], batch, seq, proj_c, ctx_len)
        y_f = _fourier(proj, batch, seq, four_lat)
        h = _merge(h, mod, lat_row, tm, u, y_f, y_a, y_r, w_bf, w_ba, w_br, w_mg, b_mg, w_o, l)
        last = final_norm.reshape(1, d) if l == depth - 1 else None
        h = _ffn(h, mod, lat_rows(tm_ffn), tm_ffn, 6, *ffn2, l, final_gain=last)

        if need_ctx:
            yc_a = _attention(proj_c, proj_c, qn, kn, l, batch, ctx_len, ctx_len)
            yc_r = _retention(proj_c, log_gamma[l], batch, ctx_len)
            yc_f = _fourier(proj_c, batch, ctx_len, four_ctx)
            hc = _merge(hc, mod, ctx_row, tmc, uc, yc_f, yc_a, yc_r, w_bf, w_ba, w_br, w_mg, b_mg, w_o, l)
            hc = _ffn(hc, mod, ctx_row, tmc_ffn, 6, *ffn2, l)
    return h.reshape(batch, seq, d)
```

```python
import functools

import numpy as np
import jax
import jax.numpy as jnp
from jax import lax
from jax.experimental import pallas as pl
from jax.experimental.pallas import tpu as pltpu

F32 = jnp.float32
BF16 = jnp.bfloat16

EPS = 1e-6
N_MOD = 9
GRID_W = 64
HEAD_DIM = 128
HALF_ROT = HEAD_DIM // 2
ATTN_HEADS = 8
ATTN_KV_HEADS = 2
ATTN_GROUP = ATTN_HEADS // ATTN_KV_HEADS
ATTN_WIDTH = ATTN_HEADS * HEAD_DIM
KV_WIDTH = ATTN_KV_HEADS * HEAD_DIM
RET_HEADS = 4
RET_DK = 128
RET_WIDTH = RET_HEADS * RET_DK
FOURIER_GROUPS = 4
FOURIER_GROUP_DIM = 128
FOURIER_WIDTH = FOURIER_GROUPS * FOURIER_GROUP_DIM
IN_WIDTH = ATTN_WIDTH + 2 * KV_WIDTH + 4 * RET_WIDTH + FOURIER_WIDTH
ROPE_THETA = 10000.0

COL_AK = ATTN_WIDTH // 128
COL_AV = COL_AK + KV_WIDTH // 128
COL_RQ = COL_AV + KV_WIDTH // 128
COL_RK = COL_RQ + RET_WIDTH // 128
COL_RV = COL_RK + RET_WIDTH // 128
COL_RG = COL_RV + RET_WIDTH // 128
COL_FZ = COL_RG + RET_WIDTH // 128

IN_TILE = ATTN_WIDTH

ATTN_Q_SCALE = HEAD_DIM ** -0.5 * float(np.log2(np.e))

V7X_VMEM_LIMIT_BYTES = 56 * 1024 * 1024
V7X_FFN_VMEM_LIMIT_BYTES = 60 * 1024 * 1024


def _params(*sem, vmem_limit_bytes=V7X_VMEM_LIMIT_BYTES):
    return pltpu.CompilerParams(dimension_semantics=sem, vmem_limit_bytes=vmem_limit_bytes)


def _tile(n, prefs):
    for p in prefs:
        if n % p == 0:
            return p
    return n


def _dot(a, b):
    return jnp.dot(a, b, preferred_element_type=F32)


def _dot_nt(a, b):
    return lax.dot_general(a, b, (((1,), (1,)), ((), ())), preferred_element_type=F32)


def _rms(x):
    return x * lax.rsqrt(jnp.mean(x * x, axis=-1, keepdims=True) + EPS)


def _silu(x):
    return x * jax.nn.sigmoid(x)


def _adaln_kernel(c_ref, w_ref, b_ref, o_ref):
    a = _silu(c_ref[...]).astype(BF16)
    o_ref[...] = _dot(a, w_ref[...].astype(BF16)) + b_ref[...]


def _adaln(c_all, w_ada, b_ada):
    depth, d, nd = w_ada.shape
    r = c_all.shape[0]
    tn = _tile(nd, (1024, 512, 256, 128))
    return pl.pallas_call(
        _adaln_kernel,
        grid=(depth, nd // tn),
        in_specs=[
            pl.BlockSpec((r, d), lambda l, j: (0, 0)),
            pl.BlockSpec((None, d, tn), lambda l, j: (l, 0, j)),
            pl.BlockSpec((None, 1, tn), lambda l, j: (l, 0, j)),
        ],
        out_specs=pl.BlockSpec((None, r, tn), lambda l, j: (l, 0, j)),
        out_shape=jax.ShapeDtypeStruct((depth, r, nd), F32),
        name="adaln",
        compiler_params=_params("parallel", "parallel"),
    )(c_all, w_ada, b_ada.reshape(depth, 1, nd))


def _mod_spec(d, row_of_tile, k):
    return pl.BlockSpec((None, 1, d), lambda i, *_: (row_of_tile(i), 0, k))


FFN_TILE = 512
FFN_ROWS = 1024
ACT_CHUNK = 32
ROW_CHUNK = 16
ROW_UNROLL = 8


def _row_chunks(tm, dst_ref, body, read_first=False):
    def trip(t, carry):
        rows = [pl.ds(pl.multiple_of((t * ROW_UNROLL + k) * ROW_CHUNK, ROW_CHUNK), ROW_CHUNK)
                for k in range(ROW_UNROLL)]
        if read_first:
            for r, v in zip(rows, [body(r) for r in rows]):
                dst_ref[r, :] = v
        else:
            for r in rows:
                dst_ref[r, :] = body(r)
        return carry

    lax.fori_loop(0, tm // (ROW_CHUNK * ROW_UNROLL), trip, 0)


def _ffn_kernel(h_ref, sh_ref, sc_ref, gt_ref, gain_ref, wg_ref, wu_ref, wd_ref, *rest, nf, final):
    if final:
        fg_ref, o_ref, xn_ref, a_ref = rest
    else:
        o_ref, xn_ref, a_ref = rest
    acc_ref = o_ref
    f = pl.program_id(1)
    tm = h_ref.shape[0]

    @pl.when(f == 0)
    def _():
        gs, sh = gain_ref[...] * (1.0 + sc_ref[...]), sh_ref[...]

        _row_chunks(tm, xn_ref, lambda rows: (_rms(h_ref[rows, :]) * gs + sh).astype(BF16))

    @pl.when((pl.program_id(0) < 2) & (f == 0))
    def _():
        acc_ref[...] = jnp.zeros_like(acc_ref)

    xn = xn_ref[...]
    g = _dot(xn, wg_ref[...])
    u = _dot(xn, wu_ref[...])
    for r in range(0, tm, ACT_CHUNK):
        rows = slice(r, r + ACT_CHUNK)
        a_ref[rows, :] = (_silu(g[rows, :]) * u[rows, :]).astype(BF16)
    acc_ref[...] = jnp.where(f == 0, 0.0, acc_ref[...]) + _dot(a_ref[...], wd_ref[...])

    @pl.when(f == nf - 1)
    def _():
        half_gate = 0.5 * gt_ref[...]
        if final:
            fg = fg_ref[...]

        def chunk(rows):
            y = h_ref[rows, :] + half_gate * acc_ref[rows, :]
            return _rms(y) * fg if final else y

        _row_chunks(tm, o_ref, chunk, read_first=final)


def _ffn(h, mod, row_of_tile, tm, k0, gain, wg, wu, wd, layer, final_gain=None):
    n, d = h.shape
    tf = _tile(wg.shape[-1], (FFN_TILE, 256, 128))
    nf = wg.shape[-1] // tf
    final = final_gain is not None
    vec_spec = pl.BlockSpec((None, 1, d), lambda i, f: (layer, 0, 0))
    in_specs = [
        pl.BlockSpec((tm, d), lambda i, f: (i, 0)),
        _mod_spec(d, row_of_tile, k0),
        _mod_spec(d, row_of_tile, k0 + 1),
        _mod_spec(d, row_of_tile, k0 + 2),
        vec_spec,
        pl.BlockSpec((None, d, tf), lambda i, f: (layer, 0, f)),
        pl.BlockSpec((None, d, tf), lambda i, f: (layer, 0, f)),
        pl.BlockSpec((None, tf, d), lambda i, f: (layer, f, 0)),
    ]
    args = [h, mod, mod, mod, gain, wg, wu, wd]
    if final:
        in_specs.append(pl.BlockSpec((1, d), lambda i, f: (0, 0)))
        args.append(final_gain)
    return pl.pallas_call(
        functools.partial(_ffn_kernel, nf=nf, final=final),
        grid=(n // tm, nf),
        in_specs=in_specs,
        out_specs=pl.BlockSpec((tm, d), lambda i, f: (i, 0)),
        out_shape=jax.ShapeDtypeStruct((n, d), F32),
        scratch_shapes=[pltpu.VMEM((tm, d), BF16), pltpu.VMEM((tm, tf), BF16)],
        name="ffn",
        compiler_params=_params("arbitrary", "arbitrary", vmem_limit_bytes=V7X_FFN_VMEM_LIMIT_BYTES),
    )(*args)


def _mixnorm_kernel(h_ref, sh_ref, sc_ref, gain_ref, o_ref):
    gs, sh = gain_ref[...] * (1.0 + sc_ref[...]), sh_ref[...]

    _row_chunks(h_ref.shape[0], o_ref, lambda rows: (_rms(h_ref[rows, :]) * gs + sh).astype(BF16))


def _mixnorm(h, mod, row_of_tile, tm, gain, layer):
    n, d = h.shape
    return pl.pallas_call(
        _mixnorm_kernel,
        grid=(n // tm,),
        in_specs=[
            pl.BlockSpec((tm, d), lambda i: (i, 0)),
            _mod_spec(d, row_of_tile, 3),
            _mod_spec(d, row_of_tile, 4),
            pl.BlockSpec((None, 1, d), lambda i: (layer, 0, 0)),
        ],
        out_specs=pl.BlockSpec((tm, d), lambda i: (i, 0)),
        out_shape=jax.ShapeDtypeStruct((n, d), BF16),
        name="mixnorm",
        compiler_params=_params("parallel"),
    )(h, mod, mod, gain)


def _inproj_kernel(u_ref, w_ref, o_ref):
    o_ref[...] = _dot(u_ref[...], w_ref[...]).astype(BF16)


def _inproj(u, w_in, layer):
    n, d = u.shape
    tn = IN_TILE
    assert w_in.shape[-1] == IN_WIDTH and IN_WIDTH % tn == 0
    tm = _tile(n, (1024, 512, 256, 128))
    return pl.pallas_call(
        _inproj_kernel,
        grid=(n // tm, IN_WIDTH // tn),
        in_specs=[
            pl.BlockSpec((tm, d), lambda i, j: (i, 0)),
            pl.BlockSpec((None, d, tn), lambda i, j: (layer, 0, j)),
        ],
        out_specs=pl.BlockSpec((tm, tn), lambda i, j: (i, j)),
        out_shape=jax.ShapeDtypeStruct((n, IN_WIDTH), BF16),
        name="inproj",
        compiler_params=_params("parallel", "parallel"),
    )(u, w_in)


def _rope_tables(seq):
    t = jnp.arange(seq)
    row = (t // GRID_W).astype(F32)
    col = (t % GRID_W).astype(F32)
    inv = ROPE_THETA ** (-jnp.arange(0, HALF_ROT, 2, dtype=F32) / HALF_ROT)
    ar, ac = row[:, None] * inv[None], col[:, None] * inv[None]
    zero = jnp.zeros_like(ar)
    cos = jnp.concatenate([jnp.cos(ar), jnp.cos(ar), jnp.cos(ac), jnp.cos(ac)], axis=-1)
    sin_lo = jnp.concatenate([-jnp.sin(ar), zero, -jnp.sin(ac), zero], axis=-1)
    sin_hi = jnp.concatenate([zero, jnp.sin(ar), zero, jnp.sin(ac)], axis=-1)
    return cos, sin_lo, sin_hi


def _rope(x, cos, sin_lo, sin_hi):
    return x * cos + pltpu.roll(x, HEAD_DIM - HALF_ROT // 2, 1) * sin_lo + pltpu.roll(x, HALF_ROT // 2, 1) * sin_hi


def _attn_kernel(q_ref, kc_ref, vc_ref, qn_ref, kn_ref, *rest, has_lat):
    if has_lat:
        kl_ref, vl_ref, qcos_ref, qlo_ref, qhi_ref, kcos_ref, klo_ref, khi_ref, o_ref, k_ref, v_ref = rest
    else:
        o_ref, k_ref, v_ref = rest
    nc = kc_ref.shape[0]

    @pl.when(pl.program_id(1) == 0)
    def _():
        kn = kn_ref[...]
        for kv in range(ATTN_KV_HEADS):
            sl = slice(kv * HEAD_DIM, (kv + 1) * HEAD_DIM)
            k_ref[kv, :nc, :] = (_rms(kc_ref[:, sl].astype(F32)) * kn).astype(BF16)
            v_ref[kv, :nc, :HEAD_DIM] = vc_ref[:, sl]
            if has_lat:
                kl = _rms(kl_ref[:, sl].astype(F32)) * kn
                k_ref[kv, nc:, :] = _rope(kl, kcos_ref[...], klo_ref[...], khi_ref[...]).astype(BF16)
                v_ref[kv, nc:, :HEAD_DIM] = vl_ref[:, sl]
            v_ref[kv, :, HEAD_DIM:] = jnp.ones((v_ref.shape[1], HEAD_DIM), BF16)

    qn = qn_ref[...]
    for hd in range(ATTN_HEADS):
        sl = slice(hd * HEAD_DIM, (hd + 1) * HEAD_DIM)
        q = _rms(q_ref[:, sl].astype(F32)) * qn
        if has_lat:
            q = _rope(q, qcos_ref[...], qlo_ref[...], qhi_ref[...])
        s = _dot_nt((q * ATTN_Q_SCALE).astype(BF16), k_ref[hd // ATTN_GROUP])
        p = jnp.exp2(s - jnp.max(s, axis=-1, keepdims=True)).astype(BF16)
        ov = _dot(p, v_ref[hd // ATTN_GROUP])
        o_ref[:, sl] = (ov[:, :HEAD_DIM] / ov[:, HEAD_DIM:]).astype(BF16)


def _attention(proj_q, proj_c, q_norm, k_norm, layer, batch, seq, ctx_len, proj_kv=None, tables=None):
    n = proj_q.shape[0]
    tq = _tile(seq, (256, 128))
    nq = seq // tq
    has_lat = proj_kv is not None
    n_keys = ctx_len + (seq if has_lat else 0)
    col_k, col_v = COL_AK * 128 // KV_WIDTH, COL_AV * 128 // KV_WIDTH
    norm_spec = pl.BlockSpec((None, 1, HEAD_DIM), lambda b, i: (layer, 0, 0))
    in_specs = [
        pl.BlockSpec((tq, ATTN_WIDTH), lambda b, i: (b * nq + i, 0)),
        pl.BlockSpec((ctx_len, KV_WIDTH), lambda b, i: (b, col_k)),
        pl.BlockSpec((ctx_len, KV_WIDTH), lambda b, i: (b, col_v)),
        norm_spec,
        norm_spec,
    ]
    args = [proj_q, proj_c, proj_c, q_norm, k_norm]
    if has_lat:
        in_specs += [
            pl.BlockSpec((seq, KV_WIDTH), lambda b, i: (b, col_k)),
            pl.BlockSpec((seq, KV_WIDTH), lambda b, i: (b, col_v)),
        ]
        in_specs += [pl.BlockSpec((tq, HEAD_DIM), lambda b, i: (i, 0))] * 3
        in_specs += [pl.BlockSpec((seq, HEAD_DIM), lambda b, i: (0, 0))] * 3
        args += [proj_kv, proj_kv, *tables, *tables]
    return pl.pallas_call(
        functools.partial(_attn_kernel, has_lat=has_lat),
        grid=(batch, nq),
        in_specs=in_specs,
        out_specs=pl.BlockSpec((tq, ATTN_WIDTH), lambda b, i: (b * nq + i, 0)),
        out_shape=jax.ShapeDtypeStruct((n, ATTN_WIDTH), BF16),
        scratch_shapes=[
            pltpu.VMEM((ATTN_KV_HEADS, n_keys, HEAD_DIM), BF16),
            pltpu.VMEM((ATTN_KV_HEADS, n_keys, 2 * HEAD_DIM), BF16),
        ],
        name="attention",
        compiler_params=_params("parallel", "arbitrary"),
    )(*args)


def _ret_kernel(lg_ref, q_ref, k_ref, v_ref, g_ref, *rest, seq, ctx_len, chunk, has_ctx):
    if has_ctx:
        kc_ref, vc_ref, o_ref, a_ref, s_ref = rest
    else:
        o_ref, a_ref, s_ref = rest
    hd = pl.program_id(1)
    lgf, lgb = lg_ref[0, hd], lg_ref[1, hd]
    c, dk = chunk, RET_DK
    n = seq // c
    scale = dk ** -0.5
    a = lax.broadcasted_iota(jnp.int32, (c, 1), 0).astype(F32)
    wq_f, wq_b = jnp.exp(lgf * (a + 1.0)), jnp.exp(lgb * (c - a))
    wk_f, wk_b = jnp.exp(lgf * (c - 1.0 - a)) * scale, jnp.exp(lgb * a) * scale
    d = (lax.broadcasted_iota(jnp.int32, (c, c), 0) - lax.broadcasted_iota(jnp.int32, (c, c), 1)).astype(F32)
    dmask = jnp.exp(jnp.where(d >= 0.0, lgf * d, -lgb * d)) * jnp.where(d == 0.0, 2.0 * scale, scale)
    zero = jnp.zeros((1, dk), F32)
    gf, gb = jnp.exp(zero + lgf * c), jnp.exp(zero + lgb * c)

    def kv_outer(kr, vr, j):
        k = kr[j * c:(j + 1) * c, :]
        kk = jnp.concatenate([k * wk_f, k * wk_b], axis=1)
        return _dot(kk.T.astype(BF16), vr[j * c:(j + 1) * c, :])

    sf = jnp.zeros((dk, dk), F32)
    sb = jnp.zeros((dk, dk), F32)
    if has_ctx:
        outer = [kv_outer(kc_ref, vc_ref, j) for j in range(ctx_len // c)]
        for o in outer:
            sf = gf * sf + o[:dk]
        for o in reversed(outer):
            sb = gb * sb + o[dk:]
    for i in range(n):
        a_ref[i] = kv_outer(k_ref, v_ref, i)
    for i in range(n):
        s_ref[i, :dk, :] = sf.astype(BF16)
        sf = gf * sf + a_ref[i, :dk, :]
    for i in reversed(range(n)):
        s_ref[i, dk:, :] = sb.astype(BF16)
        sb = gb * sb + a_ref[i, dk:, :]
    for i in range(n):
        rows = slice(i * c, (i + 1) * c)
        q = q_ref[rows, :]
        s = _dot_nt(q, k_ref[rows, :])
        o = _dot((s * dmask).astype(BF16), v_ref[rows, :])
        qq = jnp.concatenate([q * wq_f, q * wq_b], axis=1).astype(BF16)
        o = o + _dot(qq, s_ref[i])
        o_ref[rows, :] = (_silu(g_ref[rows, :].astype(F32)) * _rms(o)).astype(BF16)


def _retention(proj, log_gamma, batch, seq, proj_ctx=None, ctx_len=0):
    n = proj.shape[0]
    has_ctx = proj_ctx is not None
    chunk = 256 if seq % 256 == 0 and ctx_len % 256 == 0 else 128
    in_specs = [
        pl.BlockSpec(memory_space=pltpu.SMEM),
        pl.BlockSpec((seq, RET_DK), lambda b, h: (b, COL_RQ + h)),
        pl.BlockSpec((seq, RET_DK), lambda b, h: (b, COL_RK + h)),
        pl.BlockSpec((seq, RET_DK), lambda b, h: (b, COL_RV + h)),
        pl.BlockSpec((seq, RET_DK), lambda b, h: (b, COL_RG + h)),
    ]
    args = [log_gamma, proj, proj, proj, proj]
    if has_ctx:
        in_specs += [
            pl.BlockSpec((ctx_len, RET_DK), lambda b, h: (b, COL_RK + h)),
            pl.BlockSpec((ctx_len, RET_DK), lambda b, h: (b, COL_RV + h)),
        ]
        args += [proj_ctx, proj_ctx]
    return pl.pallas_call(
        functools.partial(_ret_kernel, seq=seq, ctx_len=ctx_len, chunk=chunk, has_ctx=has_ctx),
        grid=(batch, RET_HEADS),
        in_specs=in_specs,
        out_specs=pl.BlockSpec((seq, RET_DK), lambda b, h: (b, h)),
        out_shape=jax.ShapeDtypeStruct((n, RET_WIDTH), BF16),
        scratch_shapes=[
            pltpu.VMEM((seq // chunk, 2 * RET_DK, RET_DK), F32),
            pltpu.VMEM((seq // chunk, 2 * RET_DK, RET_DK), BF16),
        ],
        name="retention",
        compiler_params=_params("parallel", "parallel"),
    )(*args)


def _dft_cos_sin(n):
    k = np.arange(n, dtype=np.int64)
    ang = 2.0 * np.pi * ((k[:, None] * k[None, :]) % n).astype(np.float64) / n
    return np.cos(ang), np.sin(ang)


def _fourier_consts(seq):
    cg, sg = _dft_cos_sin(FOURIER_GROUP_DIM)
    norm = 1.0 / np.sqrt(float(seq) * FOURIER_GROUP_DIM)
    eye = np.eye(FOURIER_GROUPS)
    chan = np.concatenate([np.kron(eye, cg), np.kron(eye, -sg)], axis=1) * norm
    cs, ss = _dft_cos_sin(seq)
    as_bf16 = lambda a: jnp.asarray(a.astype(np.float32)).astype(BF16)
    return as_bf16(chan), as_bf16(cs), as_bf16(ss)


def _fchan_kernel(z_ref, m_ref, o_ref):
    o_ref[...] = _dot(z_ref[...], m_ref[...]).astype(BF16)


def _fseq_kernel(c_ref, s_ref, zc_ref, zs_ref, o_ref):
    o_ref[...] = (_dot(c_ref[...], zc_ref[...]) + _dot(s_ref[...], zs_ref[...])).astype(BF16)


def _fourier(proj, batch, seq, consts):
    n = proj.shape[0]
    chan, cs, ss = consts
    fw = FOURIER_WIDTH
    tm = _tile(n, (1024, 512, 256, 128))
    zcs = pl.pallas_call(
        _fchan_kernel,
        grid=(n // tm,),
        in_specs=[
            pl.BlockSpec((tm, fw), lambda i: (i, COL_FZ * 128 // fw)),
            pl.BlockSpec((fw, 2 * fw), lambda i: (0, 0)),
        ],
        out_specs=pl.BlockSpec((tm, 2 * fw), lambda i: (i, 0)),
        out_shape=jax.ShapeDtypeStruct((n, 2 * fw), BF16),
        name="fourier_chan",
        compiler_params=_params("parallel"),
    )(proj, chan)
    tk = _tile(seq, (512, 256, 128))
    nk = seq // tk
    return pl.pallas_call(
        _fseq_kernel,
        grid=(nk, batch),
        in_specs=[
            pl.BlockSpec((tk, seq), lambda k, b: (k, 0)),
            pl.BlockSpec((tk, seq), lambda k, b: (k, 0)),
            pl.BlockSpec((seq, fw), lambda k, b: (b, 0)),
            pl.BlockSpec((seq, fw), lambda k, b: (b, 1)),
        ],
        out_specs=pl.BlockSpec((tk, fw), lambda k, b: (b * nk + k, 0)),
        out_shape=jax.ShapeDtypeStruct((n, fw), BF16),
        name="fourier_seq",
        compiler_params=_params("parallel", "parallel"),
    )(cs, ss, zcs, zcs)


MERGE_TILE = 512


def _gate_kernel(u_ref, yf_ref, ya_ref, yr_ref, wf_ref, wa_ref, wr_ref,
                 gf_ref, ga_ref, gr_ref, bf_ref, ba_ref, br_ref, o_ref):
    u = u_ref[...]
    gates = [_dot(u, g[...]) for g in (gf_ref, ga_ref, gr_ref)]
    branches = [_dot(y[...], w[...]) for y, w in ((yf_ref, wf_ref), (ya_ref, wa_ref), (yr_ref, wr_ref))]
    biases = [b[...] for b in (bf_ref, ba_ref, br_ref)]
    for r in range(0, o_ref.shape[0], ACT_CHUNK):
        rows = slice(r, r + ACT_CHUNK)
        terms = [jax.nn.sigmoid(g[rows, :] + b) * y[rows, :] for g, b, y in zip(gates, biases, branches)]
        o_ref[rows, :] = (terms[0] + terms[1] + terms[2]).astype(BF16)


def _out_kernel(h_ref, gt_ref, m_ref, wo_ref, o_ref, *, tn):
    m = m_ref[...]
    for c in range(0, o_ref.shape[1], tn):
        cols = slice(c, c + tn)
        o_ref[:, cols] = h_ref[:, cols] + gt_ref[:, cols] * _dot(m, wo_ref[:, cols])


def _merge(h, mod, row_of_tile, tm, u, y_f, y_a, y_r, w_bf, w_ba, w_br, w_mg, b_mg, w_out, layer):
    n, d = h.shape
    tn = _tile(d, (MERGE_TILE, 256, 128))
    nn = d // tn
    tg = _tile(n, (1024, 512, 256, 128))
    wspec = lambda rows: pl.BlockSpec((None, rows, tn), lambda i, j: (layer, 0, j))
    gspec = lambda k: pl.BlockSpec((None, d, tn), lambda i, j: (layer, 0, k * nn + j))
    bspec = lambda k: pl.BlockSpec((None, 1, tn), lambda i, j: (layer, 0, k * nn + j))
    yspec = lambda w: pl.BlockSpec((tg, w), lambda i, j: (i, 0))
    m = pl.pallas_call(
        _gate_kernel,
        grid=(n // tg, nn),
        in_specs=[
            yspec(d), yspec(FOURIER_WIDTH), yspec(ATTN_WIDTH), yspec(RET_WIDTH),
            wspec(FOURIER_WIDTH), wspec(ATTN_WIDTH), wspec(RET_WIDTH),
            gspec(0), gspec(1), gspec(2),
            bspec(0), bspec(1), bspec(2),
        ],
        out_specs=pl.BlockSpec((tg, tn), lambda i, j: (i, j)),
        out_shape=jax.ShapeDtypeStruct((n, d), BF16),
        name="merge_gate",
        compiler_params=_params("parallel", "parallel"),
    )(u, y_f, y_a, y_r, w_bf, w_ba, w_br, w_mg, w_mg, w_mg, b_mg, b_mg, b_mg)
    return pl.pallas_call(
        functools.partial(_out_kernel, tn=tn),
        grid=(n // tm,),
        in_specs=[
            pl.BlockSpec((tm, d), lambda i: (i, 0)),
            _mod_spec(d, row_of_tile, 5),
            pl.BlockSpec((tm, d), lambda i: (i, 0)),
            pl.BlockSpec((None, d, d), lambda i: (layer, 0, 0)),
        ],
        out_specs=pl.BlockSpec((tm, d), lambda i: (i, 0)),
        out_shape=jax.ShapeDtypeStruct((n, d), F32),
        name="merge_out",
        compiler_params=_params("parallel"),
    )(h, mod, m, w_out)


def kernel(x, c, ctx, c_ctx, w_ada, b_ada, ffn1_norm, ffn1_w_gate, ffn1_w_up, ffn1_w_down, mix_norm, w_in, q_norm, k_norm, ret_decay, w_branch_fourier, w_branch_attn, w_branch_ret, w_merge_gate, b_merge_gate, w_out, ffn2_norm, ffn2_w_gate, ffn2_w_up, ffn2_w_down, final_norm):
    batch, seq, d = x.shape
    ctx_len = ctx.shape[1]
    depth = w_ada.shape[0]
    vec = lambda g: g.reshape(depth, 1, g.shape[-1])

    rows = -(-(batch + 1) // 16) * 16
    c_all = jnp.concatenate([c, c_ctx[None], jnp.zeros((rows - batch - 1, d), F32)], axis=0)
    mod = _adaln(c_all, w_ada, b_ada).reshape(depth * rows, 1, N_MOD * d)

    tm = _tile(seq, (512, 256, 128))
    tmc = _tile(batch * ctx_len, (512, 256, 128))
    tm_ffn = _tile(seq, (FFN_ROWS, 512, 256, 128))
    tmc_ffn = _tile(batch * ctx_len, (FFN_ROWS, 512, 256, 128))

    bf = lambda w: w.astype(BF16)
    ffn1 = (vec(ffn1_norm), bf(ffn1_w_gate), bf(ffn1_w_up), bf(ffn1_w_down))
    ffn2 = (vec(ffn2_norm), bf(ffn2_w_gate), bf(ffn2_w_up), bf(ffn2_w_down))
    mix_gain, w_in_t = vec(mix_norm), bf(w_in)
    qn, kn = vec(q_norm), vec(k_norm)
    w_bf, w_ba, w_br = bf(w_branch_fourier), bf(w_branch_attn), bf(w_branch_ret)
    w_mg, b_mg, w_o = bf(w_merge_gate), vec(b_merge_gate), bf(w_out)
    log_gamma = -jnp.exp(ret_decay.astype(F32))
    tables = _rope_tables(seq)
    four_lat = _fourier_consts(seq)
    four_ctx = _fourier_consts(ctx_len)

    h = x.reshape(batch * seq, d)
    hc = ctx.reshape(batch * ctx_len, d)
    for l in range(depth):
        need_ctx = l < depth - 1
        lat_rows = lambda t, l=l: (lambda i: l * rows + i // (seq // t))
        lat_row = lat_rows(tm)
        ctx_row = lambda i, l=l: l * rows + batch

        h = _ffn(h, mod, lat_rows(tm_ffn), tm_ffn, 0, *ffn1, l)
        hc = _ffn(hc, mod, ctx_row, tmc_ffn, 0, *ffn1, l)
        u = _mixnorm(h, mod, lat_row, tm, mix_gain, l)
        uc = _mixnorm(hc, mod, ctx_row, tmc, mix_gain, l)

        proj = _inproj(u, w_in_t, l)
        proj_c = _inproj(uc, w_in_t, l)

        y_a = _attention(proj, proj_c, qn, kn, l, batch, seq, ctx_len, proj, tables)
        y_r = _retention(proj, log_gamma[l], batch, seq, proj_c, ctx_len)
        y_f = _fourier(proj, batch, seq, four_lat)
        h = _merge(h, mod, lat_row, tm, u, y_f, y_a, y_r, w_bf, w_ba, w_br, w_mg, b_mg, w_o, l)
        last = final_norm.reshape(1, d) if l == depth - 1 else None
        h = _ffn(h, mod, lat_rows(tm_ffn), tm_ffn, 6, *ffn2, l, final_gain=last)

        if need_ctx:
            yc_a = _attention(proj_c, proj_c, qn, kn, l, batch, ctx_len, ctx_len)
            yc_r = _retention(proj_c, log_gamma[l], batch, ctx_len)
            yc_f = _fourier(proj_c, batch, ctx_len, four_ctx)
            hc = _merge(hc, mod, ctx_row, tmc, uc, yc_f, yc_a, yc_r, w_bf, w_ba, w_br, w_mg, b_mg, w_o, l)
            hc = _ffn(hc, mod, ctx_row, tmc_ffn, 6, *ffn2, l)
    return h.reshape(batch, seq, d)
```

```python
import functools

import numpy as np
import jax
import jax.numpy as jnp
from jax import lax
from jax.experimental import pallas as pl
from jax.experimental.pallas import tpu as pltpu

F32 = jnp.float32
BF16 = jnp.bfloat16

EPS = 1e-6
N_MOD = 9
GRID_W = 64
HEAD_DIM = 128
HALF_ROT = HEAD_DIM // 2
ATTN_HEADS = 8
ATTN_KV_HEADS = 2
ATTN_GROUP = ATTN_HEADS // ATTN_KV_HEADS
ATTN_WIDTH = ATTN_HEADS * HEAD_DIM
KV_WIDTH = ATTN_KV_HEADS * HEAD_DIM
RET_HEADS = 4
RET_DK = 128
RET_WIDTH = RET_HEADS * RET_DK
FOURIER_GROUPS = 4
FOURIER_GROUP_DIM = 128
FOURIER_WIDTH = FOURIER_GROUPS * FOURIER_GROUP_DIM
IN_WIDTH = ATTN_WIDTH + 2 * KV_WIDTH + 4 * RET_WIDTH + FOURIER_WIDTH
ROPE_THETA = 10000.0

COL_AK = ATTN_WIDTH // 128
COL_AV = COL_AK + KV_WIDTH // 128
COL_RQ = COL_AV + KV_WIDTH // 128
COL_RK = COL_RQ + RET_WIDTH // 128
COL_RV = COL_RK + RET_WIDTH // 128
COL_RG = COL_RV + RET_WIDTH // 128
COL_FZ = COL_RG + RET_WIDTH // 128

IN_TILE = ATTN_WIDTH

ATTN_Q_SCALE = HEAD_DIM ** -0.5 * float(np.log2(np.e))

V7X_VMEM_LIMIT_BYTES = 56 * 1024 * 1024
V7X_FFN_VMEM_LIMIT_BYTES = 60 * 1024 * 1024


def _params(*sem, vmem_limit_bytes=V7X_VMEM_LIMIT_BYTES):
    return pltpu.CompilerParams(dimension_semantics=sem, vmem_limit_bytes=vmem_limit_bytes)


def _tile(n, prefs):
    for p in prefs:
        if n % p == 0:
            return p
    return n


def _dot(a, b):
    return jnp.dot(a, b, preferred_element_type=F32)


def _dot_nt(a, b):
    return lax.dot_general(a, b, (((1,), (1,)), ((), ())), preferred_element_type=F32)


def _rms(x):
    return x * lax.rsqrt(jnp.mean(x * x, axis=-1, keepdims=True) + EPS)


def _silu(x):
    return x * jax.nn.sigmoid(x)


def _adaln_kernel(c_ref, w_ref, b_ref, o_ref):
    a = _silu(c_ref[...]).astype(BF16)
    o_ref[...] = _dot(a, w_ref[...].astype(BF16)) + b_ref[...]


def _adaln(c_all, w_ada, b_ada):
    depth, d, nd = w_ada.shape
    r = c_all.shape[0]
    tn = _tile(nd, (1024, 512, 256, 128))
    return pl.pallas_call(
        _adaln_kernel,
        grid=(depth, nd // tn),
        in_specs=[
            pl.BlockSpec((r, d), lambda l, j: (0, 0)),
            pl.BlockSpec((None, d, tn), lambda l, j: (l, 0, j)),
            pl.BlockSpec((None, 1, tn), lambda l, j: (l, 0, j)),
        ],
        out_specs=pl.BlockSpec((None, r, tn), lambda l, j: (l, 0, j)),
        out_shape=jax.ShapeDtypeStruct((depth, r, nd), F32),
        name="adaln",
        compiler_params=_params("parallel", "parallel"),
    )(c_all, w_ada, b_ada.reshape(depth, 1, nd))


def _mod_spec(d, row_of_tile, k):
    return pl.BlockSpec((None, 1, d), lambda i, *_: (row_of_tile(i), 0, k))


FFN_TILE = 512
FFN_ROWS = 1024
ACT_CHUNK = 32
ROW_CHUNK = 16
ROW_UNROLL = 8


def _row_chunks(tm, dst_ref, body, read_first=False):
    def trip(t, carry):
        rows = [pl.ds(pl.multiple_of((t * ROW_UNROLL + k) * ROW_CHUNK, ROW_CHUNK), ROW_CHUNK)
                for k in range(ROW_UNROLL)]
        if read_first:
            for r, v in zip(rows, [body(r) for r in rows]):
                dst_ref[r, :] = v
        else:
            for r in rows:
                dst_ref[r, :] = body(r)
        return carry

    lax.fori_loop(0, tm // (ROW_CHUNK * ROW_UNROLL), trip, 0)


def _ffn_kernel(h_ref, sh_ref, sc_ref, gt_ref, gain_ref, wg_ref, wu_ref, wd_ref, *rest, nf, final):
    if final:
        fg_ref, o_ref, xn_ref, a_ref = rest
    else:
        o_ref, xn_ref, a_ref = rest
    acc_ref = o_ref
    f = pl.program_id(1)
    tm = h_ref.shape[0]

    @pl.when(f == 0)
    def _():
        gs, sh = gain_ref[...] * (1.0 + sc_ref[...]), sh_ref[...]

        _row_chunks(tm, xn_ref, lambda rows: (_rms(h_ref[rows, :]) * gs + sh).astype(BF16))

    @pl.when((pl.program_id(0) < 2) & (f == 0))
    def _():
        acc_ref[...] = jnp.zeros_like(acc_ref)

    xn = xn_ref[...]
    g = _dot(xn, wg_ref[...])
    u = _dot(xn, wu_ref[...])
    for r in range(0, tm, ACT_CHUNK):
        rows = slice(r, r + ACT_CHUNK)
        a_ref[rows, :] = (_silu(g[rows, :]) * u[rows, :]).astype(BF16)
    acc_ref[...] = jnp.where(f == 0, 0.0, acc_ref[...]) + _dot(a_ref[...], wd_ref[...])

    @pl.when(f == nf - 1)
    def _():
        half_gate = 0.5 * gt_ref[...]
        if final:
            fg = fg_ref[...]

        def chunk(rows):
            y = h_ref[rows, :] + half_gate * acc_ref[rows, :]
            return _rms(y) * fg if final else y

        _row_chunks(tm, o_ref, chunk, read_first=final)


def _ffn(h, mod, row_of_tile, tm, k0, gain, wg, wu, wd, layer, final_gain=None):
    n, d = h.shape
    tf = _tile(wg.shape[-1], (FFN_TILE, 256, 128))
    nf = wg.shape[-1] // tf
    final = final_gain is not None
    vec_spec = pl.BlockSpec((None, 1, d), lambda i, f: (layer, 0, 0))
    in_specs = [
        pl.BlockSpec((tm, d), lambda i, f: (i, 0)),
        _mod_spec(d, row_of_tile, k0),
        _mod_spec(d, row_of_tile, k0 + 1),
        _mod_spec(d, row_of_tile, k0 + 2),
        vec_spec,
        pl.BlockSpec((None, d, tf), lambda i, f: (layer, 0, f)),
        pl.BlockSpec((None, d, tf), lambda i, f: (layer, 0, f)),
        pl.BlockSpec((None, tf, d), lambda i, f: (layer, f, 0)),
    ]
    args = [h, mod, mod, mod, gain, wg, wu, wd]
    if final:
        in_specs.append(pl.BlockSpec((1, d), lambda i, f: (0, 0)))
        args.append(final_gain)
    return pl.pallas_call(
        functools.partial(_ffn_kernel, nf=nf, final=final),
        grid=(n // tm, nf),
        in_specs=in_specs,
        out_specs=pl.BlockSpec((tm, d), lambda i, f: (i, 0)),
        out_shape=jax.ShapeDtypeStruct((n, d), F32),
        scratch_shapes=[pltpu.VMEM((tm, d), BF16), pltpu.VMEM((tm, tf), BF16)],
        name="ffn",
        compiler_params=_params("arbitrary", "arbitrary", vmem_limit_bytes=V7X_FFN_VMEM_LIMIT_BYTES),
    )(*args)


def _inproj_kernel(h_ref, sh_ref, sc_ref, gain_ref, w_ref, o_ref, u_ref):
    @pl.when(pl.program_id(1) == 0)
    def _():
        gs, sh = gain_ref[...] * (1.0 + sc_ref[...]), sh_ref[...]
        _row_chunks(h_ref.shape[0], u_ref, lambda rows: (_rms(h_ref[rows, :]) * gs + sh).astype(BF16))

    o_ref[...] = _dot(u_ref[...], w_ref[...]).astype(BF16)


def _inproj(h, mod, row_of_tile, tm, gain, w_in, layer):
    n, d = h.shape
    tn = IN_TILE
    assert w_in.shape[-1] == IN_WIDTH and IN_WIDTH % tn == 0
    return pl.pallas_call(
        _inproj_kernel,
        grid=(n // tm, IN_WIDTH // tn),
        in_specs=[
            pl.BlockSpec((tm, d), lambda i, j: (i, 0)),
            _mod_spec(d, row_of_tile, 3),
            _mod_spec(d, row_of_tile, 4),
            pl.BlockSpec((None, 1, d), lambda i, j: (layer, 0, 0)),
            pl.BlockSpec((None, d, tn), lambda i, j: (layer, 0, j)),
        ],
        out_specs=[
            pl.BlockSpec((tm, tn), lambda i, j: (i, j)),
            pl.BlockSpec((tm, d), lambda i, j: (i, 0)),
        ],
        out_shape=[
            jax.ShapeDtypeStruct((n, IN_WIDTH), BF16),
            jax.ShapeDtypeStruct((n, d), BF16),
        ],
        name="inproj",
        compiler_params=_params("parallel", "arbitrary"),
    )(h, mod, mod, gain, w_in)


def _rope_tables(seq):
    t = jnp.arange(seq)
    row = (t // GRID_W).astype(F32)
    col = (t % GRID_W).astype(F32)
    inv = ROPE_THETA ** (-jnp.arange(0, HALF_ROT, 2, dtype=F32) / HALF_ROT)
    ar, ac = row[:, None] * inv[None], col[:, None] * inv[None]
    zero = jnp.zeros_like(ar)
    cos = jnp.concatenate([jnp.cos(ar), jnp.cos(ar), jnp.cos(ac), jnp.cos(ac)], axis=-1)
    sin_lo = jnp.concatenate([-jnp.sin(ar), zero, -jnp.sin(ac), zero], axis=-1)
    sin_hi = jnp.concatenate([zero, jnp.sin(ar), zero, jnp.sin(ac)], axis=-1)
    return cos, sin_lo, sin_hi


def _rope(x, cos, sin_lo, sin_hi):
    return x * cos + pltpu.roll(x, HEAD_DIM - HALF_ROT // 2, 1) * sin_lo + pltpu.roll(x, HALF_ROT // 2, 1) * sin_hi


def _attn_kernel(q_ref, kc_ref, vc_ref, qn_ref, kn_ref, *rest, has_lat):
    if has_lat:
        kl_ref, vl_ref, qcos_ref, qlo_ref, qhi_ref, kcos_ref, klo_ref, khi_ref, o_ref, k_ref, v_ref = rest
    else:
        o_ref, k_ref, v_ref = rest
    nc = kc_ref.shape[0]

    @pl.when(pl.program_id(1) == 0)
    def _():
        kn = kn_ref[...]
        for kv in range(ATTN_KV_HEADS):
            sl = slice(kv * HEAD_DIM, (kv + 1) * HEAD_DIM)
            k_ref[kv, :nc, :] = (_rms(kc_ref[:, sl].astype(F32)) * kn).astype(BF16)
            v_ref[kv, :nc, :HEAD_DIM] = vc_ref[:, sl]
            if has_lat:
                kl = _rms(kl_ref[:, sl].astype(F32)) * kn
                k_ref[kv, nc:, :] = _rope(kl, kcos_ref[...], klo_ref[...], khi_ref[...]).astype(BF16)
                v_ref[kv, nc:, :HEAD_DIM] = vl_ref[:, sl]
            v_ref[kv, :, HEAD_DIM:] = jnp.ones((v_ref.shape[1], HEAD_DIM), BF16)

    qn = qn_ref[...]
    for hd in range(ATTN_HEADS):
        sl = slice(hd * HEAD_DIM, (hd + 1) * HEAD_DIM)
        q = _rms(q_ref[:, sl].astype(F32)) * qn
        if has_lat:
            q = _rope(q, qcos_ref[...], qlo_ref[...], qhi_ref[...])
        s = _dot_nt((q * ATTN_Q_SCALE).astype(BF16), k_ref[hd // ATTN_GROUP])
        p = jnp.exp2(s - jnp.max(s, axis=-1, keepdims=True)).astype(BF16)
        ov = _dot(p, v_ref[hd // ATTN_GROUP])
        o_ref[:, sl] = (ov[:, :HEAD_DIM] / ov[:, HEAD_DIM:]).astype(BF16)


def _attention(proj_q, proj_c, q_norm, k_norm, layer, batch, seq, ctx_len, proj_kv=None, tables=None):
    n = proj_q.shape[0]
    tq = _tile(seq, (256, 128))
    nq = seq // tq
    has_lat = proj_kv is not None
    n_keys = ctx_len + (seq if has_lat else 0)
    col_k, col_v = COL_AK * 128 // KV_WIDTH, COL_AV * 128 // KV_WIDTH
    norm_spec = pl.BlockSpec((None, 1, HEAD_DIM), lambda b, i: (layer, 0, 0))
    in_specs = [
        pl.BlockSpec((tq, ATTN_WIDTH), lambda b, i: (b * nq + i, 0)),
        pl.BlockSpec((ctx_len, KV_WIDTH), lambda b, i: (b, col_k)),
        pl.BlockSpec((ctx_len, KV_WIDTH), lambda b, i: (b, col_v)),
        norm_spec,
        norm_spec,
    ]
    args = [proj_q, proj_c, proj_c, q_norm, k_norm]
    if has_lat:
        in_specs += [
            pl.BlockSpec((seq, KV_WIDTH), lambda b, i: (b, col_k)),
            pl.BlockSpec((seq, KV_WIDTH), lambda b, i: (b, col_v)),
        ]
        in_specs += [pl.BlockSpec((tq, HEAD_DIM), lambda b, i: (i, 0))] * 3
        in_specs += [pl.BlockSpec((seq, HEAD_DIM), lambda b, i: (0, 0))] * 3
        args += [proj_kv, proj_kv, *tables, *tables]
    return pl.pallas_call(
        functools.partial(_attn_kernel, has_lat=has_lat),
        grid=(batch, nq),
        in_specs=in_specs,
        out_specs=pl.BlockSpec((tq, ATTN_WIDTH), lambda b, i: (b * nq + i, 0)),
        out_shape=jax.ShapeDtypeStruct((n, ATTN_WIDTH), BF16),
        scratch_shapes=[
            pltpu.VMEM((ATTN_KV_HEADS, n_keys, HEAD_DIM), BF16),
            pltpu.VMEM((ATTN_KV_HEADS, n_keys, 2 * HEAD_DIM), BF16),
        ],
        name="attention",
        compiler_params=_params("parallel", "arbitrary"),
    )(*args)


def _ret_kernel(lg_ref, q_ref, k_ref, v_ref, g_ref, *rest, seq, ctx_len, chunk, has_ctx):
    if has_ctx:
        kc_ref, vc_ref, o_ref, a_ref, s_ref = rest
    else:
        o_ref, a_ref, s_ref = rest
    hd = pl.program_id(1)
    lgf, lgb = lg_ref[0, hd], lg_ref[1, hd]
    c, dk = chunk, RET_DK
    n = seq // c
    scale = dk ** -0.5
    a = lax.broadcasted_iota(jnp.int32, (c, 1), 0).astype(F32)
    wq_f, wq_b = jnp.exp(lgf * (a + 1.0)), jnp.exp(lgb * (c - a))
    wk_f, wk_b = jnp.exp(lgf * (c - 1.0 - a)) * scale, jnp.exp(lgb * a) * scale
    d = (lax.broadcasted_iota(jnp.int32, (c, c), 0) - lax.broadcasted_iota(jnp.int32, (c, c), 1)).astype(F32)
    dmask = jnp.exp(jnp.where(d >= 0.0, lgf * d, -lgb * d)) * jnp.where(d == 0.0, 2.0 * scale, scale)
    zero = jnp.zeros((1, dk), F32)
    gf, gb = jnp.exp(zero + lgf * c), jnp.exp(zero + lgb * c)

    def kv_outer(kr, vr, j):
        k = kr[j * c:(j + 1) * c, :]
        kk = jnp.concatenate([k * wk_f, k * wk_b], axis=1)
        return _dot(kk.T.astype(BF16), vr[j * c:(j + 1) * c, :])

    sf = jnp.zeros((dk, dk), F32)
    sb = jnp.zeros((dk, dk), F32)
    if has_ctx:
        outer = [kv_outer(kc_ref, vc_ref, j) for j in range(ctx_len // c)]
        for o in outer:
            sf = gf * sf + o[:dk]
        for o in reversed(outer):
            sb = gb * sb + o[dk:]
    for i in range(n):
        a_ref[i] = kv_outer(k_ref, v_ref, i)
    for i in range(n):
        s_ref[i, :dk, :] = sf.astype(BF16)
        sf = gf * sf + a_ref[i, :dk, :]
    for i in reversed(range(n)):
        s_ref[i, dk:, :] = sb.astype(BF16)
        sb = gb * sb + a_ref[i, dk:, :]
    for i in range(n):
        rows = slice(i * c, (i + 1) * c)
        q = q_ref[rows, :]
        s = _dot_nt(q, k_ref[rows, :])
        o = _dot((s * dmask).astype(BF16), v_ref[rows, :])
        qq = jnp.concatenate([q * wq_f, q * wq_b], axis=1).astype(BF16)
        o = o + _dot(qq, s_ref[i])
        o_ref[rows, :] = (_silu(g_ref[rows, :].astype(F32)) * _rms(o)).astype(BF16)


def _retention(proj, log_gamma, batch, seq, proj_ctx=None, ctx_len=0):
    n = proj.shape[0]
    has_ctx = proj_ctx is not None
    chunk = 256 if seq % 256 == 0 and ctx_len % 256 == 0 else 128
    in_specs = [
        pl.BlockSpec(memory_space=pltpu.SMEM),
        pl.BlockSpec((seq, RET_DK), lambda b, h: (b, COL_RQ + h)),
        pl.BlockSpec((seq, RET_DK), lambda b, h: (b, COL_RK + h)),
        pl.BlockSpec((seq, RET_DK), lambda b, h: (b, COL_RV + h)),
        pl.BlockSpec((seq, RET_DK), lambda b, h: (b, COL_RG + h)),
    ]
    args = [log_gamma, proj, proj, proj, proj]
    if has_ctx:
        in_specs += [
            pl.BlockSpec((ctx_len, RET_DK), lambda b, h: (b, COL_RK + h)),
            pl.BlockSpec((ctx_len, RET_DK), lambda b, h: (b, COL_RV + h)),
        ]
        args += [proj_ctx, proj_ctx]
    return pl.pallas_call(
        functools.partial(_ret_kernel, seq=seq, ctx_len=ctx_len, chunk=chunk, has_ctx=has_ctx),
        grid=(batch, RET_HEADS),
        in_specs=in_specs,
        out_specs=pl.BlockSpec((seq, RET_DK), lambda b, h: (b, h)),
        out_shape=jax.ShapeDtypeStruct((n, RET_WIDTH), BF16),
        scratch_shapes=[
            pltpu.VMEM((seq // chunk, 2 * RET_DK, RET_DK), F32),
            pltpu.VMEM((seq // chunk, 2 * RET_DK, RET_DK), BF16),
        ],
        name="retention",
        compiler_params=_params("parallel", "parallel"),
    )(*args)


def _dft_cos_sin(n):
    k = np.arange(n, dtype=np.int64)
    ang = 2.0 * np.pi * ((k[:, None] * k[None, :]) % n).astype(np.float64) / n
    return np.cos(ang), np.sin(ang)


def _fourier_consts(seq):
    cg, sg = _dft_cos_sin(FOURIER_GROUP_DIM)
    norm = 1.0 / np.sqrt(float(seq) * FOURIER_GROUP_DIM)
    eye = np.eye(FOURIER_GROUPS)
    chan = np.concatenate([np.kron(eye, cg), np.kron(eye, -sg)], axis=1) * norm
    cs, ss = _dft_cos_sin(seq)
    as_bf16 = lambda a: jnp.asarray(a.astype(np.float32)).astype(BF16)
    return as_bf16(chan), as_bf16(cs), as_bf16(ss)


def _fchan_kernel(z_ref, m_ref, o_ref):
    o_ref[...] = _dot(z_ref[...], m_ref[...]).astype(BF16)


def _fseq_kernel(c_ref, s_ref, zc_ref, zs_ref, o_ref):
    o_ref[...] = (_dot(c_ref[...], zc_ref[...]) + _dot(s_ref[...], zs_ref[...])).astype(BF16)


def _fourier(proj, batch, seq, consts):
    n = proj.shape[0]
    chan, cs, ss = consts
    fw = FOURIER_WIDTH
    tm = _tile(n, (1024, 512, 256, 128))
    zcs = pl.pallas_call(
        _fchan_kernel,
        grid=(n // tm,),
        in_specs=[
            pl.BlockSpec((tm, fw), lambda i: (i, COL_FZ * 128 // fw)),
            pl.BlockSpec((fw, 2 * fw), lambda i: (0, 0)),
        ],
        out_specs=pl.BlockSpec((tm, 2 * fw), lambda i: (i, 0)),
        out_shape=jax.ShapeDtypeStruct((n, 2 * fw), BF16),
        name="fourier_chan",
        compiler_params=_params("parallel"),
    )(proj, chan)
    tk = _tile(seq, (512, 256, 128))
    nk = seq // tk
    return pl.pallas_call(
        _fseq_kernel,
        grid=(nk, batch),
        in_specs=[
            pl.BlockSpec((tk, seq), lambda k, b: (k, 0)),
            pl.BlockSpec((tk, seq), lambda k, b: (k, 0)),
            pl.BlockSpec((seq, fw), lambda k, b: (b, 0)),
            pl.BlockSpec((seq, fw), lambda k, b: (b, 1)),
        ],
        out_specs=pl.BlockSpec((tk, fw), lambda k, b: (b * nk + k, 0)),
        out_shape=jax.ShapeDtypeStruct((n, fw), BF16),
        name="fourier_seq",
        compiler_params=_params("parallel", "parallel"),
    )(cs, ss, zcs, zcs)


MERGE_TILE = 512


def _gate_kernel(u_ref, yf_ref, ya_ref, yr_ref, wf_ref, wa_ref, wr_ref,
                 gf_ref, ga_ref, gr_ref, bf_ref, ba_ref, br_ref, o_ref):
    u = u_ref[...]
    gates = [_dot(u, g[...]) for g in (gf_ref, ga_ref, gr_ref)]
    branches = [_dot(y[...], w[...]) for y, w in ((yf_ref, wf_ref), (ya_ref, wa_ref), (yr_ref, wr_ref))]
    biases = [b[...] for b in (bf_ref, ba_ref, br_ref)]
    for r in range(0, o_ref.shape[0], ACT_CHUNK):
        rows = slice(r, r + ACT_CHUNK)
        terms = [jax.nn.sigmoid(g[rows, :] + b) * y[rows, :] for g, b, y in zip(gates, biases, branches)]
        o_ref[rows, :] = (terms[0] + terms[1] + terms[2]).astype(BF16)


def _out_kernel(h_ref, gt_ref, m_ref, wo_ref, o_ref, *, tn):
    m = m_ref[...]
    for c in range(0, o_ref.shape[1], tn):
        cols = slice(c, c + tn)
        o_ref[:, cols] = h_ref[:, cols] + gt_ref[:, cols] * _dot(m, wo_ref[:, cols])


def _merge(h, mod, row_of_tile, tm, u, y_f, y_a, y_r, w_bf, w_ba, w_br, w_mg, b_mg, w_out, layer):
    n, d = h.shape
    tn = _tile(d, (MERGE_TILE, 256, 128))
    nn = d // tn
    tg = _tile(n, (1024, 512, 256, 128))
    wspec = lambda rows: pl.BlockSpec((None, rows, tn), lambda i, j: (layer, 0, j))
    gspec = lambda k: pl.BlockSpec((None, d, tn), lambda i, j: (layer, 0, k * nn + j))
    bspec = lambda k: pl.BlockSpec((None, 1, tn), lambda i, j: (layer, 0, k * nn + j))
    yspec = lambda w: pl.BlockSpec((tg, w), lambda i, j: (i, 0))
    m = pl.pallas_call(
        _gate_kernel,
        grid=(n // tg, nn),
        in_specs=[
            yspec(d), yspec(FOURIER_WIDTH), yspec(ATTN_WIDTH), yspec(RET_WIDTH),
            wspec(FOURIER_WIDTH), wspec(ATTN_WIDTH), wspec(RET_WIDTH),
            gspec(0), gspec(1), gspec(2),
            bspec(0), bspec(1), bspec(2),
        ],
        out_specs=pl.BlockSpec((tg, tn), lambda i, j: (i, j)),
        out_shape=jax.ShapeDtypeStruct((n, d), BF16),
        name="merge_gate",
        compiler_params=_params("parallel", "parallel"),
    )(u, y_f, y_a, y_r, w_bf, w_ba, w_br, w_mg, w_mg, w_mg, b_mg, b_mg, b_mg)
    return pl.pallas_call(
        functools.partial(_out_kernel, tn=tn),
        grid=(n // tm,),
        in_specs=[
            pl.BlockSpec((tm, d), lambda i: (i, 0)),
            _mod_spec(d, row_of_tile, 5),
            pl.BlockSpec((tm, d), lambda i: (i, 0)),
            pl.BlockSpec((None, d, d), lambda i: (layer, 0, 0)),
        ],
        out_specs=pl.BlockSpec((tm, d), lambda i: (i, 0)),
        out_shape=jax.ShapeDtypeStruct((n, d), F32),
        name="merge_out",
        compiler_params=_params("parallel"),
    )(h, mod, m, w_out)


def kernel(x, c, ctx, c_ctx, w_ada, b_ada, ffn1_norm, ffn1_w_gate, ffn1_w_up, ffn1_w_down, mix_norm, w_in, q_norm, k_norm, ret_decay, w_branch_fourier, w_branch_attn, w_branch_ret, w_merge_gate, b_merge_gate, w_out, ffn2_norm, ffn2_w_gate, ffn2_w_up, ffn2_w_down, final_norm):
    batch, seq, d = x.shape
    ctx_len = ctx.shape[1]
    depth = w_ada.shape[0]
    vec = lambda g: g.reshape(depth, 1, g.shape[-1])

    rows = -(-(batch + 1) // 16) * 16
    c_all = jnp.concatenate([c, c_ctx[None], jnp.zeros((rows - batch - 1, d), F32)], axis=0)
    mod = _adaln(c_all, w_ada, b_ada).reshape(depth * rows, 1, N_MOD * d)

    tm = _tile(seq, (512, 256, 128))
    tmc = _tile(batch * ctx_len, (512, 256, 128))
    tm_ffn = _tile(seq, (FFN_ROWS, 512, 256, 128))
    tmc_ffn = _tile(batch * ctx_len, (FFN_ROWS, 512, 256, 128))

    bf = lambda w: w.astype(BF16)
    ffn1 = (vec(ffn1_norm), bf(ffn1_w_gate), bf(ffn1_w_up), bf(ffn1_w_down))
    ffn2 = (vec(ffn2_norm), bf(ffn2_w_gate), bf(ffn2_w_up), bf(ffn2_w_down))
    mix_gain, w_in_t = vec(mix_norm), bf(w_in)
    qn, kn = vec(q_norm), vec(k_norm)
    w_bf, w_ba, w_br = bf(w_branch_fourier), bf(w_branch_attn), bf(w_branch_ret)
    w_mg, b_mg, w_o = bf(w_merge_gate), vec(b_merge_gate), bf(w_out)
    log_gamma = -jnp.exp(ret_decay.astype(F32))
    tables = _rope_tables(seq)
    four_lat = _fourier_consts(seq)
    four_ctx = _fourier_consts(ctx_len)

    h = x.reshape(batch * seq, d)
    hc = ctx.reshape(batch * ctx_len, d)
    for l in range(depth):
        need_ctx = l < depth - 1
        lat_rows = lambda t, l=l: (lambda i: l * rows + i // (seq // t))
        lat_row = lat_rows(tm)
        ctx_row = lambda i, l=l: l * rows + batch

        h = _ffn(h, mod, lat_rows(tm_ffn), tm_ffn, 0, *ffn1, l)
        hc = _ffn(hc, mod, ctx_row, tmc_ffn, 0, *ffn1, l)

        proj, u = _inproj(h, mod, lat_rows(tm_ffn), tm_ffn, mix_gain, w_in_t, l)
        proj_c, uc = _inproj(hc, mod, ctx_row, tmc_ffn, mix_gain, w_in_t, l)

        y_a = _attention(proj, proj_c, qn, kn, l, batch, seq, ctx_len, proj, tables)
        y_r = _retention(proj, log_gamma[l], batch, seq, proj_c, ctx_len)
        y_f = _fourier(proj, batch, seq, four_lat)
        h = _merge(h, mod, lat_row, tm, u, y_f, y_a, y_r, w_bf, w_ba, w_br, w_mg, b_mg, w_o, l)
        last = final_norm.reshape(1, d) if l == depth - 1 else None
        h = _ffn(h, mod, lat_rows(tm_ffn), tm_ffn, 6, *ffn2, l, final_gain=last)

        if need_ctx:
            yc_a = _attention(proj_c, proj_c, qn, kn, l, batch, ctx_len, ctx_len)
            yc_r = _retention(proj_c, log_gamma[l], batch, ctx_len)
            yc_f = _fourier(proj_c, batch, ctx_len, four_ctx)
            hc = _merge(hc, mod, ctx_row, tmc, uc, yc_f, yc_a, yc_r, w_bf, w_ba, w_br, w_mg, b_mg, w_o, l)
            hc = _ffn(hc, mod, ctx_row, tmc_ffn, 6, *ffn2, l)
    return h.reshape(batch, seq, d)
```

```python
import functools

import numpy as np
import jax
import jax.numpy as jnp
from jax import lax
from jax.experimental import pallas as pl
from jax.experimental.pallas import tpu as pltpu

F32 = jnp.float32
BF16 = jnp.bfloat16

EPS = 1e-6
N_MOD = 9
GRID_W = 64
HEAD_DIM = 128
HALF_ROT = HEAD_DIM // 2
ATTN_HEADS = 8
ATTN_KV_HEADS = 2
ATTN_GROUP = ATTN_HEADS // ATTN_KV_HEADS
ATTN_WIDTH = ATTN_HEADS * HEAD_DIM
KV_WIDTH = ATTN_KV_HEADS * HEAD_DIM
RET_HEADS = 4
RET_DK = 128
RET_WIDTH = RET_HEADS * RET_DK
FOURIER_GROUPS = 4
FOURIER_GROUP_DIM = 128
FOURIER_WIDTH = FOURIER_GROUPS * FOURIER_GROUP_DIM
IN_WIDTH = ATTN_WIDTH + 2 * KV_WIDTH + 4 * RET_WIDTH + FOURIER_WIDTH
ROPE_THETA = 10000.0

COL_AK = ATTN_WIDTH // 128
COL_AV = COL_AK + KV_WIDTH // 128
COL_RQ = COL_AV + KV_WIDTH // 128
COL_RK = COL_RQ + RET_WIDTH // 128
COL_RV = COL_RK + RET_WIDTH // 128
COL_RG = COL_RV + RET_WIDTH // 128
COL_FZ = COL_RG + RET_WIDTH // 128

IN_TILE = 2048

ATTN_Q_SCALE = HEAD_DIM ** -0.5 * float(np.log2(np.e))

V7X_VMEM_LIMIT_BYTES = 56 * 1024 * 1024
V7X_LARGE_VMEM_LIMIT_BYTES = 60 * 1024 * 1024


def _params(*sem, vmem_limit_bytes=V7X_VMEM_LIMIT_BYTES):
    return pltpu.CompilerParams(dimension_semantics=sem, vmem_limit_bytes=vmem_limit_bytes)


def _tile(n, prefs):
    for p in prefs:
        if n % p == 0:
            return p
    return n


def _dot(a, b):
    return jnp.dot(a, b, preferred_element_type=F32)


def _dot_nt(a, b):
    return lax.dot_general(a, b, (((1,), (1,)), ((), ())), preferred_element_type=F32)


def _rms(x):
    return x * lax.rsqrt(jnp.mean(x * x, axis=-1, keepdims=True) + EPS)


def _silu(x):
    return x * jax.nn.sigmoid(x)


def _adaln_kernel(c_ref, w_ref, b_ref, o_ref):
    a = _silu(c_ref[...]).astype(BF16)
    o_ref[...] = _dot(a, w_ref[...].astype(BF16)) + b_ref[...]


def _adaln(c_all, w_ada, b_ada):
    depth, d, nd = w_ada.shape
    r = c_all.shape[0]
    tn = _tile(nd, (1024, 512, 256, 128))
    return pl.pallas_call(
        _adaln_kernel,
        grid=(depth, nd // tn),
        in_specs=[
            pl.BlockSpec((r, d), lambda l, j: (0, 0)),
            pl.BlockSpec((None, d, tn), lambda l, j: (l, 0, j)),
            pl.BlockSpec((None, 1, tn), lambda l, j: (l, 0, j)),
        ],
        out_specs=pl.BlockSpec((None, r, tn), lambda l, j: (l, 0, j)),
        out_shape=jax.ShapeDtypeStruct((depth, r, nd), F32),
        name="adaln",
        compiler_params=_params("parallel", "parallel"),
    )(c_all, w_ada, b_ada.reshape(depth, 1, nd))


def _mod_spec(d, row_of_tile, k):
    return pl.BlockSpec((None, 1, d), lambda i, *_: (row_of_tile(i), 0, k))


FFN_TILE = 512
FFN_ROWS = 1024
ACT_CHUNK = 32
ROW_CHUNK = 16
ROW_UNROLL = 8


def _row_chunks(tm, dst_ref, body, read_first=False):
    def trip(t, carry):
        rows = [pl.ds(pl.multiple_of((t * ROW_UNROLL + k) * ROW_CHUNK, ROW_CHUNK), ROW_CHUNK)
                for k in range(ROW_UNROLL)]
        if read_first:
            for r, v in zip(rows, [body(r) for r in rows]):
                dst_ref[r, :] = v
        else:
            for r in rows:
                dst_ref[r, :] = body(r)
        return carry

    lax.fori_loop(0, tm // (ROW_CHUNK * ROW_UNROLL), trip, 0)


def _ffn_kernel(h_ref, sh_ref, sc_ref, gt_ref, gain_ref, wg_ref, wu_ref, wd_ref, *rest, nf, final):
    if final:
        fg_ref, o_ref, xn_ref, a_ref = rest
    else:
        o_ref, xn_ref, a_ref = rest
    acc_ref = o_ref
    f = pl.program_id(1)
    tm = h_ref.shape[0]

    @pl.when(f == 0)
    def _():
        gs, sh = gain_ref[...] * (1.0 + sc_ref[...]), sh_ref[...]

        _row_chunks(tm, xn_ref, lambda rows: (_rms(h_ref[rows, :]) * gs + sh).astype(BF16))

    @pl.when((pl.program_id(0) < 2) & (f == 0))
    def _():
        acc_ref[...] = jnp.zeros_like(acc_ref)

    xn = xn_ref[...]
    g = _dot(xn, wg_ref[...])
    u = _dot(xn, wu_ref[...])
    for r in range(0, tm, ACT_CHUNK):
        rows = slice(r, r + ACT_CHUNK)
        a_ref[rows, :] = (_silu(g[rows, :]) * u[rows, :]).astype(BF16)
    acc_ref[...] = jnp.where(f == 0, 0.0, acc_ref[...]) + _dot(a_ref[...], wd_ref[...])

    @pl.when(f == nf - 1)
    def _():
        half_gate = 0.5 * gt_ref[...]
        if final:
            fg = fg_ref[...]

        def chunk(rows):
            y = h_ref[rows, :] + half_gate * acc_ref[rows, :]
            return _rms(y) * fg if final else y

        _row_chunks(tm, o_ref, chunk, read_first=final)


def _ffn(h, mod, row_of_tile, tm, k0, gain, wg, wu, wd, layer, final_gain=None):
    n, d = h.shape
    tf = _tile(wg.shape[-1], (FFN_TILE, 256, 128))
    nf = wg.shape[-1] // tf
    final = final_gain is not None
    vec_spec = pl.BlockSpec((None, 1, d), lambda i, f: (layer, 0, 0))
    in_specs = [
        pl.BlockSpec((tm, d), lambda i, f: (i, 0)),
        _mod_spec(d, row_of_tile, k0),
        _mod_spec(d, row_of_tile, k0 + 1),
        _mod_spec(d, row_of_tile, k0 + 2),
        vec_spec,
        pl.BlockSpec((None, d, tf), lambda i, f: (layer, 0, f)),
        pl.BlockSpec((None, d, tf), lambda i, f: (layer, 0, f)),
        pl.BlockSpec((None, tf, d), lambda i, f: (layer, f, 0)),
    ]
    args = [h, mod, mod, mod, gain, wg, wu, wd]
    if final:
        in_specs.append(pl.BlockSpec((1, d), lambda i, f: (0, 0)))
        args.append(final_gain)
    return pl.pallas_call(
        functools.partial(_ffn_kernel, nf=nf, final=final),
        grid=(n // tm, nf),
        in_specs=in_specs,
        out_specs=pl.BlockSpec((tm, d), lambda i, f: (i, 0)),
        out_shape=jax.ShapeDtypeStruct((n, d), F32),
        scratch_shapes=[pltpu.VMEM((tm, d), BF16), pltpu.VMEM((tm, tf), BF16)],
        name="ffn",
        compiler_params=_params("arbitrary", "arbitrary", vmem_limit_bytes=V7X_LARGE_VMEM_LIMIT_BYTES),
    )(*args)


def _inproj_kernel(h_ref, sh_ref, sc_ref, gain_ref, w_ref, o_ref, u_ref):
    @pl.when(pl.program_id(1) == 0)
    def _():
        gs, sh = gain_ref[...] * (1.0 + sc_ref[...]), sh_ref[...]
        _row_chunks(h_ref.shape[0], u_ref, lambda rows: (_rms(h_ref[rows, :]) * gs + sh).astype(BF16))

    o_ref[...] = _dot(u_ref[...], w_ref[...]).astype(BF16)


def _inproj(h, mod, row_of_tile, tm, gain, w_in, layer):
    n, d = h.shape
    tn = IN_TILE
    assert w_in.shape[-1] == IN_WIDTH and IN_WIDTH % tn == 0
    return pl.pallas_call(
        _inproj_kernel,
        grid=(n // tm, IN_WIDTH // tn),
        in_specs=[
            pl.BlockSpec((tm, d), lambda i, j: (i, 0)),
            _mod_spec(d, row_of_tile, 3),
            _mod_spec(d, row_of_tile, 4),
            pl.BlockSpec((None, 1, d), lambda i, j: (layer, 0, 0)),
            pl.BlockSpec((None, d, tn), lambda i, j: (layer, 0, j)),
        ],
        out_specs=[
            pl.BlockSpec((tm, tn), lambda i, j: (i, j)),
            pl.BlockSpec((tm, d), lambda i, j: (i, 0)),
        ],
        out_shape=[
            jax.ShapeDtypeStruct((n, IN_WIDTH), BF16),
            jax.ShapeDtypeStruct((n, d), BF16),
        ],
        name="inproj",
        compiler_params=_params("parallel", "arbitrary", vmem_limit_bytes=V7X_LARGE_VMEM_LIMIT_BYTES),
    )(h, mod, mod, gain, w_in)


def _rope_tables(seq):
    t = jnp.arange(seq)
    row = (t // GRID_W).astype(F32)
    col = (t % GRID_W).astype(F32)
    inv = ROPE_THETA ** (-jnp.arange(0, HALF_ROT, 2, dtype=F32) / HALF_ROT)
    ar, ac = row[:, None] * inv[None], col[:, None] * inv[None]
    zero = jnp.zeros_like(ar)
    cos = jnp.concatenate([jnp.cos(ar), jnp.cos(ar), jnp.cos(ac), jnp.cos(ac)], axis=-1)
    sin_lo = jnp.concatenate([-jnp.sin(ar), zero, -jnp.sin(ac), zero], axis=-1)
    sin_hi = jnp.concatenate([zero, jnp.sin(ar), zero, jnp.sin(ac)], axis=-1)
    return cos, sin_lo, sin_hi


def _rope(x, cos, sin_lo, sin_hi):
    return x * cos + pltpu.roll(x, HEAD_DIM - HALF_ROT // 2, 1) * sin_lo + pltpu.roll(x, HALF_ROT // 2, 1) * sin_hi


def _attn_kernel(q_ref, kc_ref, vc_ref, qn_ref, kn_ref, *rest, has_lat):
    if has_lat:
        kl_ref, vl_ref, qcos_ref, qlo_ref, qhi_ref, kcos_ref, klo_ref, khi_ref, o_ref, k_ref, v_ref = rest
    else:
        o_ref, k_ref, v_ref = rest
    nc = kc_ref.shape[0]

    @pl.when(pl.program_id(1) == 0)
    def _():
        kn = kn_ref[...]
        for kv in range(ATTN_KV_HEADS):
            sl = slice(kv * HEAD_DIM, (kv + 1) * HEAD_DIM)
            k_ref[kv, :nc, :] = (_rms(kc_ref[:, sl].astype(F32)) * kn).astype(BF16)
            v_ref[kv, :nc, :HEAD_DIM] = vc_ref[:, sl]
            if has_lat:
                kl = _rms(kl_ref[:, sl].astype(F32)) * kn
                k_ref[kv, nc:, :] = _rope(kl, kcos_ref[...], klo_ref[...], khi_ref[...]).astype(BF16)
                v_ref[kv, nc:, :HEAD_DIM] = vl_ref[:, sl]
            v_ref[kv, :, HEAD_DIM:] = jnp.ones((v_ref.shape[1], HEAD_DIM), BF16)

    qn = qn_ref[...]
    for hd in range(ATTN_HEADS):
        sl = slice(hd * HEAD_DIM, (hd + 1) * HEAD_DIM)
        q = _rms(q_ref[:, sl].astype(F32)) * qn
        if has_lat:
            q = _rope(q, qcos_ref[...], qlo_ref[...], qhi_ref[...])
        s = _dot_nt((q * ATTN_Q_SCALE).astype(BF16), k_ref[hd // ATTN_GROUP])
        p = jnp.exp2(s - jnp.max(s, axis=-1, keepdims=True)).astype(BF16)
        ov = _dot(p, v_ref[hd // ATTN_GROUP])
        o_ref[:, sl] = (ov[:, :HEAD_DIM] / ov[:, HEAD_DIM:]).astype(BF16)


def _attention(proj_q, proj_c, q_norm, k_norm, layer, batch, seq, ctx_len, proj_kv=None, tables=None):
    n = proj_q.shape[0]
    tq = _tile(seq, (256, 128))
    nq = seq // tq
    has_lat = proj_kv is not None
    n_keys = ctx_len + (seq if has_lat else 0)
    col_k, col_v = COL_AK * 128 // KV_WIDTH, COL_AV * 128 // KV_WIDTH
    norm_spec = pl.BlockSpec((None, 1, HEAD_DIM), lambda b, i: (layer, 0, 0))
    in_specs = [
        pl.BlockSpec((tq, ATTN_WIDTH), lambda b, i: (b * nq + i, 0)),
        pl.BlockSpec((ctx_len, KV_WIDTH), lambda b, i: (b, col_k)),
        pl.BlockSpec((ctx_len, KV_WIDTH), lambda b, i: (b, col_v)),
        norm_spec,
        norm_spec,
    ]
    args = [proj_q, proj_c, proj_c, q_norm, k_norm]
    if has_lat:
        in_specs += [
            pl.BlockSpec((seq, KV_WIDTH), lambda b, i: (b, col_k)),
            pl.BlockSpec((seq, KV_WIDTH), lambda b, i: (b, col_v)),
        ]
        in_specs += [pl.BlockSpec((tq, HEAD_DIM), lambda b, i: (i, 0))] * 3
        in_specs += [pl.BlockSpec((seq, HEAD_DIM), lambda b, i: (0, 0))] * 3
        args += [proj_kv, proj_kv, *tables, *tables]
    return pl.pallas_call(
        functools.partial(_attn_kernel, has_lat=has_lat),
        grid=(batch, nq),
        in_specs=in_specs,
        out_specs=pl.BlockSpec((tq, ATTN_WIDTH), lambda b, i: (b * nq + i, 0)),
        out_shape=jax.ShapeDtypeStruct((n, ATTN_WIDTH), BF16),
        scratch_shapes=[
            pltpu.VMEM((ATTN_KV_HEADS, n_keys, HEAD_DIM), BF16),
            pltpu.VMEM((ATTN_KV_HEADS, n_keys, 2 * HEAD_DIM), BF16),
        ],
        name="attention",
        compiler_params=_params("parallel", "arbitrary"),
    )(*args)


def _ret_kernel(lg_ref, q_ref, k_ref, v_ref, g_ref, *rest, seq, ctx_len, chunk, has_ctx):
    if has_ctx:
        kc_ref, vc_ref, o_ref, a_ref, s_ref = rest
    else:
        o_ref, a_ref, s_ref = rest
    hd = pl.program_id(1)
    lgf, lgb = lg_ref[0, hd], lg_ref[1, hd]
    c, dk = chunk, RET_DK
    n = seq // c
    scale = dk ** -0.5
    a = lax.broadcasted_iota(jnp.int32, (c, 1), 0).astype(F32)
    wq_f, wq_b = jnp.exp(lgf * (a + 1.0)), jnp.exp(lgb * (c - a))
    wk_f, wk_b = jnp.exp(lgf * (c - 1.0 - a)) * scale, jnp.exp(lgb * a) * scale
    d = (lax.broadcasted_iota(jnp.int32, (c, c), 0) - lax.broadcasted_iota(jnp.int32, (c, c), 1)).astype(F32)
    dmask = jnp.exp(jnp.where(d >= 0.0, lgf * d, -lgb * d)) * jnp.where(d == 0.0, 2.0 * scale, scale)
    zero = jnp.zeros((1, dk), F32)
    gf, gb = jnp.exp(zero + lgf * c), jnp.exp(zero + lgb * c)

    def kv_outer(kr, vr, j):
        k = kr[j * c:(j + 1) * c, :]
        kk = jnp.concatenate([k * wk_f, k * wk_b], axis=1)
        return _dot(kk.T.astype(BF16), vr[j * c:(j + 1) * c, :])

    sf = jnp.zeros((dk, dk), F32)
    sb = jnp.zeros((dk, dk), F32)
    if has_ctx:
        outer = [kv_outer(kc_ref, vc_ref, j) for j in range(ctx_len // c)]
        for o in outer:
            sf = gf * sf + o[:dk]
        for o in reversed(outer):
            sb = gb * sb + o[dk:]
    for i in range(n):
        a_ref[i] = kv_outer(k_ref, v_ref, i)
    for i in range(n):
        s_ref[i, :dk, :] = sf.astype(BF16)
        sf = gf * sf + a_ref[i, :dk, :]
    for i in reversed(range(n)):
        s_ref[i, dk:, :] = sb.astype(BF16)
        sb = gb * sb + a_ref[i, dk:, :]
    for i in range(n):
        rows = slice(i * c, (i + 1) * c)
        q = q_ref[rows, :]
        s = _dot_nt(q, k_ref[rows, :])
        o = _dot((s * dmask).astype(BF16), v_ref[rows, :])
        qq = jnp.concatenate([q * wq_f, q * wq_b], axis=1).astype(BF16)
        o = o + _dot(qq, s_ref[i])
        o_ref[rows, :] = (_silu(g_ref[rows, :].astype(F32)) * _rms(o)).astype(BF16)


def _retention(proj, log_gamma, batch, seq, proj_ctx=None, ctx_len=0):
    n = proj.shape[0]
    has_ctx = proj_ctx is not None
    chunk = 256 if seq % 256 == 0 and ctx_len % 256 == 0 else 128
    in_specs = [
        pl.BlockSpec(memory_space=pltpu.SMEM),
        pl.BlockSpec((seq, RET_DK), lambda b, h: (b, COL_RQ + h)),
        pl.BlockSpec((seq, RET_DK), lambda b, h: (b, COL_RK + h)),
        pl.BlockSpec((seq, RET_DK), lambda b, h: (b, COL_RV + h)),
        pl.BlockSpec((seq, RET_DK), lambda b, h: (b, COL_RG + h)),
    ]
    args = [log_gamma, proj, proj, proj, proj]
    if has_ctx:
        in_specs += [
            pl.BlockSpec((ctx_len, RET_DK), lambda b, h: (b, COL_RK + h)),
            pl.BlockSpec((ctx_len, RET_DK), lambda b, h: (b, COL_RV + h)),
        ]
        args += [proj_ctx, proj_ctx]
    return pl.pallas_call(
        functools.partial(_ret_kernel, seq=seq, ctx_len=ctx_len, chunk=chunk, has_ctx=has_ctx),
        grid=(batch, RET_HEADS),
        in_specs=in_specs,
        out_specs=pl.BlockSpec((seq, RET_DK), lambda b, h: (b, h)),
        out_shape=jax.ShapeDtypeStruct((n, RET_WIDTH), BF16),
        scratch_shapes=[
            pltpu.VMEM((seq // chunk, 2 * RET_DK, RET_DK), F32),
            pltpu.VMEM((seq // chunk, 2 * RET_DK, RET_DK), BF16),
        ],
        name="retention",
        compiler_params=_params("parallel", "parallel"),
    )(*args)


def _dft_cos_sin(n):
    k = np.arange(n, dtype=np.int64)
    ang = 2.0 * np.pi * ((k[:, None] * k[None, :]) % n).astype(np.float64) / n
    return np.cos(ang), np.sin(ang)


def _fourier_consts(seq):
    cg, sg = _dft_cos_sin(FOURIER_GROUP_DIM)
    norm = 1.0 / np.sqrt(float(seq) * FOURIER_GROUP_DIM)
    eye = np.eye(FOURIER_GROUPS)
    chan = np.concatenate([np.kron(eye, cg), np.kron(eye, -sg)], axis=1) * norm
    cs, ss = _dft_cos_sin(seq)
    as_bf16 = lambda a: jnp.asarray(a.astype(np.float32)).astype(BF16)
    return as_bf16(chan), as_bf16(cs), as_bf16(ss)


def _fchan_kernel(z_ref, m_ref, o_ref):
    o_ref[...] = _dot(z_ref[...], m_ref[...]).astype(BF16)


def _fseq_kernel(c_ref, s_ref, zc_ref, zs_ref, o_ref):
    o_ref[...] = (_dot(c_ref[...], zc_ref[...]) + _dot(s_ref[...], zs_ref[...])).astype(BF16)


def _fourier(proj, batch, seq, consts):
    n = proj.shape[0]
    chan, cs, ss = consts
    fw = FOURIER_WIDTH
    tm = _tile(n, (1024, 512, 256, 128))
    zcs = pl.pallas_call(
        _fchan_kernel,
        grid=(n // tm,),
        in_specs=[
            pl.BlockSpec((tm, fw), lambda i: (i, COL_FZ * 128 // fw)),
            pl.BlockSpec((fw, 2 * fw), lambda i: (0, 0)),
        ],
        out_specs=pl.BlockSpec((tm, 2 * fw), lambda i: (i, 0)),
        out_shape=jax.ShapeDtypeStruct((n, 2 * fw), BF16),
        name="fourier_chan",
        compiler_params=_params("parallel"),
    )(proj, chan)
    tk = _tile(seq, (1024, 512, 256, 128))
    nk = seq // tk
    return pl.pallas_call(
        _fseq_kernel,
        grid=(nk, batch),
        in_specs=[
            pl.BlockSpec((tk, seq), lambda k, b: (k, 0)),
            pl.BlockSpec((tk, seq), lambda k, b: (k, 0)),
            pl.BlockSpec((seq, fw), lambda k, b: (b, 0)),
            pl.BlockSpec((seq, fw), lambda k, b: (b, 1)),
        ],
        out_specs=pl.BlockSpec((tk, fw), lambda k, b: (b * nk + k, 0)),
        out_shape=jax.ShapeDtypeStruct((n, fw), BF16),
        name="fourier_seq",
        compiler_params=_params("parallel", "parallel"),
    )(cs, ss, zcs, zcs)


MERGE_TILE = 512
MERGE_GATE_TILE = 512


def _gate_kernel(u_ref, yf_ref, ya_ref, yr_ref, wf_ref, wa_ref, wr_ref,
                 gf_ref, ga_ref, gr_ref, bf_ref, ba_ref, br_ref, o_ref):
    u = u_ref[...]
    gates = [_dot(u, g[...]) for g in (gf_ref, ga_ref, gr_ref)]
    branches = [_dot(y[...], w[...]) for y, w in ((yf_ref, wf_ref), (ya_ref, wa_ref), (yr_ref, wr_ref))]
    biases = [b[...] for b in (bf_ref, ba_ref, br_ref)]
    for r in range(0, o_ref.shape[0], ACT_CHUNK):
        rows = slice(r, r + ACT_CHUNK)
        terms = [jax.nn.sigmoid(g[rows, :] + b) * y[rows, :] for g, b, y in zip(gates, biases, branches)]
        o_ref[rows, :] = (terms[0] + terms[1] + terms[2]).astype(BF16)


def _out_kernel(h_ref, gt_ref, m_ref, wo_ref, o_ref, *, tn):
    m = m_ref[...]
    for c in range(0, o_ref.shape[1], tn):
        cols = slice(c, c + tn)
        o_ref[:, cols] = h_ref[:, cols] + gt_ref[:, cols] * _dot(m, wo_ref[:, cols])


def _merge(h, mod, row_of_tile, tm, u, y_f, y_a, y_r, w_bf, w_ba, w_br, w_mg, b_mg, w_out, layer):
    n, d = h.shape
    tn = _tile(d, (MERGE_TILE, 256, 128))
    tc = _tile(d, (MERGE_GATE_TILE, 512, 256, 128))
    nn = d // tc
    tg = _tile(n, (1024, 512, 256, 128))
    wspec = lambda rows: pl.BlockSpec((None, rows, tc), lambda i, j: (layer, 0, j))
    gspec = lambda k: pl.BlockSpec((None, d, tc), lambda i, j: (layer, 0, k * nn + j))
    bspec = lambda k: pl.BlockSpec((None, 1, tc), lambda i, j: (layer, 0, k * nn + j))
    yspec = lambda w: pl.BlockSpec((tg, w), lambda i, j: (i, 0))
    m = pl.pallas_call(
        _gate_kernel,
        grid=(n // tg, nn),
        in_specs=[
            yspec(d), yspec(FOURIER_WIDTH), yspec(ATTN_WIDTH), yspec(RET_WIDTH),
            wspec(FOURIER_WIDTH), wspec(ATTN_WIDTH), wspec(RET_WIDTH),
            gspec(0), gspec(1), gspec(2),
            bspec(0), bspec(1), bspec(2),
        ],
        out_specs=pl.BlockSpec((tg, tc), lambda i, j: (i, j)),
        out_shape=jax.ShapeDtypeStruct((n, d), BF16),
        name="merge_gate",
        compiler_params=_params("parallel", "parallel", vmem_limit_bytes=V7X_LARGE_VMEM_LIMIT_BYTES),
    )(u, y_f, y_a, y_r, w_bf, w_ba, w_br, w_mg, w_mg, w_mg, b_mg, b_mg, b_mg)
    return pl.pallas_call(
        functools.partial(_out_kernel, tn=tn),
        grid=(n // tm,),
        in_specs=[
            pl.BlockSpec((tm, d), lambda i: (i, 0)),
            _mod_spec(d, row_of_tile, 5),
            pl.BlockSpec((tm, d), lambda i: (i, 0)),
            pl.BlockSpec((None, d, d), lambda i: (layer, 0, 0)),
        ],
        out_specs=pl.BlockSpec((tm, d), lambda i: (i, 0)),
        out_shape=jax.ShapeDtypeStruct((n, d), F32),
        name="merge_out",
        compiler_params=_params("parallel"),
    )(h, mod, m, w_out)


def kernel(x, c, ctx, c_ctx, w_ada, b_ada, ffn1_norm, ffn1_w_gate, ffn1_w_up, ffn1_w_down, mix_norm, w_in, q_norm, k_norm, ret_decay, w_branch_fourier, w_branch_attn, w_branch_ret, w_merge_gate, b_merge_gate, w_out, ffn2_norm, ffn2_w_gate, ffn2_w_up, ffn2_w_down, final_norm):
    batch, seq, d = x.shape
    ctx_len = ctx.shape[1]
    depth = w_ada.shape[0]
    vec = lambda g: g.reshape(depth, 1, g.shape[-1])

    rows = -(-(batch + 1) // 16) * 16
    c_all = jnp.concatenate([c, c_ctx[None], jnp.zeros((rows - batch - 1, d), F32)], axis=0)
    mod = _adaln(c_all, w_ada, b_ada).reshape(depth * rows, 1, N_MOD * d)

    tm = _tile(seq, (512, 256, 128))
    tmc = _tile(batch * ctx_len, (512, 256, 128))
    tm_ffn = _tile(seq, (FFN_ROWS, 512, 256, 128))
    tmc_ffn = _tile(batch * ctx_len, (FFN_ROWS, 512, 256, 128))

    bf = lambda w: w.astype(BF16)
    ffn1 = (vec(ffn1_norm), bf(ffn1_w_gate), bf(ffn1_w_up), bf(ffn1_w_down))
    ffn2 = (vec(ffn2_norm), bf(ffn2_w_gate), bf(ffn2_w_up), bf(ffn2_w_down))
    mix_gain, w_in_t = vec(mix_norm), bf(w_in)
    qn, kn = vec(q_norm), vec(k_norm)
    w_bf, w_ba, w_br = bf(w_branch_fourier), bf(w_branch_attn), bf(w_branch_ret)
    w_mg, b_mg, w_o = bf(w_merge_gate), vec(b_merge_gate), bf(w_out)
    log_gamma = -jnp.exp(ret_decay.astype(F32))
    tables = _rope_tables(seq)
    four_lat = _fourier_consts(seq)
    four_ctx = _fourier_consts(ctx_len)

    h = x.reshape(batch * seq, d)
    hc = ctx.reshape(batch * ctx_len, d)
    for l in range(depth):
        need_ctx = l < depth - 1
        lat_rows = lambda t, l=l: (lambda i: l * rows + i // (seq // t))
        lat_row = lat_rows(tm)
        ctx_row = lambda i, l=l: l * rows + batch

        h = _ffn(h, mod, lat_rows(tm_ffn), tm_ffn, 0, *ffn1, l)
        hc = _ffn(hc, mod, ctx_row, tmc_ffn, 0, *ffn1, l)

        proj, u = _inproj(h, mod, lat_rows(tm_ffn), tm_ffn, mix_gain, w_in_t, l)
        proj_c, uc = _inproj(hc, mod, ctx_row, tmc_ffn, mix_gain, w_in_t, l)

        y_a = _attention(proj, proj_c, qn, kn, l, batch, seq, ctx_len, proj, tables)
        y_r = _retention(proj, log_gamma[l], batch, seq, proj_c, ctx_len)
        y_f = _fourier(proj, batch, seq, four_lat)
        h = _merge(h, mod, lat_row, tm, u, y_f, y_a, y_r, w_bf, w_ba, w_br, w_mg, b_mg, w_o, l)
        last = final_norm.reshape(1, d) if l == depth - 1 else None
        h = _ffn(h, mod, lat_rows(tm_ffn), tm_ffn, 6, *ffn2, l, final_gain=last)

        if need_ctx:
            yc_a = _attention(proj_c, proj_c, qn, kn, l, batch, ctx_len, ctx_len)
            yc_r = _retention(proj_c, log_gamma[l], batch, ctx_len)
            yc_f = _fourier(proj_c, batch, ctx_len, four_ctx)
            hc = _merge(hc, mod, ctx_row, tmc, uc, yc_f, yc_a, yc_r, w_bf, w_ba, w_br, w_mg, b_mg, w_o, l)
            hc = _ffn(hc, mod, ctx_row, tmc_ffn, 6, *ffn2, l)
    return h.reshape(batch, seq, d)
```

```python
import functools

import numpy as np
import jax
import jax.numpy as jnp
from jax import lax
from jax.experimental import pallas as pl
from jax.experimental.pallas import tpu as pltpu

F32 = jnp.float32
BF16 = jnp.bfloat16

EPS = 1e-6
N_MOD = 9
GRID_W = 64
HEAD_DIM = 128
HALF_ROT = HEAD_DIM // 2
ATTN_HEADS = 8
ATTN_KV_HEADS = 2
ATTN_GROUP = ATTN_HEADS // ATTN_KV_HEADS
ATTN_WIDTH = ATTN_HEADS * HEAD_DIM
KV_WIDTH = ATTN_KV_HEADS * HEAD_DIM
RET_HEADS = 4
RET_DK = 128
RET_WIDTH = RET_HEADS * RET_DK
FOURIER_GROUPS = 4
FOURIER_GROUP_DIM = 128
FOURIER_WIDTH = FOURIER_GROUPS * FOURIER_GROUP_DIM
IN_WIDTH = ATTN_WIDTH + 2 * KV_WIDTH + 4 * RET_WIDTH + FOURIER_WIDTH
ROPE_THETA = 10000.0

COL_AK = ATTN_WIDTH // 128
COL_AV = COL_AK + KV_WIDTH // 128
COL_RQ = COL_AV + KV_WIDTH // 128
COL_RK = COL_RQ + RET_WIDTH // 128
COL_RV = COL_RK + RET_WIDTH // 128
COL_RG = COL_RV + RET_WIDTH // 128
COL_FZ = COL_RG + RET_WIDTH // 128

IN_TILE = 2048

ATTN_Q_SCALE = HEAD_DIM ** -0.5 * float(np.log2(np.e))

V7X_VMEM_LIMIT_BYTES = 56 * 1024 * 1024
V7X_LARGE_VMEM_LIMIT_BYTES = 60 * 1024 * 1024


def _params(*sem, vmem_limit_bytes=V7X_VMEM_LIMIT_BYTES):
    return pltpu.CompilerParams(dimension_semantics=sem, vmem_limit_bytes=vmem_limit_bytes)


def _tile(n, prefs):
    for p in prefs:
        if n % p == 0:
            return p
    return n


def _dot(a, b):
    return jnp.dot(a, b, preferred_element_type=F32)


def _dot_nt(a, b):
    return lax.dot_general(a, b, (((1,), (1,)), ((), ())), preferred_element_type=F32)


def _rms(x):
    return x * lax.rsqrt(jnp.mean(x * x, axis=-1, keepdims=True) + EPS)


def _silu(x):
    return x * jax.nn.sigmoid(x)


def _adaln_kernel(c_ref, w_ref, b_ref, o_ref):
    a = _silu(c_ref[...]).astype(BF16)
    o_ref[...] = _dot(a, w_ref[...].astype(BF16)) + b_ref[...]


def _adaln(c_all, w_ada, b_ada):
    depth, d, nd = w_ada.shape
    r = c_all.shape[0]
    tn = _tile(nd, (1024, 512, 256, 128))
    return pl.pallas_call(
        _adaln_kernel,
        grid=(depth, nd // tn),
        in_specs=[
            pl.BlockSpec((r, d), lambda l, j: (0, 0)),
            pl.BlockSpec((None, d, tn), lambda l, j: (l, 0, j)),
            pl.BlockSpec((None, 1, tn), lambda l, j: (l, 0, j)),
        ],
        out_specs=pl.BlockSpec((None, r, tn), lambda l, j: (l, 0, j)),
        out_shape=jax.ShapeDtypeStruct((depth, r, nd), F32),
        name="adaln",
        compiler_params=_params("parallel", "parallel"),
    )(c_all, w_ada, b_ada.reshape(depth, 1, nd))


def _mod_spec(d, row_of_tile, k):
    return pl.BlockSpec((None, 1, d), lambda i, *_: (row_of_tile(i), 0, k))


FFN_TILE = 512
FFN_ROWS = 1024
ACT_CHUNK = 32
ROW_CHUNK = 16
ROW_UNROLL = 16


def _row_chunks(tm, dst_ref, body, read_first=False):
    unroll = ROW_UNROLL
    while tm % (ROW_CHUNK * unroll):
        unroll //= 2

    def trip(t, carry):
        rows = [pl.ds(pl.multiple_of((t * unroll + k) * ROW_CHUNK, ROW_CHUNK), ROW_CHUNK)
                for k in range(unroll)]
        if read_first:
            for r, v in zip(rows, [body(r) for r in rows]):
                dst_ref[r, :] = v
        else:
            for r in rows:
                dst_ref[r, :] = body(r)
        return carry

    lax.fori_loop(0, tm // (ROW_CHUNK * unroll), trip, 0)


def _ffn_kernel(h_ref, sh_ref, sc_ref, gt_ref, gain_ref, wg_ref, wu_ref, wd_ref, *rest, nf, final):
    if final:
        fg_ref, o_ref, xn_ref, a_ref = rest
    else:
        o_ref, xn_ref, a_ref = rest
    acc_ref = o_ref
    f = pl.program_id(1)
    tm = h_ref.shape[0]

    @pl.when(f == 0)
    def _():
        gs, sh = gain_ref[...] * (1.0 + sc_ref[...]), sh_ref[...]

        _row_chunks(tm, xn_ref, lambda rows: (_rms(h_ref[rows, :]) * gs + sh).astype(BF16))

    @pl.when((pl.program_id(0) < 2) & (f == 0))
    def _():
        acc_ref[...] = jnp.zeros_like(acc_ref)

    xn = xn_ref[...]
    g = _dot(xn, wg_ref[...])
    u = _dot(xn, wu_ref[...])
    for r in range(0, tm, ACT_CHUNK):
        rows = slice(r, r + ACT_CHUNK)
        a_ref[rows, :] = (_silu(g[rows, :]) * u[rows, :]).astype(BF16)
    acc_ref[...] = jnp.where(f == 0, 0.0, acc_ref[...]) + _dot(a_ref[...], wd_ref[...])

    @pl.when(f == nf - 1)
    def _():
        half_gate = 0.5 * gt_ref[...]
        if final:
            fg = fg_ref[...]

        def chunk(rows):
            y = h_ref[rows, :] + half_gate * acc_ref[rows, :]
            return _rms(y) * fg if final else y

        _row_chunks(tm, o_ref, chunk, read_first=final)


def _ffn(h, mod, row_of_tile, tm, k0, gain, wg, wu, wd, layer, final_gain=None):
    n, d = h.shape
    tf = _tile(wg.shape[-1], (FFN_TILE, 256, 128))
    nf = wg.shape[-1] // tf
    final = final_gain is not None
    vec_spec = pl.BlockSpec((None, 1, d), lambda i, f: (layer, 0, 0))
    in_specs = [
        pl.BlockSpec((tm, d), lambda i, f: (i, 0)),
        _mod_spec(d, row_of_tile, k0),
        _mod_spec(d, row_of_tile, k0 + 1),
        _mod_spec(d, row_of_tile, k0 + 2),
        vec_spec,
        pl.BlockSpec((None, d, tf), lambda i, f: (layer, 0, f)),
        pl.BlockSpec((None, d, tf), lambda i, f: (layer, 0, f)),
        pl.BlockSpec((None, tf, d), lambda i, f: (layer, f, 0)),
    ]
    args = [h, mod, mod, mod, gain, wg, wu, wd]
    if final:
        in_specs.append(pl.BlockSpec((1, d), lambda i, f: (0, 0)))
        args.append(final_gain)
    return pl.pallas_call(
        functools.partial(_ffn_kernel, nf=nf, final=final),
        grid=(n // tm, nf),
        in_specs=in_specs,
        out_specs=pl.BlockSpec((tm, d), lambda i, f: (i, 0)),
        out_shape=jax.ShapeDtypeStruct((n, d), F32),
        scratch_shapes=[pltpu.VMEM((tm, d), BF16), pltpu.VMEM((tm, tf), BF16)],
        name="ffn",
        compiler_params=_params("arbitrary", "arbitrary", vmem_limit_bytes=V7X_LARGE_VMEM_LIMIT_BYTES),
    )(*args)


def _inproj_kernel(h_ref, sh_ref, sc_ref, gain_ref, w_ref, o_ref, u_ref):
    @pl.when(pl.program_id(1) == 0)
    def _():
        gs, sh = gain_ref[...] * (1.0 + sc_ref[...]), sh_ref[...]
        _row_chunks(h_ref.shape[0], u_ref, lambda rows: (_rms(h_ref[rows, :]) * gs + sh).astype(BF16))

    o_ref[...] = _dot(u_ref[...], w_ref[...]).astype(BF16)


def _inproj(h, mod, row_of_tile, tm, gain, w_in, layer):
    n, d = h.shape
    tn = IN_TILE
    assert w_in.shape[-1] == IN_WIDTH and IN_WIDTH % tn == 0
    return pl.pallas_call(
        _inproj_kernel,
        grid=(n // tm, IN_WIDTH // tn),
        in_specs=[
            pl.BlockSpec((tm, d), lambda i, j: (i, 0)),
            _mod_spec(d, row_of_tile, 3),
            _mod_spec(d, row_of_tile, 4),
            pl.BlockSpec((None, 1, d), lambda i, j: (layer, 0, 0)),
            pl.BlockSpec((None, d, tn), lambda i, j: (layer, 0, j)),
        ],
        out_specs=[
            pl.BlockSpec((tm, tn), lambda i, j: (i, j)),
            pl.BlockSpec((tm, d), lambda i, j: (i, 0)),
        ],
        out_shape=[
            jax.ShapeDtypeStruct((n, IN_WIDTH), BF16),
            jax.ShapeDtypeStruct((n, d), BF16),
        ],
        name="inproj",
        compiler_params=_params("parallel", "arbitrary", vmem_limit_bytes=V7X_LARGE_VMEM_LIMIT_BYTES),
    )(h, mod, mod, gain, w_in)


def _rope_tables(seq):
    t = jnp.arange(seq)
    row = (t // GRID_W).astype(F32)
    col = (t % GRID_W).astype(F32)
    inv = ROPE_THETA ** (-jnp.arange(0, HALF_ROT, 2, dtype=F32) / HALF_ROT)
    ar, ac = row[:, None] * inv[None], col[:, None] * inv[None]
    zero = jnp.zeros_like(ar)
    cos = jnp.concatenate([jnp.cos(ar), jnp.cos(ar), jnp.cos(ac), jnp.cos(ac)], axis=-1)
    sin_lo = jnp.concatenate([-jnp.sin(ar), zero, -jnp.sin(ac), zero], axis=-1)
    sin_hi = jnp.concatenate([zero, jnp.sin(ar), zero, jnp.sin(ac)], axis=-1)
    return cos, sin_lo, sin_hi


def _rope(x, cos, sin_lo, sin_hi):
    return x * cos + pltpu.roll(x, HEAD_DIM - HALF_ROT // 2, 1) * sin_lo + pltpu.roll(x, HALF_ROT // 2, 1) * sin_hi


def _attn_kernel(q_ref, kc_ref, vc_ref, qn_ref, kn_ref, *rest, has_lat):
    if has_lat:
        kl_ref, vl_ref, qcos_ref, qlo_ref, qhi_ref, kcos_ref, klo_ref, khi_ref, o_ref, k_ref, v_ref = rest
    else:
        o_ref, k_ref, v_ref = rest
    nc = kc_ref.shape[0]

    @pl.when(pl.program_id(1) == 0)
    def _():
        kn = kn_ref[...]
        for kv in range(ATTN_KV_HEADS):
            sl = slice(kv * HEAD_DIM, (kv + 1) * HEAD_DIM)
            k_ref[kv, :nc, :] = (_rms(kc_ref[:, sl].astype(F32)) * kn).astype(BF16)
            v_ref[kv, :nc, :HEAD_DIM] = vc_ref[:, sl]
            if has_lat:
                kl = _rms(kl_ref[:, sl].astype(F32)) * kn
                k_ref[kv, nc:, :] = _rope(kl, kcos_ref[...], klo_ref[...], khi_ref[...]).astype(BF16)
                v_ref[kv, nc:, :HEAD_DIM] = vl_ref[:, sl]
            v_ref[kv, :, HEAD_DIM:] = jnp.ones((v_ref.shape[1], HEAD_DIM), BF16)

    qn = qn_ref[...]
    for hd in range(ATTN_HEADS):
        sl = slice(hd * HEAD_DIM, (hd + 1) * HEAD_DIM)
        q = _rms(q_ref[:, sl].astype(F32)) * qn
        if has_lat:
            q = _rope(q, qcos_ref[...], qlo_ref[...], qhi_ref[...])
        s = _dot_nt((q * ATTN_Q_SCALE).astype(BF16), k_ref[hd // ATTN_GROUP])
        p = jnp.exp2(s - jnp.max(s, axis=-1, keepdims=True)).astype(BF16)
        ov = _dot(p, v_ref[hd // ATTN_GROUP])
        o_ref[:, sl] = (ov[:, :HEAD_DIM] / ov[:, HEAD_DIM:]).astype(BF16)


def _attention(proj_q, proj_c, q_norm, k_norm, layer, batch, seq, ctx_len, proj_kv=None, tables=None):
    n = proj_q.shape[0]
    tq = _tile(seq, (256, 128))
    nq = seq // tq
    has_lat = proj_kv is not None
    n_keys = ctx_len + (seq if has_lat else 0)
    col_k, col_v = COL_AK * 128 // KV_WIDTH, COL_AV * 128 // KV_WIDTH
    norm_spec = pl.BlockSpec((None, 1, HEAD_DIM), lambda b, i: (layer, 0, 0))
    in_specs = [
        pl.BlockSpec((tq, ATTN_WIDTH), lambda b, i: (b * nq + i, 0)),
        pl.BlockSpec((ctx_len, KV_WIDTH), lambda b, i: (b, col_k)),
        pl.BlockSpec((ctx_len, KV_WIDTH), lambda b, i: (b, col_v)),
        norm_spec,
        norm_spec,
    ]
    args = [proj_q, proj_c, proj_c, q_norm, k_norm]
    if has_lat:
        in_specs += [
            pl.BlockSpec((seq, KV_WIDTH), lambda b, i: (b, col_k)),
            pl.BlockSpec((seq, KV_WIDTH), lambda b, i: (b, col_v)),
        ]
        in_specs += [pl.BlockSpec((tq, HEAD_DIM), lambda b, i: (i, 0))] * 3
        in_specs += [pl.BlockSpec((seq, HEAD_DIM), lambda b, i: (0, 0))] * 3
        args += [proj_kv, proj_kv, *tables, *tables]
    return pl.pallas_call(
        functools.partial(_attn_kernel, has_lat=has_lat),
        grid=(batch, nq),
        in_specs=in_specs,
        out_specs=pl.BlockSpec((tq, ATTN_WIDTH), lambda b, i: (b * nq + i, 0)),
        out_shape=jax.ShapeDtypeStruct((n, ATTN_WIDTH), BF16),
        scratch_shapes=[
            pltpu.VMEM((ATTN_KV_HEADS, n_keys, HEAD_DIM), BF16),
            pltpu.VMEM((ATTN_KV_HEADS, n_keys, 2 * HEAD_DIM), BF16),
        ],
        name="attention",
        compiler_params=_params("parallel", "arbitrary"),
    )(*args)


def _ret_kernel(lg_ref, q_ref, k_ref, v_ref, g_ref, *rest, seq, ctx_len, chunk, has_ctx):
    if has_ctx:
        kc_ref, vc_ref, o_ref, a_ref, s_ref = rest
    else:
        o_ref, a_ref, s_ref = rest
    hd = pl.program_id(1)
    lgf, lgb = lg_ref[0, hd], lg_ref[1, hd]
    c, dk = chunk, RET_DK
    n = seq // c
    scale = dk ** -0.5
    a = lax.broadcasted_iota(jnp.int32, (c, 1), 0).astype(F32)
    wq_f, wq_b = jnp.exp(lgf * (a + 1.0)), jnp.exp(lgb * (c - a))
    wk_f, wk_b = jnp.exp(lgf * (c - 1.0 - a)) * scale, jnp.exp(lgb * a) * scale
    d = (lax.broadcasted_iota(jnp.int32, (c, c), 0) - lax.broadcasted_iota(jnp.int32, (c, c), 1)).astype(F32)
    dmask = jnp.exp(jnp.where(d >= 0.0, lgf * d, -lgb * d)) * jnp.where(d == 0.0, 2.0 * scale, scale)
    zero = jnp.zeros((1, dk), F32)
    gf, gb = jnp.exp(zero + lgf * c), jnp.exp(zero + lgb * c)

    def kv_outer(kr, vr, j):
        k = kr[j * c:(j + 1) * c, :]
        kk = jnp.concatenate([k * wk_f, k * wk_b], axis=1)
        return _dot(kk.T.astype(BF16), vr[j * c:(j + 1) * c, :])

    sf = jnp.zeros((dk, dk), F32)
    sb = jnp.zeros((dk, dk), F32)
    if has_ctx:
        outer = [kv_outer(kc_ref, vc_ref, j) for j in range(ctx_len // c)]
        for o in outer:
            sf = gf * sf + o[:dk]
        for o in reversed(outer):
            sb = gb * sb + o[dk:]
    for i in range(n):
        a_ref[i] = kv_outer(k_ref, v_ref, i)
    for i in range(n):
        s_ref[i, :dk, :] = sf.astype(BF16)
        sf = gf * sf + a_ref[i, :dk, :]
    for i in reversed(range(n)):
        s_ref[i, dk:, :] = sb.astype(BF16)
        sb = gb * sb + a_ref[i, dk:, :]
    for i in range(n):
        rows = slice(i * c, (i + 1) * c)
        q = q_ref[rows, :]
        s = _dot_nt(q, k_ref[rows, :])
        o = _dot((s * dmask).astype(BF16), v_ref[rows, :])
        qq = jnp.concatenate([q * wq_f, q * wq_b], axis=1).astype(BF16)
        o = o + _dot(qq, s_ref[i])
        o_ref[rows, :] = (_silu(g_ref[rows, :].astype(F32)) * _rms(o)).astype(BF16)


def _retention(proj, log_gamma, batch, seq, proj_ctx=None, ctx_len=0):
    n = proj.shape[0]
    has_ctx = proj_ctx is not None
    chunk = 256 if seq % 256 == 0 and ctx_len % 256 == 0 else 128
    in_specs = [
        pl.BlockSpec(memory_space=pltpu.SMEM),
        pl.BlockSpec((seq, RET_DK), lambda b, h: (b, COL_RQ + h)),
        pl.BlockSpec((seq, RET_DK), lambda b, h: (b, COL_RK + h)),
        pl.BlockSpec((seq, RET_DK), lambda b, h: (b, COL_RV + h)),
        pl.BlockSpec((seq, RET_DK), lambda b, h: (b, COL_RG + h)),
    ]
    args = [log_gamma, proj, proj, proj, proj]
    if has_ctx:
        in_specs += [
            pl.BlockSpec((ctx_len, RET_DK), lambda b, h: (b, COL_RK + h)),
            pl.BlockSpec((ctx_len, RET_DK), lambda b, h: (b, COL_RV + h)),
        ]
        args += [proj_ctx, proj_ctx]
    return pl.pallas_call(
        functools.partial(_ret_kernel, seq=seq, ctx_len=ctx_len, chunk=chunk, has_ctx=has_ctx),
        grid=(batch, RET_HEADS),
        in_specs=in_specs,
        out_specs=pl.BlockSpec((seq, RET_DK), lambda b, h: (b, h)),
        out_shape=jax.ShapeDtypeStruct((n, RET_WIDTH), BF16),
        scratch_shapes=[
            pltpu.VMEM((seq // chunk, 2 * RET_DK, RET_DK), F32),
            pltpu.VMEM((seq // chunk, 2 * RET_DK, RET_DK), BF16),
        ],
        name="retention",
        compiler_params=_params("parallel", "parallel"),
    )(*args)


def _dft_cos_sin(n):
    k = np.arange(n, dtype=np.int64)
    ang = 2.0 * np.pi * ((k[:, None] * k[None, :]) % n).astype(np.float64) / n
    return np.cos(ang), np.sin(ang)


def _fourier_consts(seq):
    cg, sg = _dft_cos_sin(FOURIER_GROUP_DIM)
    norm = 1.0 / np.sqrt(float(seq) * FOURIER_GROUP_DIM)
    eye = np.eye(FOURIER_GROUPS)
    chan = np.concatenate([np.kron(eye, cg), np.kron(eye, -sg)], axis=1) * norm
    cs, ss = _dft_cos_sin(seq)
    as_bf16 = lambda a: jnp.asarray(a.astype(np.float32)).astype(BF16)
    return as_bf16(chan), as_bf16(cs), as_bf16(ss)


def _fchan_kernel(z_ref, m_ref, o_ref):
    o_ref[...] = _dot(z_ref[...], m_ref[...]).astype(BF16)


def _fseq_kernel(c_ref, s_ref, zc_ref, zs_ref, o_ref):
    o_ref[...] = (_dot(c_ref[...], zc_ref[...]) + _dot(s_ref[...], zs_ref[...])).astype(BF16)


def _fourier(proj, batch, seq, consts):
    n = proj.shape[0]
    chan, cs, ss = consts
    fw = FOURIER_WIDTH
    tm = _tile(n, (1024, 512, 256, 128))
    zcs = pl.pallas_call(
        _fchan_kernel,
        grid=(n // tm,),
        in_specs=[
            pl.BlockSpec((tm, fw), lambda i: (i, COL_FZ * 128 // fw)),
            pl.BlockSpec((fw, 2 * fw), lambda i: (0, 0)),
        ],
        out_specs=pl.BlockSpec((tm, 2 * fw), lambda i: (i, 0)),
        out_shape=jax.ShapeDtypeStruct((n, 2 * fw), BF16),
        name="fourier_chan",
        compiler_params=_params("parallel"),
    )(proj, chan)
    tk = _tile(seq, (2048, 1024, 512, 256, 128))
    nk = seq // tk
    return pl.pallas_call(
        _fseq_kernel,
        grid=(nk, batch),
        in_specs=[
            pl.BlockSpec((tk, seq), lambda k, b: (k, 0)),
            pl.BlockSpec((tk, seq), lambda k, b: (k, 0)),
            pl.BlockSpec((seq, fw), lambda k, b: (b, 0)),
            pl.BlockSpec((seq, fw), lambda k, b: (b, 1)),
        ],
        out_specs=pl.BlockSpec((tk, fw), lambda k, b: (b * nk + k, 0)),
        out_shape=jax.ShapeDtypeStruct((n, fw), BF16),
        name="fourier_seq",
        compiler_params=_params("parallel", "parallel"),
    )(cs, ss, zcs, zcs)


MERGE_TILE = 512
MERGE_GATE_TILE = 512


def _gate_kernel(u_ref, yf_ref, ya_ref, yr_ref, wf_ref, wa_ref, wr_ref,
                 gf_ref, ga_ref, gr_ref, bf_ref, ba_ref, br_ref, o_ref):
    u = u_ref[...]
    gates = [_dot(u, g[...]) for g in (gf_ref, ga_ref, gr_ref)]
    branches = [_dot(y[...], w[...]) for y, w in ((yf_ref, wf_ref), (ya_ref, wa_ref), (yr_ref, wr_ref))]
    biases = [b[...] for b in (bf_ref, ba_ref, br_ref)]
    for r in range(0, o_ref.shape[0], ACT_CHUNK):
        rows = slice(r, r + ACT_CHUNK)
        terms = [jax.nn.sigmoid(g[rows, :] + b) * y[rows, :] for g, b, y in zip(gates, biases, branches)]
        o_ref[rows, :] = (terms[0] + terms[1] + terms[2]).astype(BF16)


def _out_kernel(h_ref, gt_ref, m_ref, wo_ref, o_ref, *, tn):
    m = m_ref[...]
    for c in range(0, o_ref.shape[1], tn):
        cols = slice(c, c + tn)
        o_ref[:, cols] = h_ref[:, cols] + gt_ref[:, cols] * _dot(m, wo_ref[:, cols])


def _merge(h, mod, row_of_tile, tm, u, y_f, y_a, y_r, w_bf, w_ba, w_br, w_mg, b_mg, w_out, layer):
    n, d = h.shape
    tn = _tile(d, (MERGE_TILE, 256, 128))
    tc = _tile(d, (MERGE_GATE_TILE, 512, 256, 128))
    nn = d // tc
    tg = _tile(n, (1024, 512, 256, 128))
    wspec = lambda rows: pl.BlockSpec((None, rows, tc), lambda i, j: (layer, 0, j))
    gspec = lambda k: pl.BlockSpec((None, d, tc), lambda i, j: (layer, 0, k * nn + j))
    bspec = lambda k: pl.BlockSpec((None, 1, tc), lambda i, j: (layer, 0, k * nn + j))
    yspec = lambda w: pl.BlockSpec((tg, w), lambda i, j: (i, 0))
    m = pl.pallas_call(
        _gate_kernel,
        grid=(n // tg, nn),
        in_specs=[
            yspec(d), yspec(FOURIER_WIDTH), yspec(ATTN_WIDTH), yspec(RET_WIDTH),
            wspec(FOURIER_WIDTH), wspec(ATTN_WIDTH), wspec(RET_WIDTH),
            gspec(0), gspec(1), gspec(2),
            bspec(0), bspec(1), bspec(2),
        ],
        out_specs=pl.BlockSpec((tg, tc), lambda i, j: (i, j)),
        out_shape=jax.ShapeDtypeStruct((n, d), BF16),
        name="merge_gate",
        compiler_params=_params("parallel", "parallel", vmem_limit_bytes=V7X_LARGE_VMEM_LIMIT_BYTES),
    )(u, y_f, y_a, y_r, w_bf, w_ba, w_br, w_mg, w_mg, w_mg, b_mg, b_mg, b_mg)
    return pl.pallas_call(
        functools.partial(_out_kernel, tn=tn),
        grid=(n // tm,),
        in_specs=[
            pl.BlockSpec((tm, d), lambda i: (i, 0)),
            _mod_spec(d, row_of_tile, 5),
            pl.BlockSpec((tm, d), lambda i: (i, 0)),
            pl.BlockSpec((None, d, d), lambda i: (layer, 0, 0)),
        ],
        out_specs=pl.BlockSpec((tm, d), lambda i: (i, 0)),
        out_shape=jax.ShapeDtypeStruct((n, d), F32),
        name="merge_out",
        compiler_params=_params("parallel"),
    )(h, mod, m, w_out)


def kernel(x, c, ctx, c_ctx, w_ada, b_ada, ffn1_norm, ffn1_w_gate, ffn1_w_up, ffn1_w_down, mix_norm, w_in, q_norm, k_norm, ret_decay, w_branch_fourier, w_branch_attn, w_branch_ret, w_merge_gate, b_merge_gate, w_out, ffn2_norm, ffn2_w_gate, ffn2_w_up, ffn2_w_down, final_norm):
    batch, seq, d = x.shape
    ctx_len = ctx.shape[1]
    depth = w_ada.shape[0]
    vec = lambda g: g.reshape(depth, 1, g.shape[-1])

    rows = -(-(batch + 1) // 16) * 16
    c_all = jnp.concatenate([c, c_ctx[None], jnp.zeros((rows - batch - 1, d), F32)], axis=0)
    mod = _adaln(c_all, w_ada, b_ada).reshape(depth * rows, 1, N_MOD * d)

    tm = _tile(seq, (512, 256, 128))
    tmc = _tile(batch * ctx_len, (512, 256, 128))
    tm_ffn = _tile(seq, (FFN_ROWS, 512, 256, 128))
    tmc_ffn = _tile(batch * ctx_len, (FFN_ROWS, 512, 256, 128))

    bf = lambda w: w.astype(BF16)
    ffn1 = (vec(ffn1_norm), bf(ffn1_w_gate), bf(ffn1_w_up), bf(ffn1_w_down))
    ffn2 = (vec(ffn2_norm), bf(ffn2_w_gate), bf(ffn2_w_up), bf(ffn2_w_down))
    mix_gain, w_in_t = vec(mix_norm), bf(w_in)
    qn, kn = vec(q_norm), vec(k_norm)
    w_bf, w_ba, w_br = bf(w_branch_fourier), bf(w_branch_attn), bf(w_branch_ret)
    w_mg, b_mg, w_o = bf(w_merge_gate), vec(b_merge_gate), bf(w_out)
    log_gamma = -jnp.exp(ret_decay.astype(F32))
    tables = _rope_tables(seq)
    four_lat = _fourier_consts(seq)
    four_ctx = _fourier_consts(ctx_len)

    h = x.reshape(batch * seq, d)
    hc = ctx.reshape(batch * ctx_len, d)
    for l in range(depth):
        need_ctx = l < depth - 1
        lat_rows = lambda t, l=l: (lambda i: l * rows + i // (seq // t))
        lat_row = lat_rows(tm)
        ctx_row = lambda i, l=l: l * rows + batch

        h = _ffn(h, mod, lat_rows(tm_ffn), tm_ffn, 0, *ffn1, l)
        hc = _ffn(hc, mod, ctx_row, tmc_ffn, 0, *ffn1, l)

        proj, u = _inproj(h, mod, lat_rows(tm_ffn), tm_ffn, mix_gain, w_in_t, l)
        proj_c, uc = _inproj(hc, mod, ctx_row, tmc_ffn, mix_gain, w_in_t, l)

        y_a = _attention(proj, proj_c, qn, kn, l, batch, seq, ctx_len, proj, tables)
        y_r = _retention(proj, log_gamma[l], batch, seq, proj_c, ctx_len)
        y_f = _fourier(proj, batch, seq, four_lat)
        h = _merge(h, mod, lat_row, tm, u, y_f, y_a, y_r, w_bf, w_ba, w_br, w_mg, b_mg, w_o, l)
        last = final_norm.reshape(1, d) if l == depth - 1 else None
        h = _ffn(h, mod, lat_rows(tm_ffn), tm_ffn, 6, *ffn2, l, final_gain=last)

        if need_ctx:
            yc_a = _attention(proj_c, proj_c, qn, kn, l, batch, ctx_len, ctx_len)
            yc_r = _retention(proj_c, log_gamma[l], batch, ctx_len)
            yc_f = _fourier(proj_c, batch, ctx_len, four_ctx)
            hc = _merge(hc, mod, ctx_row, tmc, uc, yc_f, yc_a, yc_r, w_bf, w_ba, w_br, w_mg, b_mg, w_o, l)
            hc = _ffn(hc, mod, ctx_row, tmc_ffn, 6, *ffn2, l)
    return h.reshape(batch, seq, d)
```

```python
import functools

import numpy as np
import jax
import jax.numpy as jnp
from jax import lax
from jax.experimental import pallas as pl
from jax.experimental.pallas import tpu as pltpu

F32 = jnp.float32
BF16 = jnp.bfloat16

EPS = 1e-6
N_MOD = 9
GRID_W = 64
HEAD_DIM = 128
HALF_ROT = HEAD_DIM // 2
ATTN_HEADS = 8
ATTN_KV_HEADS = 2
ATTN_GROUP = ATTN_HEADS // ATTN_KV_HEADS
ATTN_WIDTH = ATTN_HEADS * HEAD_DIM
KV_WIDTH = ATTN_KV_HEADS * HEAD_DIM
RET_HEADS = 4
RET_DK = 128
RET_WIDTH = RET_HEADS * RET_DK
FOURIER_GROUPS = 4
FOURIER_GROUP_DIM = 128
FOURIER_WIDTH = FOURIER_GROUPS * FOURIER_GROUP_DIM
IN_WIDTH = ATTN_WIDTH + 2 * KV_WIDTH + 4 * RET_WIDTH + FOURIER_WIDTH
ROPE_THETA = 10000.0

COL_AK = ATTN_WIDTH // 128
COL_AV = COL_AK + KV_WIDTH // 128
COL_RQ = COL_AV + KV_WIDTH // 128
COL_RK = COL_RQ + RET_WIDTH // 128
COL_RV = COL_RK + RET_WIDTH // 128
COL_RG = COL_RV + RET_WIDTH // 128
COL_FZ = COL_RG + RET_WIDTH // 128

IN_TILE = 2048

ATTN_Q_SCALE = HEAD_DIM ** -0.5 * float(np.log2(np.e))

V7X_VMEM_LIMIT_BYTES = 56 * 1024 * 1024
V7X_LARGE_VMEM_LIMIT_BYTES = 60 * 1024 * 1024


def _params(*sem, vmem_limit_bytes=V7X_VMEM_LIMIT_BYTES):
    return pltpu.CompilerParams(dimension_semantics=sem, vmem_limit_bytes=vmem_limit_bytes)


def _tile(n, prefs):
    for p in prefs:
        if n % p == 0:
            return p
    return n


def _dot(a, b):
    return jnp.dot(a, b, preferred_element_type=F32)


def _dot_nt(a, b):
    return lax.dot_general(a, b, (((1,), (1,)), ((), ())), preferred_element_type=F32)


def _rms(x):
    return x * lax.rsqrt(jnp.mean(x * x, axis=-1, keepdims=True) + EPS)


def _silu(x):
    return x * jax.nn.sigmoid(x)


def _adaln_kernel(c_ref, w_ref, b_ref, o_ref):
    a = _silu(c_ref[...]).astype(BF16)
    o_ref[...] = _dot(a, w_ref[...].astype(BF16)) + b_ref[...]


def _adaln(c_all, w_ada, b_ada):
    depth, d, nd = w_ada.shape
    r = c_all.shape[0]
    tn = _tile(nd, (1024, 512, 256, 128))
    return pl.pallas_call(
        _adaln_kernel,
        grid=(depth, nd // tn),
        in_specs=[
            pl.BlockSpec((r, d), lambda l, j: (0, 0)),
            pl.BlockSpec((None, d, tn), lambda l, j: (l, 0, j)),
            pl.BlockSpec((None, 1, tn), lambda l, j: (l, 0, j)),
        ],
        out_specs=pl.BlockSpec((None, r, tn), lambda l, j: (l, 0, j)),
        out_shape=jax.ShapeDtypeStruct((depth, r, nd), F32),
        name="adaln",
        compiler_params=_params("parallel", "parallel"),
    )(c_all, w_ada, b_ada.reshape(depth, 1, nd))


def _mod_spec(d, row_of_tile, k):
    return pl.BlockSpec((None, 1, d), lambda i, *_: (row_of_tile(i), 0, k))


FFN_TILE = 512
FFN_ROWS = 1024
ACT_CHUNK = 32
ROW_CHUNK = 16
ROW_UNROLL = 4


def _row_chunks(tm, dst_ref, body, read_first=False):
    def trip(t, carry):
        rows = [pl.ds(pl.multiple_of((t * ROW_UNROLL + k) * ROW_CHUNK, ROW_CHUNK), ROW_CHUNK)
                for k in range(ROW_UNROLL)]
        if read_first:
            for r, v in zip(rows, [body(r) for r in rows]):
                dst_ref[r, :] = v
        else:
            for r in rows:
                dst_ref[r, :] = body(r)
        return carry

    lax.fori_loop(0, tm // (ROW_CHUNK * ROW_UNROLL), trip, 0)


def _ffn_kernel(h_ref, sh_ref, sc_ref, gt_ref, gain_ref, wg_ref, wu_ref, wd_ref, *rest, nf, final):
    if final:
        fg_ref, o_ref, xn_ref, a_ref = rest
    else:
        o_ref, xn_ref, a_ref = rest
    acc_ref = o_ref
    f = pl.program_id(1)
    tm = h_ref.shape[0]

    @pl.when(f == 0)
    def _():
        gs, sh = gain_ref[...] * (1.0 + sc_ref[...]), sh_ref[...]

        _row_chunks(tm, xn_ref, lambda rows: (_rms(h_ref[rows, :]) * gs + sh).astype(BF16))

    @pl.when((pl.program_id(0) < 2) & (f == 0))
    def _():
        acc_ref[...] = jnp.zeros_like(acc_ref)

    xn = xn_ref[...]
    g = _dot(xn, wg_ref[...])
    u = _dot(xn, wu_ref[...])
    for r in range(0, tm, ACT_CHUNK):
        rows = slice(r, r + ACT_CHUNK)
        a_ref[rows, :] = (_silu(g[rows, :]) * u[rows, :]).astype(BF16)
    acc_ref[...] = jnp.where(f == 0, 0.0, acc_ref[...]) + _dot(a_ref[...], wd_ref[...])

    @pl.when(f == nf - 1)
    def _():
        half_gate = 0.5 * gt_ref[...]
        if final:
            fg = fg_ref[...]

        def chunk(rows):
            y = h_ref[rows, :] + half_gate * acc_ref[rows, :]
            return _rms(y) * fg if final else y

        _row_chunks(tm, o_ref, chunk, read_first=final)


def _ffn(h, mod, row_of_tile, tm, k0, gain, wg, wu, wd, layer, final_gain=None):
    n, d = h.shape
    tf = _tile(wg.shape[-1], (FFN_TILE, 256, 128))
    nf = wg.shape[-1] // tf
    final = final_gain is not None
    vec_spec = pl.BlockSpec((None, 1, d), lambda i, f: (layer, 0, 0))
    in_specs = [
        pl.BlockSpec((tm, d), lambda i, f: (i, 0)),
        _mod_spec(d, row_of_tile, k0),
        _mod_spec(d, row_of_tile, k0 + 1),
        _mod_spec(d, row_of_tile, k0 + 2),
        vec_spec,
        pl.BlockSpec((None, d, tf), lambda i, f: (layer, 0, f)),
        pl.BlockSpec((None, d, tf), lambda i, f: (layer, 0, f)),
        pl.BlockSpec((None, tf, d), lambda i, f: (layer, f, 0)),
    ]
    args = [h, mod, mod, mod, gain, wg, wu, wd]
    if final:
        in_specs.append(pl.BlockSpec((1, d), lambda i, f: (0, 0)))
        args.append(final_gain)
    return pl.pallas_call(
        functools.partial(_ffn_kernel, nf=nf, final=final),
        grid=(n // tm, nf),
        in_specs=in_specs,
        out_specs=pl.BlockSpec((tm, d), lambda i, f: (i, 0)),
        out_shape=jax.ShapeDtypeStruct((n, d), F32),
        scratch_shapes=[pltpu.VMEM((tm, d), BF16), pltpu.VMEM((tm, tf), BF16)],
        name="ffn",
        compiler_params=_params("arbitrary", "arbitrary", vmem_limit_bytes=V7X_LARGE_VMEM_LIMIT_BYTES),
    )(*args)


def _inproj_kernel(h_ref, sh_ref, sc_ref, gain_ref, w_ref, o_ref, u_ref):
    @pl.when(pl.program_id(1) == 0)
    def _():
        gs, sh = gain_ref[...] * (1.0 + sc_ref[...]), sh_ref[...]
        _row_chunks(h_ref.shape[0], u_ref, lambda rows: (_rms(h_ref[rows, :]) * gs + sh).astype(BF16))

    o_ref[...] = _dot(u_ref[...], w_ref[...]).astype(BF16)


def _inproj(h, mod, row_of_tile, tm, gain, w_in, layer):
    n, d = h.shape
    tn = IN_TILE
    assert w_in.shape[-1] == IN_WIDTH and IN_WIDTH % tn == 0
    return pl.pallas_call(
        _inproj_kernel,
        grid=(n // tm, IN_WIDTH // tn),
        in_specs=[
            pl.BlockSpec((tm, d), lambda i, j: (i, 0)),
            _mod_spec(d, row_of_tile, 3),
            _mod_spec(d, row_of_tile, 4),
            pl.BlockSpec((None, 1, d), lambda i, j: (layer, 0, 0)),
            pl.BlockSpec((None, d, tn), lambda i, j: (layer, 0, j)),
        ],
        out_specs=[
            pl.BlockSpec((tm, tn), lambda i, j: (i, j)),
            pl.BlockSpec((tm, d), lambda i, j: (i, 0)),
        ],
        out_shape=[
            jax.ShapeDtypeStruct((n, IN_WIDTH), BF16),
            jax.ShapeDtypeStruct((n, d), BF16),
        ],
        name="inproj",
        compiler_params=_params("parallel", "arbitrary", vmem_limit_bytes=V7X_LARGE_VMEM_LIMIT_BYTES),
    )(h, mod, mod, gain, w_in)


def _rope_tables(seq):
    t = jnp.arange(seq)
    row = (t // GRID_W).astype(F32)
    col = (t % GRID_W).astype(F32)
    inv = ROPE_THETA ** (-jnp.arange(0, HALF_ROT, 2, dtype=F32) / HALF_ROT)
    ar, ac = row[:, None] * inv[None], col[:, None] * inv[None]
    zero = jnp.zeros_like(ar)
    cos = jnp.concatenate([jnp.cos(ar), jnp.cos(ar), jnp.cos(ac), jnp.cos(ac)], axis=-1)
    sin_lo = jnp.concatenate([-jnp.sin(ar), zero, -jnp.sin(ac), zero], axis=-1)
    sin_hi = jnp.concatenate([zero, jnp.sin(ar), zero, jnp.sin(ac)], axis=-1)
    return cos, sin_lo, sin_hi


def _rope(x, cos, sin_lo, sin_hi):
    return x * cos + pltpu.roll(x, HEAD_DIM - HALF_ROT // 2, 1) * sin_lo + pltpu.roll(x, HALF_ROT // 2, 1) * sin_hi


def _attn_kernel(q_ref, kc_ref, vc_ref, qn_ref, kn_ref, *rest, has_lat):
    if has_lat:
        kl_ref, vl_ref, qcos_ref, qlo_ref, qhi_ref, kcos_ref, klo_ref, khi_ref, o_ref, k_ref, v_ref = rest
    else:
        o_ref, k_ref, v_ref = rest
    nc = kc_ref.shape[0]

    @pl.when(pl.program_id(1) == 0)
    def _():
        kn = kn_ref[...]
        for kv in range(ATTN_KV_HEADS):
            sl = slice(kv * HEAD_DIM, (kv + 1) * HEAD_DIM)
            k_ref[kv, :nc, :] = (_rms(kc_ref[:, sl].astype(F32)) * kn).astype(BF16)
            v_ref[kv, :nc, :HEAD_DIM] = vc_ref[:, sl]
            if has_lat:
                kl = _rms(kl_ref[:, sl].astype(F32)) * kn
                k_ref[kv, nc:, :] = _rope(kl, kcos_ref[...], klo_ref[...], khi_ref[...]).astype(BF16)
                v_ref[kv, nc:, :HEAD_DIM] = vl_ref[:, sl]
            v_ref[kv, :, HEAD_DIM:] = jnp.ones((v_ref.shape[1], HEAD_DIM), BF16)

    qn = qn_ref[...]
    for hd in range(ATTN_HEADS):
        sl = slice(hd * HEAD_DIM, (hd + 1) * HEAD_DIM)
        q = _rms(q_ref[:, sl].astype(F32)) * qn
        if has_lat:
            q = _rope(q, qcos_ref[...], qlo_ref[...], qhi_ref[...])
        s = _dot_nt((q * ATTN_Q_SCALE).astype(BF16), k_ref[hd // ATTN_GROUP])
        p = jnp.exp2(s - jnp.max(s, axis=-1, keepdims=True)).astype(BF16)
        ov = _dot(p, v_ref[hd // ATTN_GROUP])
        o_ref[:, sl] = (ov[:, :HEAD_DIM] / ov[:, HEAD_DIM:]).astype(BF16)


def _attention(proj_q, proj_c, q_norm, k_norm, layer, batch, seq, ctx_len, proj_kv=None, tables=None):
    n = proj_q.shape[0]
    tq = _tile(seq, (256, 128))
    nq = seq // tq
    has_lat = proj_kv is not None
    n_keys = ctx_len + (seq if has_lat else 0)
    col_k, col_v = COL_AK * 128 // KV_WIDTH, COL_AV * 128 // KV_WIDTH
    norm_spec = pl.BlockSpec((None, 1, HEAD_DIM), lambda b, i: (layer, 0, 0))
    in_specs = [
        pl.BlockSpec((tq, ATTN_WIDTH), lambda b, i: (b * nq + i, 0)),
        pl.BlockSpec((ctx_len, KV_WIDTH), lambda b, i: (b, col_k)),
        pl.BlockSpec((ctx_len, KV_WIDTH), lambda b, i: (b, col_v)),
        norm_spec,
        norm_spec,
    ]
    args = [proj_q, proj_c, proj_c, q_norm, k_norm]
    if has_lat:
        in_specs += [
            pl.BlockSpec((seq, KV_WIDTH), lambda b, i: (b, col_k)),
            pl.BlockSpec((seq, KV_WIDTH), lambda b, i: (b, col_v)),
        ]
        in_specs += [pl.BlockSpec((tq, HEAD_DIM), lambda b, i: (i, 0))] * 3
        in_specs += [pl.BlockSpec((seq, HEAD_DIM), lambda b, i: (0, 0))] * 3
        args += [proj_kv, proj_kv, *tables, *tables]
    return pl.pallas_call(
        functools.partial(_attn_kernel, has_lat=has_lat),
        grid=(batch, nq),
        in_specs=in_specs,
        out_specs=pl.BlockSpec((tq, ATTN_WIDTH), lambda b, i: (b * nq + i, 0)),
        out_shape=jax.ShapeDtypeStruct((n, ATTN_WIDTH), BF16),
        scratch_shapes=[
            pltpu.VMEM((ATTN_KV_HEADS, n_keys, HEAD_DIM), BF16),
            pltpu.VMEM((ATTN_KV_HEADS, n_keys, 2 * HEAD_DIM), BF16),
        ],
        name="attention",
        compiler_params=_params("parallel", "arbitrary"),
    )(*args)


def _ret_kernel(lg_ref, q_ref, k_ref, v_ref, g_ref, *rest, seq, ctx_len, chunk, has_ctx):
    if has_ctx:
        kc_ref, vc_ref, o_ref, a_ref, s_ref = rest
    else:
        o_ref, a_ref, s_ref = rest
    hd = pl.program_id(1)
    lgf, lgb = lg_ref[0, hd], lg_ref[1, hd]
    c, dk = chunk, RET_DK
    n = seq // c
    scale = dk ** -0.5
    a = lax.broadcasted_iota(jnp.int32, (c, 1), 0).astype(F32)
    wq_f, wq_b = jnp.exp(lgf * (a + 1.0)), jnp.exp(lgb * (c - a))
    wk_f, wk_b = jnp.exp(lgf * (c - 1.0 - a)) * scale, jnp.exp(lgb * a) * scale
    d = (lax.broadcasted_iota(jnp.int32, (c, c), 0) - lax.broadcasted_iota(jnp.int32, (c, c), 1)).astype(F32)
    dmask = jnp.exp(jnp.where(d >= 0.0, lgf * d, -lgb * d)) * jnp.where(d == 0.0, 2.0 * scale, scale)
    zero = jnp.zeros((1, dk), F32)
    gf, gb = jnp.exp(zero + lgf * c), jnp.exp(zero + lgb * c)

    def kv_outer(kr, vr, j):
        k = kr[j * c:(j + 1) * c, :]
        kk = jnp.concatenate([k * wk_f, k * wk_b], axis=1)
        return _dot(kk.T.astype(BF16), vr[j * c:(j + 1) * c, :])

    sf = jnp.zeros((dk, dk), F32)
    sb = jnp.zeros((dk, dk), F32)
    if has_ctx:
        outer = [kv_outer(kc_ref, vc_ref, j) for j in range(ctx_len // c)]
        for o in outer:
            sf = gf * sf + o[:dk]
        for o in reversed(outer):
            sb = gb * sb + o[dk:]
    for i in range(n):
        a_ref[i] = kv_outer(k_ref, v_ref, i)
    for i in range(n):
        s_ref[i, :dk, :] = sf.astype(BF16)
        sf = gf * sf + a_ref[i, :dk, :]
    for i in reversed(range(n)):
        s_ref[i, dk:, :] = sb.astype(BF16)
        sb = gb * sb + a_ref[i, dk:, :]
    for i in range(n):
        rows = slice(i * c, (i + 1) * c)
        q = q_ref[rows, :]
        s = _dot_nt(q, k_ref[rows, :])
        o = _dot((s * dmask).astype(BF16), v_ref[rows, :])
        qq = jnp.concatenate([q * wq_f, q * wq_b], axis=1).astype(BF16)
        o = o + _dot(qq, s_ref[i])
        o_ref[rows, :] = (_silu(g_ref[rows, :].astype(F32)) * _rms(o)).astype(BF16)


def _retention(proj, log_gamma, batch, seq, proj_ctx=None, ctx_len=0):
    n = proj.shape[0]
    has_ctx = proj_ctx is not None
    chunk = 256 if seq % 256 == 0 and ctx_len % 256 == 0 else 128
    in_specs = [
        pl.BlockSpec(memory_space=pltpu.SMEM),
        pl.BlockSpec((seq, RET_DK), lambda b, h: (b, COL_RQ + h)),
        pl.BlockSpec((seq, RET_DK), lambda b, h: (b, COL_RK + h)),
        pl.BlockSpec((seq, RET_DK), lambda b, h: (b, COL_RV + h)),
        pl.BlockSpec((seq, RET_DK), lambda b, h: (b, COL_RG + h)),
    ]
    args = [log_gamma, proj, proj, proj, proj]
    if has_ctx:
        in_specs += [
            pl.BlockSpec((ctx_len, RET_DK), lambda b, h: (b, COL_RK + h)),
            pl.BlockSpec((ctx_len, RET_DK), lambda b, h: (b, COL_RV + h)),
        ]
        args += [proj_ctx, proj_ctx]
    return pl.pallas_call(
        functools.partial(_ret_kernel, seq=seq, ctx_len=ctx_len, chunk=chunk, has_ctx=has_ctx),
        grid=(batch, RET_HEADS),
        in_specs=in_specs,
        out_specs=pl.BlockSpec((seq, RET_DK), lambda b, h: (b, h)),
        out_shape=jax.ShapeDtypeStruct((n, RET_WIDTH), BF16),
        scratch_shapes=[
            pltpu.VMEM((seq // chunk, 2 * RET_DK, RET_DK), F32),
            pltpu.VMEM((seq // chunk, 2 * RET_DK, RET_DK), BF16),
        ],
        name="retention",
        compiler_params=_params("parallel", "parallel"),
    )(*args)


def _dft_cos_sin(n):
    k = np.arange(n, dtype=np.int64)
    ang = 2.0 * np.pi * ((k[:, None] * k[None, :]) % n).astype(np.float64) / n
    return np.cos(ang), np.sin(ang)


def _fourier_consts(seq):
    cg, sg = _dft_cos_sin(FOURIER_GROUP_DIM)
    norm = 1.0 / np.sqrt(float(seq) * FOURIER_GROUP_DIM)
    eye = np.eye(FOURIER_GROUPS)
    chan = np.concatenate([np.kron(eye, cg), np.kron(eye, -sg)], axis=1) * norm
    cs, ss = _dft_cos_sin(seq)
    as_bf16 = lambda a: jnp.asarray(a.astype(np.float32)).astype(BF16)
    return as_bf16(chan), as_bf16(cs), as_bf16(ss)


def _fchan_kernel(z_ref, m_ref, o_ref):
    o_ref[...] = _dot(z_ref[...], m_ref[...]).astype(BF16)


def _fseq_kernel(c_ref, s_ref, zc_ref, zs_ref, o_ref):
    o_ref[...] = (_dot(c_ref[...], zc_ref[...]) + _dot(s_ref[...], zs_ref[...])).astype(BF16)


def _fourier(proj, batch, seq, consts):
    n = proj.shape[0]
    chan, cs, ss = consts
    fw = FOURIER_WIDTH
    tm = _tile(n, (1024, 512, 256, 128))
    zcs = pl.pallas_call(
        _fchan_kernel,
        grid=(n // tm,),
        in_specs=[
            pl.BlockSpec((tm, fw), lambda i: (i, COL_FZ * 128 // fw)),
            pl.BlockSpec((fw, 2 * fw), lambda i: (0, 0)),
        ],
        out_specs=pl.BlockSpec((tm, 2 * fw), lambda i: (i, 0)),
        out_shape=jax.ShapeDtypeStruct((n, 2 * fw), BF16),
        name="fourier_chan",
        compiler_params=_params("parallel"),
    )(proj, chan)
    tk = _tile(seq, (1024, 512, 256, 128))
    nk = seq // tk
    return pl.pallas_call(
        _fseq_kernel,
        grid=(nk, batch),
        in_specs=[
            pl.BlockSpec((tk, seq), lambda k, b: (k, 0)),
            pl.BlockSpec((tk, seq), lambda k, b: (k, 0)),
            pl.BlockSpec((seq, fw), lambda k, b: (b, 0)),
            pl.BlockSpec((seq, fw), lambda k, b: (b, 1)),
        ],
        out_specs=pl.BlockSpec((tk, fw), lambda k, b: (b * nk + k, 0)),
        out_shape=jax.ShapeDtypeStruct((n, fw), BF16),
        name="fourier_seq",
        compiler_params=_params("parallel", "parallel"),
    )(cs, ss, zcs, zcs)


MERGE_TILE = 512
MERGE_GATE_TILE = 512


def _gate_kernel(u_ref, yf_ref, ya_ref, yr_ref, wf_ref, wa_ref, wr_ref,
                 gf_ref, ga_ref, gr_ref, bf_ref, ba_ref, br_ref, o_ref):
    u = u_ref[...]
    gates = [_dot(u, g[...]) for g in (gf_ref, ga_ref, gr_ref)]
    branches = [_dot(y[...], w[...]) for y, w in ((yf_ref, wf_ref), (ya_ref, wa_ref), (yr_ref, wr_ref))]
    biases = [b[...] for b in (bf_ref, ba_ref, br_ref)]
    for r in range(0, o_ref.shape[0], ACT_CHUNK):
        rows = slice(r, r + ACT_CHUNK)
        terms = [jax.nn.sigmoid(g[rows, :] + b) * y[rows, :] for g, b, y in zip(gates, biases, branches)]
        o_ref[rows, :] = (terms[0] + terms[1] + terms[2]).astype(BF16)


def _out_kernel(h_ref, gt_ref, m_ref, wo_ref, o_ref, *, tn):
    m = m_ref[...]
    for c in range(0, o_ref.shape[1], tn):
        cols = slice(c, c + tn)
        o_ref[:, cols] = h_ref[:, cols] + gt_ref[:, cols] * _dot(m, wo_ref[:, cols])


def _merge(h, mod, row_of_tile, tm, u, y_f, y_a, y_r, w_bf, w_ba, w_br, w_mg, b_mg, w_out, layer):
    n, d = h.shape
    tn = _tile(d, (MERGE_TILE, 256, 128))
    tc = _tile(d, (MERGE_GATE_TILE, 512, 256, 128))
    nn = d // tc
    tg = _tile(n, (1024, 512, 256, 128))
    wspec = lambda rows: pl.BlockSpec((None, rows, tc), lambda i, j: (layer, 0, j))
    gspec = lambda k: pl.BlockSpec((None, d, tc), lambda i, j: (layer, 0, k * nn + j))
    bspec = lambda k: pl.BlockSpec((None, 1, tc), lambda i, j: (layer, 0, k * nn + j))
    yspec = lambda w: pl.BlockSpec((tg, w), lambda i, j: (i, 0))
    m = pl.pallas_call(
        _gate_kernel,
        grid=(n // tg, nn),
        in_specs=[
            yspec(d), yspec(FOURIER_WIDTH), yspec(ATTN_WIDTH), yspec(RET_WIDTH),
            wspec(FOURIER_WIDTH), wspec(ATTN_WIDTH), wspec(RET_WIDTH),
            gspec(0), gspec(1), gspec(2),
            bspec(0), bspec(1), bspec(2),
        ],
        out_specs=pl.BlockSpec((tg, tc), lambda i, j: (i, j)),
        out_shape=jax.ShapeDtypeStruct((n, d), BF16),
        name="merge_gate",
        compiler_params=_params("parallel", "parallel", vmem_limit_bytes=V7X_LARGE_VMEM_LIMIT_BYTES),
    )(u, y_f, y_a, y_r, w_bf, w_ba, w_br, w_mg, w_mg, w_mg, b_mg, b_mg, b_mg)
    return pl.pallas_call(
        functools.partial(_out_kernel, tn=tn),
        grid=(n // tm,),
        in_specs=[
            pl.BlockSpec((tm, d), lambda i: (i, 0)),
            _mod_spec(d, row_of_tile, 5),
            pl.BlockSpec((tm, d), lambda i: (i, 0)),
            pl.BlockSpec((None, d, d), lambda i: (layer, 0, 0)),
        ],
        out_specs=pl.BlockSpec((tm, d), lambda i: (i, 0)),
        out_shape=jax.ShapeDtypeStruct((n, d), F32),
        name="merge_out",
        compiler_params=_params("parallel"),
    )(h, mod, m, w_out)


def kernel(x, c, ctx, c_ctx, w_ada, b_ada, ffn1_norm, ffn1_w_gate, ffn1_w_up, ffn1_w_down, mix_norm, w_in, q_norm, k_norm, ret_decay, w_branch_fourier, w_branch_attn, w_branch_ret, w_merge_gate, b_merge_gate, w_out, ffn2_norm, ffn2_w_gate, ffn2_w_up, ffn2_w_down, final_norm):
    batch, seq, d = x.shape
    ctx_len = ctx.shape[1]
    depth = w_ada.shape[0]
    vec = lambda g: g.reshape(depth, 1, g.shape[-1])

    rows = -(-(batch + 1) // 16) * 16
    c_all = jnp.concatenate([c, c_ctx[None], jnp.zeros((rows - batch - 1, d), F32)], axis=0)
    mod = _adaln(c_all, w_ada, b_ada).reshape(depth * rows, 1, N_MOD * d)

    tm = _tile(seq, (512, 256, 128))
    tmc = _tile(batch * ctx_len, (512, 256, 128))
    tm_ffn = _tile(seq, (FFN_ROWS, 512, 256, 128))
    tmc_ffn = _tile(batch * ctx_len, (FFN_ROWS, 512, 256, 128))

    bf = lambda w: w.astype(BF16)
    ffn1 = (vec(ffn1_norm), bf(ffn1_w_gate), bf(ffn1_w_up), bf(ffn1_w_down))
    ffn2 = (vec(ffn2_norm), bf(ffn2_w_gate), bf(ffn2_w_up), bf(ffn2_w_down))
    mix_gain, w_in_t = vec(mix_norm), bf(w_in)
    qn, kn = vec(q_norm), vec(k_norm)
    w_bf, w_ba, w_br = bf(w_branch_fourier), bf(w_branch_attn), bf(w_branch_ret)
    w_mg, b_mg, w_o = bf(w_merge_gate), vec(b_merge_gate), bf(w_out)
    log_gamma = -jnp.exp(ret_decay.astype(F32))
    tables = _rope_tables(seq)
    four_lat = _fourier_consts(seq)
    four_ctx = _fourier_consts(ctx_len)

    h = x.reshape(batch * seq, d)
    hc = ctx.reshape(batch * ctx_len, d)
    for l in range(depth):
        need_ctx = l < depth - 1
        lat_rows = lambda t, l=l: (lambda i: l * rows + i // (seq // t))
        lat_row = lat_rows(tm)
        ctx_row = lambda i, l=l: l * rows + batch

        h = _ffn(h, mod, lat_rows(tm_ffn), tm_ffn, 0, *ffn1, l)
        hc = _ffn(hc, mod, ctx_row, tmc_ffn, 0, *ffn1, l)

        proj, u = _inproj(h, mod, lat_rows(tm_ffn), tm_ffn, mix_gain, w_in_t, l)
        proj_c, uc = _inproj(hc, mod, ctx_row, tmc_ffn, mix_gain, w_in_t, l)

        y_a = _attention(proj, proj_c, qn, kn, l, batch, seq, ctx_len, proj, tables)
        y_r = _retention(proj, log_gamma[l], batch, seq, proj_c, ctx_len)
        y_f = _fourier(proj, batch, seq, four_lat)
        h = _merge(h, mod, lat_row, tm, u, y_f, y_a, y_r, w_bf, w_ba, w_br, w_mg, b_mg, w_o, l)
        last = final_norm.reshape(1, d) if l == depth - 1 else None
        h = _ffn(h, mod, lat_rows(tm_ffn), tm_ffn, 6, *ffn2, l, final_gain=last)

        if need_ctx:
            yc_a = _attention(proj_c, proj_c, qn, kn, l, batch, ctx_len, ctx_len)
            yc_r = _retention(proj_c, log_gamma[l], batch, ctx_len)
            yc_f = _fourier(proj_c, batch, ctx_len, four_ctx)
            hc = _merge(hc, mod, ctx_row, tmc, uc, yc_f, yc_a, yc_r, w_bf, w_ba, w_br, w_mg, b_mg, w_o, l)
            hc = _ffn(hc, mod, ctx_row, tmc_ffn, 6, *ffn2, l)
    return h.reshape(batch, seq, d)
```
